```python
import math
import jax, jax.numpy as jnp
from jax import lax
import numpy as np

D_MODEL = 1024
BATCH = 1
SEQ = 16384
DEPTH = 1
DEC_BATCH = 128
DEC_SEQ = 8
PAST_LEN = 16384
PAGE_SIZE = 128

N_Q_HEADS = 8
N_KV_HEADS = 2
HEAD_DIM = 64
Q_PER_KV = N_Q_HEADS // N_KV_HEADS
ATTN_WIDTH = N_Q_HEADS * HEAD_DIM
KV_WIDTH = N_KV_HEADS * HEAD_DIM
WINDOW = 128
ROPE_THETA = 10000.0
SSM_WIDTH = D_MODEL // 2
SSM_GROUP = 16
N_SSM_GROUPS = SSM_WIDTH // SSM_GROUP
SSM_STATE = 64
DT_MIN = 0.001
DT_MAX = 0.1
N_MEM = 256
N_X_HEADS = 4
X_HEAD_DIM = D_MODEL // N_X_HEADS
D_FF = 2816
RMS_EPS = 1e-6
NEG_INF = -1e30
IN_SPLITS = (ATTN_WIDTH, KV_WIDTH, KV_WIDTH, SSM_WIDTH, D_MODEL, D_MODEL)
IN_WIDTH = sum(IN_SPLITS)

kernel_name = 'hybrid_swa_sink_s5_macaron_decoder_step'


def rms_norm(x, g):
    xf = x.astype(jnp.float32)
    y = xf * lax.rsqrt(jnp.mean(xf * xf, axis=-1, keepdims=True) + RMS_EPS) * g.astype(jnp.float32)
    return y.astype(x.dtype)


def swiglu_ffn(h, w_in, w_out):
    a, b = jnp.split(h @ w_in, 2, axis=-1)
    return (jax.nn.silu(a) * b) @ w_out


def rope(x, pos):
    half = HEAD_DIM // 2
    inv = ROPE_THETA ** (-jnp.arange(half, dtype=jnp.float32) / half)
    ang = pos[:, None] * inv[None, :]
    cos = jnp.cos(ang)[None, :, None, :]
    sin = jnp.sin(ang)[None, :, None, :]
    xf = x.astype(jnp.float32)
    x1, x2 = xf[..., :half], xf[..., half:]
    return jnp.concatenate([x1 * cos - x2 * sin, x2 * cos + x1 * sin], axis=-1).astype(x.dtype)


def band_window_attention(q, k_ctx, v_ctx, k_cur, v_cur, ctx_valid, sinks):
    l = q.shape[1]
    k = jnp.concatenate([k_ctx, k_cur], axis=1)
    v = jnp.concatenate([v_ctx, v_cur], axis=1)
    s = jnp.einsum('nqhgd,nkhd->nhgqk', q, k).astype(jnp.float32) * (HEAD_DIM ** -0.5)
    r = jnp.arange(l)[:, None]
    c = jnp.arange(WINDOW + l)[None, :]
    band = (c >= r) & (c <= WINDOW + r)
    avail = (c >= WINDOW) | ctx_valid[:, None, None]
    mask = (band[None] & avail)[:, None, None]
    s = jnp.where(mask, s, NEG_INF)
    sink = sinks.astype(jnp.float32).reshape(1, N_KV_HEADS, Q_PER_KV, 1, 1)
    m = jnp.maximum(jnp.max(s, axis=-1, keepdims=True), sink)
    p = jnp.exp(s - m)
    p = p / (jnp.sum(p, axis=-1, keepdims=True) + jnp.exp(sink - m))
    return jnp.einsum('nhgqk,nkhd->nqhgd', p.astype(v.dtype), v)


def _complex_linear_combine(e1, e2):
    a1r, a1i, b1r, b1i = e1
    a2r, a2i, b2r, b2i = e2
    return (a2r * a1r - a2i * a1i,
            a2r * a1i + a2i * a1r,
            a2r * b1r - a2i * b1i + b2r,
            a2r * b1i + a2i * b1r + b2i)


def s5_scan(u, h0_re, h0_im, a_re, a_im, log_dt, b_re, b_im, c_re, c_im, d_skip):
    f32 = jnp.float32
    a_re = a_re.astype(f32)
    a_im = a_im.astype(f32)
    dt = jnp.exp(log_dt.astype(f32))[:, None]
    mag = jnp.exp(a_re * dt)
    lb_re = mag * jnp.cos(a_im * dt)
    lb_im = mag * jnp.sin(a_im * dt)
    den = a_re * a_re + a_im * a_im
    nr = lb_re - 1.0
    ni = lb_im
    k_re = (nr * a_re + ni * a_im) / den
    k_im = (ni * a_re - nr * a_im) / den
    b_re = b_re.astype(f32)
    b_im = b_im.astype(f32)
    bb_re = k_re[..., None] * b_re - k_im[..., None] * b_im
    bb_im = k_re[..., None] * b_im + k_im[..., None] * b_re
    x_re = jnp.einsum('btgc,gnc->btgn', u, bb_re)
    x_im = jnp.einsum('btgc,gnc->btgn', u, bb_im)
    x_re = x_re.at[:, 0].add(lb_re * h0_re - lb_im * h0_im)
    x_im = x_im.at[:, 0].add(lb_re * h0_im + lb_im * h0_re)
    a_r = jnp.broadcast_to(lb_re, x_re.shape)
    a_i = jnp.broadcast_to(lb_im, x_re.shape)
    _, _, h_re, h_im = lax.associative_scan(_complex_linear_combine, (a_r, a_i, x_re, x_im), axis=1)
    y = (jnp.einsum('btgn,gcn->btgc', h_re, c_re.astype(f32))
         - jnp.einsum('btgn,gcn->btgc', h_im, c_im.astype(f32))
         + d_skip.astype(f32).reshape(N_SSM_GROUPS, SSM_GROUP) * u)
    return y, h_re[:, -1], h_im[:, -1]


def token_mixing(h, offset, ctx_k, ctx_v, h0_re, h0_im, lp):
    bt, t = h.shape[0], h.shape[1]
    cuts = [int(c) for c in np.cumsum(IN_SPLITS)[:-1]]
    q, k, v, u, gate_a, gate_b = jnp.split(h @ lp['w_in'], cuts, axis=-1)
    pos = jnp.arange(t, dtype=jnp.float32) + offset
    q = rope(q.reshape(bt, t, N_Q_HEADS, HEAD_DIM), pos)
    k = rope(k.reshape(bt, t, N_KV_HEADS, HEAD_DIM), pos)
    v = v.reshape(bt, t, N_KV_HEADS, HEAD_DIM)
    sinks = lp['attn_sinks']
    if ctx_k is None:
        nb = t // WINDOW
        qb = q.reshape(bt * nb, WINDOW, N_KV_HEADS, Q_PER_KV, HEAD_DIM)
        kb = k.reshape(bt, nb, WINDOW, N_KV_HEADS, HEAD_DIM)
        vb = v.reshape(bt, nb, WINDOW, N_KV_HEADS, HEAD_DIM)
        k_prev = jnp.concatenate([jnp.zeros_like(kb[:, :1]), kb[:, :-1]], axis=1)
        v_prev = jnp.concatenate([jnp.zeros_like(vb[:, :1]), vb[:, :-1]], axis=1)
        flat = (bt * nb, WINDOW, N_KV_HEADS, HEAD_DIM)
        valid = jnp.tile(jnp.arange(nb) > 0, bt)
        o = band_window_attention(qb, k_prev.reshape(flat), v_prev.reshape(flat),
                                  kb.reshape(flat), vb.reshape(flat), valid, sinks)
        new_k = k[:, -WINDOW:]
        new_v = v[:, -WINDOW:]
    else:
        qs = q.reshape(bt, t, N_KV_HEADS, Q_PER_KV, HEAD_DIM)
        ctx_k = ctx_k.astype(k.dtype)
        ctx_v = ctx_v.astype(v.dtype)
        o = band_window_attention(qs, ctx_k, ctx_v, k, v, jnp.ones((bt,), dtype=bool), sinks)
        new_k = jnp.concatenate([ctx_k, k], axis=1)[:, -WINDOW:]
        new_v = jnp.concatenate([ctx_v, v], axis=1)[:, -WINDOW:]
    y_a = o.reshape(bt, t, ATTN_WIDTH) @ lp['w_attn_up']
    if h0_re is None:
        h0_re = jnp.zeros((bt, N_SSM_GROUPS, SSM_STATE), jnp.float32)
        h0_im = jnp.zeros((bt, N_SSM_GROUPS, SSM_STATE), jnp.float32)
    y_ssm, h_re, h_im = s5_scan(u.reshape(bt, t, N_SSM_GROUPS, SSM_GROUP).astype(jnp.float32),
                                h0_re.astype(jnp.float32), h0_im.astype(jnp.float32),
                                lp['ssm_a_re'], lp['ssm_a_im'], lp['ssm_log_dt'],
                                lp['ssm_b_re'], lp['ssm_b_im'], lp['ssm_c_re'], lp['ssm_c_im'], lp['ssm_d'])
    z = jax.nn.gelu(y_ssm.reshape(bt, t, SSM_WIDTH)).astype(h.dtype)
    glu_a, glu_b = jnp.split(z @ lp['w_ssm_glu'], 2, axis=-1)
    y_b = glu_a * jax.nn.sigmoid(glu_b)
    merged = jax.nn.sigmoid(gate_a) * y_a + jax.nn.sigmoid(gate_b) * y_b
    return merged @ lp['w_out'], new_k, new_v, h_re, h_im


def memory_kv(mem, g_mem, w_xk, w_xv):
    bt, nm = mem.shape[0], mem.shape[1]
    mn = rms_norm(mem, g_mem)
    mk = (mn @ w_xk).reshape(bt, nm, N_X_HEADS, X_HEAD_DIM)
    mv = (mn @ w_xv).reshape(bt, nm, N_X_HEADS, X_HEAD_DIM)
    return mk, mv


def cross_attention(h, mem_k, mem_v, w_xq, w_xo):
    bt, t = h.shape[0], h.shape[1]
    q = (h @ w_xq).reshape(bt, t, N_X_HEADS, X_HEAD_DIM)
    s = jnp.einsum('bqhd,bkhd->bhqk', q, mem_k.astype(q.dtype)).astype(jnp.float32) * (X_HEAD_DIM ** -0.5)
    p = jax.nn.softmax(s, axis=-1).astype(q.dtype)
    o = jnp.einsum('bhqk,bkhd->bqhd', p, mem_v.astype(q.dtype)).reshape(bt, t, D_MODEL)
    return o @ w_xo


def decoder_layer(x, offset, ctx_k, ctx_v, h0_re, h0_im, mem_k, mem_v, lp):
    x = x + 0.5 * swiglu_ffn(rms_norm(x, lp['g_ffn1']), lp['w_ffn1_in'], lp['w_ffn1_out'])
    mix, new_k, new_v, h_re, h_im = token_mixing(rms_norm(x, lp['g_mix']), offset, ctx_k, ctx_v, h0_re, h0_im, lp)
    x = x + mix
    x = x + cross_attention(rms_norm(x, lp['g_xattn']), mem_k, mem_v, lp['w_xq'], lp['w_xo'])
    x = x + 0.5 * swiglu_ffn(rms_norm(x, lp['g_ffn2']), lp['w_ffn2_in'], lp['w_ffn2_out'])
    return x, new_k, new_v, h_re, h_im


def _normal(k, shape, scale):
    return jax.random.normal(k, shape, jnp.float32) * scale


def _gain(k, shape):
    return 1.0 + 0.02 * jax.random.normal(k, shape, jnp.float32)


def setup_inputs(seed: int = 0) -> dict:
    key = jax.random.key(seed)
    ks = iter(jax.random.split(key, 48))
    L = DEPTH
    G, N, GS = N_SSM_GROUPS, SSM_STATE, SSM_GROUP
    a_re = -0.5 * jnp.exp(0.05 * jax.random.normal(next(ks), (L, G, N), jnp.float32))
    a_im = math.pi * jnp.arange(N, dtype=jnp.float32)[None, None, :] + 0.05 * jax.random.normal(next(ks), (L, G, N), jnp.float32)
    log_dt = jax.random.uniform(next(ks), (L, G), jnp.float32, minval=math.log(DT_MIN), maxval=math.log(DT_MAX))
    return {
        'x_prompt': _normal(next(ks), (BATCH, SEQ, D_MODEL), 1.0),
        'x_sample': _normal(next(ks), (DEC_BATCH, DEC_SEQ, D_MODEL), 1.0),
        'cache_win_k': _normal(next(ks), (L, DEC_BATCH, WINDOW, N_KV_HEADS, HEAD_DIM), 1.0),
        'cache_win_v': _normal(next(ks), (L, DEC_BATCH, WINDOW, N_KV_HEADS, HEAD_DIM), 1.0),
        'state_ssm_re': _normal(next(ks), (L, DEC_BATCH, G, N), 0.3),
        'state_ssm_im': _normal(next(ks), (L, DEC_BATCH, G, N), 0.3),
        'cache_mem_k': _normal(next(ks), (L, DEC_BATCH, N_MEM, N_X_HEADS, X_HEAD_DIM), 1.0),
        'cache_mem_v': _normal(next(ks), (L, DEC_BATCH, N_MEM, N_X_HEADS, X_HEAD_DIM), 1.0),
        'mem_prompt': _normal(next(ks), (BATCH, N_MEM, D_MODEL), 1.0),
        'g_ffn1': _gain(next(ks), (L, D_MODEL)),
        'w_ffn1_in': _normal(next(ks), (L, D_MODEL, 2 * D_FF), D_MODEL ** -0.5),
        'w_ffn1_out': _normal(next(ks), (L, D_FF, D_MODEL), D_FF ** -0.5),
        'g_mix': _gain(next(ks), (L, D_MODEL)),
        'w_in': _normal(next(ks), (L, D_MODEL, IN_WIDTH), D_MODEL ** -0.5),
        'attn_sinks': _normal(next(ks), (L, N_Q_HEADS), 0.5),
        'ssm_a_re': a_re,
        'ssm_a_im': a_im,
        'ssm_log_dt': log_dt,
        'ssm_b_re': _normal(next(ks), (L, G, N, GS), (2 * GS) ** -0.5),
        'ssm_b_im': _normal(next(ks), (L, G, N, GS), (2 * GS) ** -0.5),
        'ssm_c_re': _normal(next(ks), (L, G, GS, N), N ** -0.5),
        'ssm_c_im': _normal(next(ks), (L, G, GS, N), N ** -0.5),
        'ssm_d': _normal(next(ks), (L, SSM_WIDTH), 1.0),
        'w_attn_up': _normal(next(ks), (L, ATTN_WIDTH, D_MODEL), ATTN_WIDTH ** -0.5),
        'w_ssm_glu': _normal(next(ks), (L, SSM_WIDTH, 2 * D_MODEL), SSM_WIDTH ** -0.5),
        'w_out': _normal(next(ks), (L, D_MODEL, D_MODEL), D_MODEL ** -0.5),
        'g_xattn': _gain(next(ks), (L, D_MODEL)),
        'g_mem': _gain(next(ks), (L, D_MODEL)),
        'w_xq': _normal(next(ks), (L, D_MODEL, D_MODEL), D_MODEL ** -0.5),
        'w_xk': _normal(next(ks), (L, D_MODEL, D_MODEL), D_MODEL ** -0.5),
        'w_xv': _normal(next(ks), (L, D_MODEL, D_MODEL), D_MODEL ** -0.5),
        'w_xo': _normal(next(ks), (L, D_MODEL, D_MODEL), D_MODEL ** -0.5),
        'g_ffn2': _gain(next(ks), (L, D_MODEL)),
        'w_ffn2_in': _normal(next(ks), (L, D_MODEL, 2 * D_FF), D_MODEL ** -0.5),
        'w_ffn2_out': _normal(next(ks), (L, D_FF, D_MODEL), D_FF ** -0.5),
        'g_final': _gain(next(ks), (D_MODEL,)),
    }


def reference(x_prompt, x_sample, cache_win_k, cache_win_v, state_ssm_re, state_ssm_im,
              cache_mem_k, cache_mem_v, mem_prompt,
              g_ffn1, w_ffn1_in, w_ffn1_out, g_mix, w_in, attn_sinks,
              ssm_a_re, ssm_a_im, ssm_log_dt, ssm_b_re, ssm_b_im, ssm_c_re, ssm_c_im, ssm_d,
              w_attn_up, w_ssm_glu, w_out, g_xattn, g_mem, w_xq, w_xk, w_xv, w_xo,
              g_ffn2, w_ffn2_in, w_ffn2_out, g_final):
    xp = x_prompt
    xs = x_sample
    wk_p, wv_p, sr_p, si_p, mk_p, mv_p = [], [], [], [], [], []
    wk_s, wv_s, sr_s, si_s = [], [], [], []
    for l in range(DEPTH):
        lp = dict(g_ffn1=g_ffn1[l], w_ffn1_in=w_ffn1_in[l], w_ffn1_out=w_ffn1_out[l],
                  g_mix=g_mix[l], w_in=w_in[l], attn_sinks=attn_sinks[l],
                  ssm_a_re=ssm_a_re[l], ssm_a_im=ssm_a_im[l], ssm_log_dt=ssm_log_dt[l],
                  ssm_b_re=ssm_b_re[l], ssm_b_im=ssm_b_im[l], ssm_c_re=ssm_c_re[l], ssm_c_im=ssm_c_im[l],
                  ssm_d=ssm_d[l], w_attn_up=w_attn_up[l], w_ssm_glu=w_ssm_glu[l], w_out=w_out[l],
                  g_xattn=g_xattn[l], w_xq=w_xq[l], w_xo=w_xo[l],
                  g_ffn2=g_ffn2[l], w_ffn2_in=w_ffn2_in[l], w_ffn2_out=w_ffn2_out[l])
        mem_k_p, mem_v_p = memory_kv(mem_prompt, g_mem[l], w_xk[l], w_xv[l])
        xp, nk, nv, hr, hi = decoder_layer(xp, 0, None, None, None, None, mem_k_p, mem_v_p, lp)
        wk_p.append(nk); wv_p.append(nv); sr_p.append(hr); si_p.append(hi)
        mk_p.append(mem_k_p); mv_p.append(mem_v_p)
        xs, nk, nv, hr, hi = decoder_layer(xs, PAST_LEN, cache_win_k[l], cache_win_v[l],
                                           state_ssm_re[l], state_ssm_im[l],
                                           cache_mem_k[l], cache_mem_v[l], lp)
        wk_s.append(nk); wv_s.append(nv); sr_s.append(hr); si_s.append(hi)
    y_prompt = rms_norm(xp, g_final)
    y_sample = rms_norm(xs, g_final)
    return (y_prompt, y_sample,
            jnp.stack(wk_p), jnp.stack(wv_p), jnp.stack(sr_p), jnp.stack(si_p),
            jnp.stack(mk_p), jnp.stack(mv_p),
            jnp.stack(wk_s), jnp.stack(wv_s), jnp.stack(sr_s), jnp.stack(si_s))
```

```python
import functools
import math

import jax
import jax.numpy as jnp
from jax import lax
from jax.experimental import pallas as pl
from jax.experimental.pallas import tpu as pltpu

F32 = jnp.float32
BF16 = jnp.bfloat16

D_MODEL = 1024
N_Q_HEADS = 8
N_KV_HEADS = 2
HEAD_DIM = 64
Q_PER_KV = N_Q_HEADS // N_KV_HEADS
ATTN_WIDTH = N_Q_HEADS * HEAD_DIM
KV_WIDTH = N_KV_HEADS * HEAD_DIM
WINDOW = 128
ROPE_THETA = 10000.0
SSM_WIDTH = D_MODEL // 2
SSM_GROUP = 16
N_SSM_GROUPS = SSM_WIDTH // SSM_GROUP
SSM_STATE = 64
N_MEM = 256
N_X_HEADS = 4
X_HEAD_DIM = D_MODEL // N_X_HEADS
D_FF = 2816
RMS_EPS = 1e-6
NEG_INF = -1e30
IN_SPLITS = (ATTN_WIDTH, KV_WIDTH, KV_WIDTH, SSM_WIDTH, D_MODEL, D_MODEL)
IN_WIDTH = sum(IN_SPLITS)
OFF_Q, OFF_K, OFF_V, OFF_U, OFF_GA, OFF_GB = (0, 512, 640, 768, 1280, 2304)

LANES = 128
SUBLANES = 8
VMEM_LIMIT = 56 * 1024 * 1024

SSM_CH = N_SSM_GROUPS * SSM_STATE
HALF_CH = SSM_CH // 2
HALF_U = SSM_WIDTH // 2
BLK_PER_HALF = 2 * HALF_CH // LANES
SSM_TB = 256
SSM_PITCH = SSM_TB + 4


def _cparams(sem):
    return pltpu.CompilerParams(dimension_semantics=sem, vmem_limit_bytes=VMEM_LIMIT)


def _const_spec(shape):
    nd = len(shape)
    return pl.BlockSpec(shape, lambda *_: (0,) * nd, pipeline_mode=pl.Buffered(1))


def _rms(x, g):
    return x * lax.rsqrt(jnp.mean(x * x, axis=-1, keepdims=True) + RMS_EPS) * g


def _dot(a, b):
    return jnp.dot(a, b, preferred_element_type=F32)


def _dot_nt(a, b):
    return lax.dot_general(a, b, (((1,), (1,)), ((), ())), preferred_element_type=F32)


def _ffn_kernel(*refs, has_pre, final):
    if has_pre:
        x_ref, ox_ref, wxo_ref, g_ref, wi_ref, wo_ref, gf_ref, o_ref = refs
        x = x_ref[...] + _dot(ox_ref[...], wxo_ref[...])
    else:
        x_ref, g_ref, wi_ref, wo_ref, gf_ref, o_ref = refs
        x = x_ref[...]
    h = _rms(x, g_ref[...]).astype(BF16)
    a = _dot(h, wi_ref[:, :D_FF])
    b = _dot(h, wi_ref[:, D_FF:])
    act = (a * jax.nn.sigmoid(a) * b).astype(BF16)
    y = x + 0.5 * _dot(act, wo_ref[...])
    if final:
        y = _rms(y, gf_ref[...])
    o_ref[...] = y


def _ffn(x, g, wi, wo, gf, *, tm, pre=None, final=False):
    t = x.shape[0]
    row = lambda w: pl.BlockSpec((tm, w), lambda i: (i, 0))
    in_specs = [row(D_MODEL)]
    args = [x]
    if pre is not None:
        ox, wxo = pre
        in_specs += [row(D_MODEL), _const_spec((D_MODEL, D_MODEL))]
        args += [ox, wxo]
    in_specs += [_const_spec((1, D_MODEL)), _const_spec((D_MODEL, 2 * D_FF)), _const_spec((D_FF, D_MODEL)),
                 _const_spec((1, D_MODEL))]
    args += [g, wi, wo, gf]
    return pl.pallas_call(
        functools.partial(_ffn_kernel, has_pre=pre is not None, final=final),
        grid=(t // tm,),
        in_specs=in_specs,
        out_specs=row(D_MODEL),
        out_shape=jax.ShapeDtypeStruct((t, D_MODEL), F32),
        compiler_params=_cparams(("parallel",)),
        name="tail" if pre is not None else "ffn",
    )(*args)


def _inproj_kernel(x_ref, g_ref, w_ref, cos_ref, sin_ref, q_ref, k_ref, v_ref, u_ref, ga_ref, gb_ref):
    h = _rms(x_ref[...], g_ref[...]).astype(BF16)
    cos = cos_ref[...]
    sin = sin_ref[...]
    lane = lax.broadcasted_iota(jnp.int32, cos.shape, 1)
    first_half = (lane & (HEAD_DIM - 1)) < (HEAD_DIM // 2)

    def rope(xc):
        rot = jnp.where(first_half, pltpu.roll(xc, LANES - HEAD_DIM // 2, 1), pltpu.roll(xc, HEAD_DIM // 2, 1))
        return xc * cos + rot * sin

    yq = _dot(h, w_ref[:, OFF_Q:OFF_K])
    for c in range(ATTN_WIDTH // LANES):
        q_ref[:, c * LANES:(c + 1) * LANES] = (rope(yq[:, c * LANES:(c + 1) * LANES]) * (HEAD_DIM ** -0.5)).astype(BF16)
    ykvu = _dot(h, w_ref[:, OFF_K:OFF_GA])
    k_ref[...] = rope(ykvu[:, :KV_WIDTH])
    v_ref[...] = ykvu[:, KV_WIDTH:2 * KV_WIDTH]
    u_ref[...] = ykvu[:, 2 * KV_WIDTH:]
    ga_ref[...] = jax.nn.sigmoid(_dot(h, w_ref[:, OFF_GA:OFF_GB])).astype(BF16)
    gb_ref[...] = jax.nn.sigmoid(_dot(h, w_ref[:, OFF_GB:])).astype(BF16)


def _inproj(x, g, w, cos, sin, *, tm):
    t = x.shape[0]
    row = lambda wd: pl.BlockSpec((tm, wd), lambda i: (i, 0))
    widths = (ATTN_WIDTH, KV_WIDTH, KV_WIDTH, SSM_WIDTH, D_MODEL, D_MODEL)
    dtypes = (BF16, F32, F32, F32, BF16, BF16)
    return pl.pallas_call(
        _inproj_kernel,
        grid=(t // tm,),
        in_specs=[row(D_MODEL), _const_spec((1, D_MODEL)), _const_spec((D_MODEL, IN_WIDTH)), row(LANES), row(LANES)],
        out_specs=[row(wd) for wd in widths],
        out_shape=[jax.ShapeDtypeStruct((t, wd), dt) for wd, dt in zip(widths, dtypes)],
        compiler_params=_cparams(("parallel",)),
        name="inproj",
    )(x, g, w, cos, sin)


def _rope_tables(t, offset, repeat):
    half = HEAD_DIM // 2
    inv = ROPE_THETA ** (-jnp.arange(half, dtype=F32) / half)
    pos = jnp.arange(t, dtype=F32) + offset
    ang = pos[:, None] * inv[None, :]
    cos = jnp.cos(ang)
    sin = jnp.sin(ang)
    cos = jnp.tile(cos, (repeat, LANES // half))
    sin = jnp.tile(jnp.concatenate([-sin, sin], axis=1), (repeat, LANES // HEAD_DIM))
    return cos, sin


def _attn_prompt_kernel(sinks_ref, q_ref, kp_ref, kc_ref, vp_ref, vc_ref, o_ref):
    i = pl.program_id(0)
    r = lax.broadcasted_iota(jnp.int32, (WINDOW, 2 * WINDOW), 0)
    c = lax.broadcasted_iota(jnp.int32, (WINDOW, 2 * WINDOW), 1)
    mask = (c >= r) & (c <= WINDOW + r) & ((c >= WINDOW) | (i > 0))
    q = q_ref[...]
    kcat = jnp.concatenate([kp_ref[...], kc_ref[...]], axis=0).astype(BF16)
    vcat = jnp.concatenate([vp_ref[...], vc_ref[...]], axis=0).astype(BF16)
    outs = []
    for h in range(N_Q_HEADS):
        kv = h // Q_PER_KV
        s = _dot_nt(q[:, h * HEAD_DIM:(h + 1) * HEAD_DIM], kcat[:, kv * HEAD_DIM:(kv + 1) * HEAD_DIM])
        s = jnp.where(mask, s, NEG_INF)
        sink = sinks_ref[h]
        m = jnp.maximum(jnp.max(s, axis=-1, keepdims=True), sink)
        p = jnp.exp(s - m)
        denom = jnp.sum(p, axis=-1, keepdims=True) + jnp.exp(sink - m)
        outs.append(_dot(p.astype(BF16), vcat[:, kv * HEAD_DIM:(kv + 1) * HEAD_DIM]) / denom)
    o_ref[...] = jnp.concatenate(outs, axis=1).astype(BF16)


def _attn_prompt(sinks, q, k, v):
    t = q.shape[0]
    cur = lambda w: pl.BlockSpec((WINDOW, w), lambda i: (i, 0))
    prev = lambda w: pl.BlockSpec((WINDOW, w), lambda i: (jnp.maximum(i - 1, 0), 0))
    return pl.pallas_call(
        _attn_prompt_kernel,
        grid=(t // WINDOW,),
        in_specs=[pl.BlockSpec(memory_space=pltpu.SMEM), cur(ATTN_WIDTH), prev(KV_WIDTH), cur(KV_WIDTH),
                  prev(KV_WIDTH), cur(KV_WIDTH)],
        out_specs=cur(ATTN_WIDTH),
        out_shape=jax.ShapeDtypeStruct((t, ATTN_WIDTH), BF16),
        compiler_params=_cparams(("parallel",)),
        name="attn_prompt",
    )(sinks, q, k, k, v, v)


def _attn_sample_kernel(sinks_ref, q_ref, k_ref, v_ref, ck_ref, cv_ref, o_ref, nk_ref, nv_ref, *, nb, ts):
    rows = nb * ts
    q = q_ref[...]
    knew = k_ref[...]
    vnew = v_ref[...]
    ck = ck_ref[...]
    cv = cv_ref[...]
    nk_ref[:, :WINDOW - ts, :] = ck[:, ts:, :]
    nk_ref[:, WINDOW - ts:, :] = knew.reshape(nb, ts, KV_WIDTH)
    nv_ref[:, :WINDOW - ts, :] = cv[:, ts:, :]
    nv_ref[:, WINDOW - ts:, :] = vnew.reshape(nb, ts, KV_WIDTH)

    kctx = ck.reshape(nb * WINDOW, KV_WIDTH).astype(BF16)
    vctx = cv.reshape(nb * WINDOW, KV_WIDTH).astype(BF16)
    knew = knew.astype(BF16)
    vnew = vnew.astype(BF16)
    grows = Q_PER_KV * rows
    row1 = lax.broadcasted_iota(jnp.int32, (grows, nb * WINDOW), 0)
    col1 = lax.broadcasted_iota(jnp.int32, (grows, nb * WINDOW), 1)
    rseq1 = (row1 % rows) // ts
    mask_ctx = (rseq1 == col1 // WINDOW) & (col1 % WINDOW >= row1 % ts)
    row2 = lax.broadcasted_iota(jnp.int32, (grows, rows), 0)
    col2 = lax.broadcasted_iota(jnp.int32, (grows, rows), 1)
    mask_new = ((row2 % rows) // ts == col2 // ts) & (col2 % ts <= row2 % ts)
    ghead = lax.broadcasted_iota(jnp.int32, (grows, 1), 0) // rows
    for kv in range(N_KV_HEADS):
        ksl = slice(kv * HEAD_DIM, (kv + 1) * HEAD_DIM)
        qg = jnp.concatenate([q[:, (kv * Q_PER_KV + g) * HEAD_DIM:(kv * Q_PER_KV + g + 1) * HEAD_DIM]
                              for g in range(Q_PER_KV)], axis=0)
        s1 = jnp.where(mask_ctx, _dot_nt(qg, kctx[:, ksl]), NEG_INF)
        s2 = jnp.where(mask_new, _dot_nt(qg, knew[:, ksl]), NEG_INF)
        sink = jnp.zeros((grows, 1), F32)
        for g in range(Q_PER_KV):
            sink = jnp.where(ghead == g, sinks_ref[kv * Q_PER_KV + g], sink)
        m = jnp.maximum(jnp.maximum(jnp.max(s1, axis=-1, keepdims=True), jnp.max(s2, axis=-1, keepdims=True)), sink)
        p1 = jnp.exp(s1 - m)
        p2 = jnp.exp(s2 - m)
        denom = jnp.sum(p1, axis=-1, keepdims=True) + jnp.sum(p2, axis=-1, keepdims=True) + jnp.exp(sink - m)
        o = (_dot(p1.astype(BF16), vctx[:, ksl]) + _dot(p2.astype(BF16), vnew[:, ksl])) / denom
        for g in range(Q_PER_KV):
            h = kv * Q_PER_KV + g
            o_ref[:, h * HEAD_DIM:(h + 1) * HEAD_DIM] = o[g * rows:(g + 1) * rows].astype(BF16)


def _attn_sample(sinks, q, k, v, ck, cv, *, nb, ts):
    nseq = ck.shape[0]
    rows = nb * ts
    row = lambda w: pl.BlockSpec((rows, w), lambda i: (i, 0))
    win = pl.BlockSpec((nb, WINDOW, KV_WIDTH), lambda i: (i, 0, 0))
    return pl.pallas_call(
        functools.partial(_attn_sample_kernel, nb=nb, ts=ts),
        grid=(nseq // nb,),
        in_specs=[pl.BlockSpec(memory_space=pltpu.SMEM), row(ATTN_WIDTH), row(KV_WIDTH), row(KV_WIDTH), win, win],
        out_specs=[row(ATTN_WIDTH), win, win],
        out_shape=[jax.ShapeDtypeStruct((nseq * ts, ATTN_WIDTH), BF16),
                   jax.ShapeDtypeStruct((nseq, WINDOW, KV_WIDTH), F32),
                   jax.ShapeDtypeStruct((nseq, WINDOW, KV_WIDTH), F32)],
        compiler_params=_cparams(("parallel",)),
        name="attn_sample",
    )(sinks, q, k, v, ck, cv)


def _ssm_prep_kernel(are_ref, aim_ref, ldt_ref, bre_ref, bim_ref, lre_ref, lim_ref, bbre_ref, bbim_ref):
    a_re = are_ref[...]
    a_im = aim_ref[...]
    dt = jnp.exp(ldt_ref[...])
    mag = jnp.exp(a_re * dt)
    lb_re = mag * jnp.cos(a_im * dt)
    lb_im = mag * jnp.sin(a_im * dt)
    den = a_re * a_re + a_im * a_im
    nr = lb_re - 1.0
    ni = lb_im
    k_re = (nr * a_re + ni * a_im) / den
    k_im = (ni * a_re - nr * a_im) / den
    lre_ref[...] = lb_re
    lim_ref[...] = lb_im
    b_re = bre_ref[...]
    b_im = bim_ref[...]
    bbre_ref[...] = k_re[:, None, :] * b_re - k_im[:, None, :] * b_im
    bbim_ref[...] = k_re[:, None, :] * b_im + k_im[:, None, :] * b_re


def _ssm_prep(a_re, a_im, log_dt, b_re, b_im):
    g, n, gs = b_re.shape
    sds = jax.ShapeDtypeStruct
    return pl.pallas_call(
        _ssm_prep_kernel,
        out_shape=[sds((g, n), F32), sds((g, n), F32), sds((g, gs, n), F32), sds((g, gs, n), F32)],
        name="ssm_prep",
    )(a_re, a_im, log_dt.reshape(g, 1), jnp.swapaxes(b_re, 1, 2), jnp.swapaxes(b_im, 1, 2))


def _block_diag_halves(m):
    g, a, b = m.shape
    gh = g // 2
    eye = jnp.eye(gh, dtype=m.dtype)
    return jnp.einsum('hgab,gk->hgakb', m.reshape(2, gh, a, b), eye).reshape(2, gh * a, gh * b)


def _ssm_y(u, scan_block, cc_ref, d_ref):
    ys = []
    for half in range(2):
        hc = jnp.concatenate([scan_block(half, jj) for jj in range(BLK_PER_HALF)], axis=1).astype(BF16)
        ys.append(_dot(hc, cc_ref[half]))
    y = jnp.concatenate(ys, axis=1) + d_ref[...] * u
    return jax.nn.gelu(y).astype(BF16)


def _ssm_prompt_kernel(u_ref, bb_ref, cc_ref, lam_ref, d_ref, h0_ref, z_ref, hout_ref, xs_ref, hst_ref):
    tb, pitch = SSM_TB, SSM_PITCH

    @pl.when(pl.program_id(0) == 0)
    def _():
        hst_ref[...] = h0_ref[...]

    u = u_ref[...]
    ub = u.astype(BF16)
    for half in range(2):
        x = _dot(ub[:, half * HALF_U:(half + 1) * HALF_U], bb_ref[half])
        for jj in range(BLK_PER_HALF):
            xs_ref[pl.ds((half * BLK_PER_HALF + jj) * pitch, tb), :] = x[:, jj * LANES:(jj + 1) * LANES]

    lam = [lam_ref[g] for g in range(4)]

    def step(t, carry):
        new = []
        for half in range(2):
            hr, hi = carry[2 * half], carry[2 * half + 1]
            lr, li = lam[2 * half], lam[2 * half + 1]
            rows_re = pl.ds((half * BLK_PER_HALF) * pitch + t, SUBLANES, stride=pitch)
            rows_im = pl.ds((half * BLK_PER_HALF + SUBLANES) * pitch + t, SUBLANES, stride=pitch)
            nr = lr * hr - li * hi + xs_ref[rows_re, :]
            ni = lr * hi + li * hr + xs_ref[rows_im, :]
            xs_ref[rows_re, :] = nr
            xs_ref[rows_im, :] = ni
            new += [nr, ni]
        return tuple(new)

    carry = lax.fori_loop(0, tb, step, tuple(hst_ref[g] for g in range(4)), unroll=8)
    for g in range(4):
        hst_ref[g] = carry[g]
        hout_ref[g] = carry[g]

    z_ref[...] = _ssm_y(u, lambda half, jj: xs_ref[pl.ds((half * BLK_PER_HALF + jj) * pitch, tb), :], cc_ref, d_ref)


def _ssm_prompt(u, bb, cc, lam, d, h0):
    t = u.shape[0]
    return pl.pallas_call(
        _ssm_prompt_kernel,
        grid=(t // SSM_TB,),
        in_specs=[pl.BlockSpec((SSM_TB, SSM_WIDTH), lambda i: (i, 0)), _const_spec(bb.shape), _const_spec(cc.shape),
                  _const_spec(lam.shape), _const_spec(d.shape), _const_spec(h0.shape)],
        out_specs=[pl.BlockSpec((SSM_TB, SSM_WIDTH), lambda i: (i, 0)),
                   pl.BlockSpec((4, SUBLANES, LANES), lambda i: (0, 0, 0))],
        out_shape=[jax.ShapeDtypeStruct((t, SSM_WIDTH), BF16), jax.ShapeDtypeStruct((4, SUBLANES, LANES), F32)],
        scratch_shapes=[pltpu.VMEM((2 * BLK_PER_HALF * SSM_PITCH, LANES), F32),
                        pltpu.VMEM((4, SUBLANES, LANES), F32)],
        compiler_params=_cparams(("arbitrary",)),
        name="ssm_prompt",
    )(u, bb, cc, lam, d, h0)


def _ssm_sample_kernel(u_ref, bb_ref, cc_ref, lre_ref, lim_ref, d_ref, h0re_ref, h0im_ref,
                       z_ref, hre_ref, him_ref, xs_ref, *, nb, ts):
    u = u_ref[...]
    ub = u.astype(BF16)
    for half in range(2):
        x = _dot(ub[:, half * HALF_U:(half + 1) * HALF_U], bb_ref[half])
        for jj in range(BLK_PER_HALF):
            xs_ref[half * BLK_PER_HALF + jj] = x[:, jj * LANES:(jj + 1) * LANES]
    for cb in range(SSM_CH // LANES):
        half, jj = divmod(cb, SUBLANES)
        j_re = half * BLK_PER_HALF + jj
        j_im = j_re + SUBLANES
        csl = slice(cb * LANES, (cb + 1) * LANES)
        lr = lre_ref[:, csl]
        li = lim_ref[:, csl]
        hr = h0re_ref[:, csl]
        hi = h0im_ref[:, csl]
        for t in range(ts):
            rows = pl.ds(t, nb, stride=ts)
            nr = lr * hr - li * hi + xs_ref[j_re, rows, :]
            ni = lr * hi + li * hr + xs_ref[j_im, rows, :]
            xs_ref[j_re, rows, :] = nr
            xs_ref[j_im, rows, :] = ni
            hr, hi = nr, ni
        hre_ref[:, csl] = hr
        him_ref[:, csl] = hi
    z_ref[...] = _ssm_y(u, lambda half, jj: xs_ref[half * BLK_PER_HALF + jj], cc_ref, d_ref)


def _ssm_sample(u, bb, cc, lre, lim, d, h0re, h0im, *, nb, ts):
    nseq = h0re.shape[0]
    rows = nb * ts
    st = pl.BlockSpec((nb, SSM_CH), lambda i: (i, 0))
    return pl.pallas_call(
        functools.partial(_ssm_sample_kernel, nb=nb, ts=ts),
        grid=(nseq // nb,),
        in_specs=[pl.BlockSpec((rows, SSM_WIDTH), lambda i: (i, 0)), _const_spec(bb.shape), _const_spec(cc.shape),
                  _const_spec(lre.shape), _const_spec(lim.shape), _const_spec(d.shape), st, st],
        out_specs=[pl.BlockSpec((rows, SSM_WIDTH), lambda i: (i, 0)), st, st],
        out_shape=[jax.ShapeDtypeStruct((nseq * ts, SSM_WIDTH), BF16),
                   jax.ShapeDtypeStruct((nseq, SSM_CH), F32), jax.ShapeDtypeStruct((nseq, SSM_CH), F32)],
        scratch_shapes=[pltpu.VMEM((2 * BLK_PER_HALF, rows, LANES), F32)],
        compiler_params=_cparams(("parallel",)),
        name="ssm_sample",
    )(u, bb, cc, lre, lim, d, h0re, h0im)


def _merge_kernel(x1_ref, o_ref, z_ref, ga_ref, gb_ref, wup_ref, wglu_ref, wout_ref, gx_ref, wxq_ref,
                  x2_ref, qx_ref):
    ya = _dot(o_ref[...], wup_ref[...])
    glu = _dot(z_ref[...], wglu_ref[...])
    yb = glu[:, :D_MODEL] * jax.nn.sigmoid(glu[:, D_MODEL:])
    merged = ga_ref[...].astype(F32) * ya + gb_ref[...].astype(F32) * yb
    x2 = x1_ref[...] + _dot(merged.astype(BF16), wout_ref[...])
    x2_ref[...] = x2
    hx = _rms(x2, gx_ref[...]).astype(BF16)
    qx_ref[...] = (_dot(hx, wxq_ref[...]) * (X_HEAD_DIM ** -0.5)).astype(BF16)


def _merge(x1, o, z, ga, gb, wup, wglu, wout, gx, wxq, *, tm):
    t = x1.shape[0]
    row = lambda w: pl.BlockSpec((tm, w), lambda i: (i, 0))
    return pl.pallas_call(
        _merge_kernel,
        grid=(t // tm,),
        in_specs=[row(D_MODEL), row(ATTN_WIDTH), row(SSM_WIDTH), row(D_MODEL), row(D_MODEL),
                  _const_spec(wup.shape), _const_spec(wglu.shape), _const_spec(wout.shape), _const_spec(gx.shape),
                  _const_spec(wxq.shape)],
        out_specs=[row(D_MODEL), row(D_MODEL)],
        out_shape=[jax.ShapeDtypeStruct((t, D_MODEL), F32), jax.ShapeDtypeStruct((t, D_MODEL), BF16)],
        compiler_params=_cparams(("parallel",)),
        name="merge",
    )(x1, o, z, ga, gb, wup, wglu, wout, gx, wxq)


def _memkv_kernel(mem_ref, g_ref, wk_ref, wv_ref, mk_ref, mv_ref):
    mn = _rms(mem_ref[...], g_ref[...]).astype(BF16)
    mk_ref[...] = _dot(mn, wk_ref[...])
    mv_ref[...] = _dot(mn, wv_ref[...])


def _memkv(mem, g, wk, wv):
    sds = jax.ShapeDtypeStruct((mem.shape[0], D_MODEL), F32)
    return pl.pallas_call(_memkv_kernel, out_shape=[sds, sds], name="memkv",
                          compiler_params=pltpu.CompilerParams(vmem_limit_bytes=VMEM_LIMIT))(mem, g, wk, wv)


def _xattn_heads(q, mk, mv):
    outs = []
    for h in range(N_X_HEADS):
        sl = slice(h * X_HEAD_DIM, (h + 1) * X_HEAD_DIM)
        s = _dot_nt(q[:, sl], mk[:, sl])
        p = jnp.exp(s - jnp.max(s, axis=-1, keepdims=True))
        outs.append(_dot(p.astype(BF16), mv[:, sl]) / jnp.sum(p, axis=-1, keepdims=True))
    return jnp.concatenate(outs, axis=1)


def _xattn_prompt_kernel(q_ref, mk_ref, mv_ref, o_ref):
    o_ref[...] = _xattn_heads(q_ref[...], mk_ref[...].astype(BF16), mv_ref[...].astype(BF16)).astype(BF16)


def _xattn_prompt(qx, mk, mv, *, tm):
    t = qx.shape[0]
    row = pl.BlockSpec((tm, D_MODEL), lambda i: (i, 0))
    return pl.pallas_call(
        _xattn_prompt_kernel,
        grid=(t // tm,),
        in_specs=[row, _const_spec(mk.shape), _const_spec(mv.shape)],
        out_specs=row,
        out_shape=jax.ShapeDtypeStruct((t, D_MODEL), BF16),
        compiler_params=_cparams(("parallel",)),
        name="xattn_prompt",
    )(qx, mk, mv)


def _xattn_sample_kernel(q_ref, mk_ref, mv_ref, o_ref, *, nb, ts):
    for b in range(nb):
        q = q_ref[b * ts:(b + 1) * ts, :]
        o_ref[b * ts:(b + 1) * ts, :] = _xattn_heads(q, mk_ref[b].astype(BF16), mv_ref[b].astype(BF16)).astype(BF16)


def _xattn_sample(qx, mk, mv, *, nb, ts):
    nseq = mk.shape[0]
    row = pl.BlockSpec((nb * ts, D_MODEL), lambda i: (i, 0))
    mem = pl.BlockSpec((nb, N_MEM, D_MODEL), lambda i: (i, 0, 0))
    return pl.pallas_call(
        functools.partial(_xattn_sample_kernel, nb=nb, ts=ts),
        grid=(nseq // nb,),
        in_specs=[row, mem, mem],
        out_specs=row,
        out_shape=jax.ShapeDtypeStruct((nseq * ts, D_MODEL), BF16),
        compiler_params=_cparams(("parallel",)),
        name="xattn_sample",
    )(qx, mk, mv)


def kernel(x_prompt, x_sample, cache_win_k, cache_win_v, state_ssm_re, state_ssm_im, cache_mem_k, cache_mem_v, mem_prompt, g_ffn1, w_ffn1_in, w_ffn1_out, g_mix, w_in, attn_sinks, ssm_a_re, ssm_a_im, ssm_log_dt, ssm_b_re, ssm_b_im, ssm_c_re, ssm_c_im, ssm_d, w_attn_up, w_ssm_glu, w_out, g_xattn, g_mem, w_xq, w_xk, w_xv, w_xo, g_ffn2, w_ffn2_in, w_ffn2_out, g_final):
    assert x_prompt.shape[0] == 1 and g_ffn1.shape[0] == 1
    seq = x_prompt.shape[1]
    nseq, ts = x_sample.shape[0], x_sample.shape[1]
    past_len = seq
    l = 0
    bf = lambda w: w[l].astype(BF16)
    vec = lambda g: g[l].reshape(1, -1)
    w1i, w1o, w2i, w2o = bf(w_ffn1_in), bf(w_ffn1_out), bf(w_ffn2_in), bf(w_ffn2_out)
    win, wup, wglu, wout = bf(w_in), bf(w_attn_up), bf(w_ssm_glu), bf(w_out)
    wxq, wxk, wxv, wxo = bf(w_xq), bf(w_xk), bf(w_xv), bf(w_xo)
    gf = g_final.reshape(1, -1)
    sinks = attn_sinks[l]

    lam_re, lam_im, bbt_re, bbt_im = _ssm_prep(ssm_a_re[l], ssm_a_im[l], ssm_log_dt[l], ssm_b_re[l], ssm_b_im[l])
    bb = jnp.concatenate([_block_diag_halves(bbt_re), _block_diag_halves(bbt_im)], axis=-1).astype(BF16)
    ct_re = jnp.swapaxes(ssm_c_re[l], 1, 2)
    ct_im = jnp.swapaxes(ssm_c_im[l], 1, 2)
    cc = jnp.concatenate([_block_diag_halves(ct_re), -_block_diag_halves(ct_im)], axis=1).astype(BF16)
    d_skip = ssm_d[l].reshape(1, -1)
    lre16 = lam_re.reshape(2, SUBLANES, LANES)
    lim16 = lam_im.reshape(2, SUBLANES, LANES)
    lam_tm = jnp.stack([lre16[0], lim16[0], lre16[1], lim16[1]])
    lre_row = lam_re.reshape(1, SSM_CH)
    lim_row = lam_im.reshape(1, SSM_CH)

    mk_p, mv_p = _memkv(mem_prompt[0], vec(g_mem), wxk, wxv)

    def group(x, cos, sin, attn_fn, ssm_fn, xattn_fn, tm):
        x1 = _ffn(x, vec(g_ffn1), w1i, w1o, gf, tm=tm)
        q, k, v, u, ga, gb = _inproj(x1, vec(g_mix), win, cos, sin, tm=tm)
        o, extra_attn = attn_fn(q, k, v)
        z, extra_ssm = ssm_fn(u)
        x2, qx = _merge(x1, o, z, ga, gb, wup, wglu, wout, vec(g_xattn), wxq, tm=tm)
        ox = xattn_fn(qx)
        y = _ffn(x2, vec(g_ffn2), w2i, w2o, gf, tm=tm, pre=(ox, wxo), final=True)
        return y, extra_attn, extra_ssm

    cos_p, sin_p = _rope_tables(seq, 0.0, 1)

    def attn_p(q, k, v):
        return _attn_prompt(sinks, q, k, v), (k[-WINDOW:], v[-WINDOW:])

    def ssm_p(u):
        z, hout = _ssm_prompt(u, bb, cc, lam_tm, d_skip, jnp.zeros((4, SUBLANES, LANES), F32))
        return z, hout

    y_p, (nk_p, nv_p), hout_p = group(x_prompt[0], cos_p, sin_p, attn_p, ssm_p,
                                      lambda qx: _xattn_prompt(qx, mk_p, mv_p, tm=512), 512)
    hre_p = jnp.concatenate([hout_p[0], hout_p[2]], axis=0)
    him_p = jnp.concatenate([hout_p[1], hout_p[3]], axis=0)

    cos_s, sin_s = _rope_tables(ts, float(past_len), nseq)

    def attn_s(q, k, v):
        o, nk, nv = _attn_sample(sinks, q, k, v, cache_win_k[l].reshape(nseq, WINDOW, KV_WIDTH),
                                 cache_win_v[l].reshape(nseq, WINDOW, KV_WIDTH), nb=8, ts=ts)
        return o, (nk, nv)

    def ssm_s(u):
        z, hre, him = _ssm_sample(u, bb, cc, lre_row, lim_row, d_skip, state_ssm_re[l].reshape(nseq, SSM_CH),
                                  state_ssm_im[l].reshape(nseq, SSM_CH), nb=64, ts=ts)
        return z, (hre, him)

    y_s, (nk_s, nv_s), (hre_s, him_s) = group(
        x_sample.reshape(nseq * ts, D_MODEL), cos_s, sin_s, attn_s, ssm_s,
        lambda qx: _xattn_sample(qx, cache_mem_k[l].reshape(nseq, N_MEM, D_MODEL),
                                 cache_mem_v[l].reshape(nseq, N_MEM, D_MODEL), nb=4, ts=ts), 512)

    kvshape = (1, 1, WINDOW, N_KV_HEADS, HEAD_DIM)
    stshape = (1, 1, N_SSM_GROUPS, SSM_STATE)
    memshape = (1, 1, N_MEM, N_X_HEADS, X_HEAD_DIM)
    return (y_p.reshape(1, seq, D_MODEL), y_s.reshape(nseq, ts, D_MODEL),
            nk_p.reshape(kvshape), nv_p.reshape(kvshape), hre_p.reshape(stshape), him_p.reshape(stshape),
            mk_p.reshape(memshape), mv_p.reshape(memshape),
            nk_s.reshape(1, nseq, WINDOW, N_KV_HEADS, HEAD_DIM), nv_s.reshape(1, nseq, WINDOW, N_KV_HEADS, HEAD_DIM),
            hre_s.reshape(1, nseq, N_SSM_GROUPS, SSM_STATE), him_s.reshape(1, nseq, N_SSM_GROUPS, SSM_STATE))
```

```python
import functools
import math

import jax
import jax.numpy as jnp
from jax import lax
from jax.experimental import pallas as pl
from jax.experimental.pallas import tpu as pltpu

F32 = jnp.float32
BF16 = jnp.bfloat16

D_MODEL = 1024
N_Q_HEADS = 8
N_KV_HEADS = 2
HEAD_DIM = 64
Q_PER_KV = N_Q_HEADS // N_KV_HEADS
ATTN_WIDTH = N_Q_HEADS * HEAD_DIM
KV_WIDTH = N_KV_HEADS * HEAD_DIM
WINDOW = 128
ROPE_THETA = 10000.0
SSM_WIDTH = D_MODEL // 2
SSM_GROUP = 16
N_SSM_GROUPS = SSM_WIDTH // SSM_GROUP
SSM_STATE = 64
N_MEM = 256
N_X_HEADS = 4
X_HEAD_DIM = D_MODEL // N_X_HEADS
D_FF = 2816
RMS_EPS = 1e-6
NEG_INF = -1e30
IN_SPLITS = (ATTN_WIDTH, KV_WIDTH, KV_WIDTH, SSM_WIDTH, D_MODEL, D_MODEL)
IN_WIDTH = sum(IN_SPLITS)
OFF_Q, OFF_K, OFF_V, OFF_U, OFF_GA, OFF_GB = (0, 512, 640, 768, 1280, 2304)

LANES = 128
SUBLANES = 8
VMEM_LIMIT = 56 * 1024 * 1024

SSM_CH = N_SSM_GROUPS * SSM_STATE
HALF_CH = SSM_CH // 2
HALF_U = SSM_WIDTH // 2
BLK_PER_HALF = 2 * HALF_CH // LANES
SSM_TB = 256
SSM_PITCH = SSM_TB + 4


def _cparams(sem):
    return pltpu.CompilerParams(dimension_semantics=sem, vmem_limit_bytes=VMEM_LIMIT)


def _const_spec(shape):
    nd = len(shape)
    return pl.BlockSpec(shape, lambda *_: (0,) * nd, pipeline_mode=pl.Buffered(1))


def _rms(x, g):
    return x * lax.rsqrt(jnp.mean(x * x, axis=-1, keepdims=True) + RMS_EPS) * g


def _dot(a, b):
    return jnp.dot(a, b, preferred_element_type=F32)


def _dot_nt(a, b):
    return lax.dot_general(a, b, (((1,), (1,)), ((), ())), preferred_element_type=F32)


def _ffn_kernel(*refs, has_pre, final):
    if has_pre:
        x_ref, ox_ref, wxo_ref, g_ref, wi_ref, wo_ref, gf_ref, o_ref = refs
        x = x_ref[...] + _dot(ox_ref[...], wxo_ref[...])
    else:
        x_ref, g_ref, wi_ref, wo_ref, gf_ref, o_ref = refs
        x = x_ref[...]
    h = _rms(x, g_ref[...]).astype(BF16)
    a = _dot(h, wi_ref[:, :D_FF])
    b = _dot(h, wi_ref[:, D_FF:])
    act = (a * jax.nn.sigmoid(a) * b).astype(BF16)
    y = x + 0.5 * _dot(act, wo_ref[...])
    if final:
        y = _rms(y, gf_ref[...])
    o_ref[...] = y


def _ffn(x, g, wi, wo, gf, *, tm, pre=None, final=False):
    t = x.shape[0]
    row = lambda w: pl.BlockSpec((tm, w), lambda i: (i, 0))
    in_specs = [row(D_MODEL)]
    args = [x]
    if pre is not None:
        ox, wxo = pre
        in_specs += [row(D_MODEL), _const_spec((D_MODEL, D_MODEL))]
        args += [ox, wxo]
    in_specs += [_const_spec((1, D_MODEL)), _const_spec((D_MODEL, 2 * D_FF)), _const_spec((D_FF, D_MODEL)),
                 _const_spec((1, D_MODEL))]
    args += [g, wi, wo, gf]
    return pl.pallas_call(
        functools.partial(_ffn_kernel, has_pre=pre is not None, final=final),
        grid=(t // tm,),
        in_specs=in_specs,
        out_specs=row(D_MODEL),
        out_shape=jax.ShapeDtypeStruct((t, D_MODEL), F32),
        compiler_params=_cparams(("parallel",)),
        name="tail" if pre is not None else "ffn",
    )(*args)


def _inproj_kernel(x_ref, g_ref, w_ref, cos_ref, sin_ref, q_ref, k_ref, v_ref, u_ref, ga_ref, gb_ref):
    h = _rms(x_ref[...], g_ref[...]).astype(BF16)
    cos = cos_ref[...]
    sin = sin_ref[...]
    lane = lax.broadcasted_iota(jnp.int32, cos.shape, 1)
    first_half = (lane & (HEAD_DIM - 1)) < (HEAD_DIM // 2)

    def rope(xc):
        rot = jnp.where(first_half, pltpu.roll(xc, LANES - HEAD_DIM // 2, 1), pltpu.roll(xc, HEAD_DIM // 2, 1))
        return xc * cos + rot * sin

    yq = _dot(h, w_ref[:, OFF_Q:OFF_K])
    for c in range(ATTN_WIDTH // LANES):
        q_ref[:, c * LANES:(c + 1) * LANES] = (rope(yq[:, c * LANES:(c + 1) * LANES]) * (HEAD_DIM ** -0.5)).astype(BF16)
    ykvu = _dot(h, w_ref[:, OFF_K:OFF_GA])
    k_ref[...] = rope(ykvu[:, :KV_WIDTH])
    v_ref[...] = ykvu[:, KV_WIDTH:2 * KV_WIDTH]
    u_ref[...] = ykvu[:, 2 * KV_WIDTH:]
    ga_ref[...] = jax.nn.sigmoid(_dot(h, w_ref[:, OFF_GA:OFF_GB])).astype(BF16)
    gb_ref[...] = jax.nn.sigmoid(_dot(h, w_ref[:, OFF_GB:])).astype(BF16)


def _inproj(x, g, w, cos, sin, *, tm):
    t = x.shape[0]
    row = lambda wd: pl.BlockSpec((tm, wd), lambda i: (i, 0))
    widths = (ATTN_WIDTH, KV_WIDTH, KV_WIDTH, SSM_WIDTH, D_MODEL, D_MODEL)
    dtypes = (BF16, F32, F32, F32, BF16, BF16)
    return pl.pallas_call(
        _inproj_kernel,
        grid=(t // tm,),
        in_specs=[row(D_MODEL), _const_spec((1, D_MODEL)), _const_spec((D_MODEL, IN_WIDTH)), row(LANES), row(LANES)],
        out_specs=[row(wd) for wd in widths],
        out_shape=[jax.ShapeDtypeStruct((t, wd), dt) for wd, dt in zip(widths, dtypes)],
        compiler_params=_cparams(("parallel",)),
        name="inproj",
    )(x, g, w, cos, sin)


def _rope_tables(t, offset, repeat):
    half = HEAD_DIM // 2
    inv = ROPE_THETA ** (-jnp.arange(half, dtype=F32) / half)
    pos = jnp.arange(t, dtype=F32) + offset
    ang = pos[:, None] * inv[None, :]
    cos = jnp.cos(ang)
    sin = jnp.sin(ang)
    cos = jnp.tile(cos, (repeat, LANES // half))
    sin = jnp.tile(jnp.concatenate([-sin, sin], axis=1), (repeat, LANES // HEAD_DIM))
    return cos, sin


def _attn_prompt_kernel(sinks_ref, q_ref, kp_ref, kc_ref, vp_ref, vc_ref, o_ref):
    i = pl.program_id(0)
    r = lax.broadcasted_iota(jnp.int32, (WINDOW, 2 * WINDOW), 0)
    c = lax.broadcasted_iota(jnp.int32, (WINDOW, 2 * WINDOW), 1)
    mask = (c >= r) & (c <= WINDOW + r) & ((c >= WINDOW) | (i > 0))
    q = q_ref[...]
    kcat = jnp.concatenate([kp_ref[...], kc_ref[...]], axis=0).astype(BF16)
    vcat = jnp.concatenate([vp_ref[...], vc_ref[...]], axis=0).astype(BF16)
    outs = []
    for h in range(N_Q_HEADS):
        kv = h // Q_PER_KV
        s = _dot_nt(q[:, h * HEAD_DIM:(h + 1) * HEAD_DIM], kcat[:, kv * HEAD_DIM:(kv + 1) * HEAD_DIM])
        s = jnp.where(mask, s, NEG_INF)
        sink = sinks_ref[h]
        m = jnp.maximum(jnp.max(s, axis=-1, keepdims=True), sink)
        p = jnp.exp(s - m)
        denom = jnp.sum(p, axis=-1, keepdims=True) + jnp.exp(sink - m)
        outs.append(_dot(p.astype(BF16), vcat[:, kv * HEAD_DIM:(kv + 1) * HEAD_DIM]) / denom)
    o_ref[...] = jnp.concatenate(outs, axis=1).astype(BF16)


def _attn_prompt(sinks, q, k, v):
    t = q.shape[0]
    cur = lambda w: pl.BlockSpec((WINDOW, w), lambda i: (i, 0))
    prev = lambda w: pl.BlockSpec((WINDOW, w), lambda i: (jnp.maximum(i - 1, 0), 0))
    return pl.pallas_call(
        _attn_prompt_kernel,
        grid=(t // WINDOW,),
        in_specs=[pl.BlockSpec(memory_space=pltpu.SMEM), cur(ATTN_WIDTH), prev(KV_WIDTH), cur(KV_WIDTH),
                  prev(KV_WIDTH), cur(KV_WIDTH)],
        out_specs=cur(ATTN_WIDTH),
        out_shape=jax.ShapeDtypeStruct((t, ATTN_WIDTH), BF16),
        compiler_params=_cparams(("parallel",)),
        name="attn_prompt",
    )(sinks, q, k, k, v, v)


def _attn_sample_kernel(sinks_ref, q_ref, k_ref, v_ref, ckt_ref, cvt_ref, o_ref, nkt_ref, nvt_ref, *, nb, ts):
    rows = nb * ts
    q = q_ref[...]
    pad = jnp.zeros((LANES - rows, KV_WIDTH), F32)
    knt = jnp.concatenate([k_ref[...], pad], axis=0).T
    vnt = jnp.concatenate([v_ref[...], pad], axis=0).T
    lane = lax.broadcasted_iota(jnp.int32, (HEAD_DIM, WINDOW), 1)
    is_new = lane >= WINDOW - ts
    for b in range(nb):
        knt_b = pltpu.roll(knt, (WINDOW - ts - b * ts) % LANES, 1)
        vnt_b = pltpu.roll(vnt, (WINDOW - ts - b * ts) % LANES, 1)
        for kv in range(N_KV_HEADS):
            ksl = slice(kv * HEAD_DIM, (kv + 1) * HEAD_DIM)
            nkt_ref[b, kv] = jnp.where(is_new, knt_b[ksl], pltpu.roll(ckt_ref[b, kv], WINDOW - ts, 1))
            nvt_ref[b, kv] = jnp.where(is_new, vnt_b[ksl], pltpu.roll(cvt_ref[b, kv], WINDOW - ts, 1))

    knt = knt[:, :rows].astype(BF16)
    vnt = vnt[:, :rows].astype(BF16)
    grows = Q_PER_KV * rows
    row1 = lax.broadcasted_iota(jnp.int32, (grows, nb * WINDOW), 0)
    col1 = lax.broadcasted_iota(jnp.int32, (grows, nb * WINDOW), 1)
    rseq1 = (row1 % rows) // ts
    mask_ctx = (rseq1 == col1 // WINDOW) & (col1 % WINDOW >= row1 % ts)
    row2 = lax.broadcasted_iota(jnp.int32, (grows, rows), 0)
    col2 = lax.broadcasted_iota(jnp.int32, (grows, rows), 1)
    mask_new = ((row2 % rows) // ts == col2 // ts) & (col2 % ts <= row2 % ts)
    ghead = lax.broadcasted_iota(jnp.int32, (grows, 1), 0) // rows
    for kv in range(N_KV_HEADS):
        ksl = slice(kv * HEAD_DIM, (kv + 1) * HEAD_DIM)
        qg = jnp.concatenate([q[:, (kv * Q_PER_KV + g) * HEAD_DIM:(kv * Q_PER_KV + g + 1) * HEAD_DIM]
                              for g in range(Q_PER_KV)], axis=0)
        kctx = jnp.concatenate([ckt_ref[b, kv] for b in range(nb)], axis=1).astype(BF16)
        vctx = jnp.concatenate([cvt_ref[b, kv] for b in range(nb)], axis=1).astype(BF16)
        s1 = jnp.where(mask_ctx, _dot(qg, kctx), NEG_INF)
        s2 = jnp.where(mask_new, _dot(qg, knt[ksl]), NEG_INF)
        sink = jnp.zeros((grows, 1), F32)
        for g in range(Q_PER_KV):
            sink = jnp.where(ghead == g, sinks_ref[kv * Q_PER_KV + g], sink)
        m = jnp.maximum(jnp.maximum(jnp.max(s1, axis=-1, keepdims=True), jnp.max(s2, axis=-1, keepdims=True)), sink)
        p1 = jnp.exp(s1 - m)
        p2 = jnp.exp(s2 - m)
        denom = jnp.sum(p1, axis=-1, keepdims=True) + jnp.sum(p2, axis=-1, keepdims=True) + jnp.exp(sink - m)
        o = (_dot_nt(p1.astype(BF16), vctx) + _dot_nt(p2.astype(BF16), vnt[ksl])) / denom
        for g in range(Q_PER_KV):
            h = kv * Q_PER_KV + g
            o_ref[:, h * HEAD_DIM:(h + 1) * HEAD_DIM] = o[g * rows:(g + 1) * rows].astype(BF16)


def _attn_sample(sinks, q, k, v, ckt, cvt, *, nb, ts):
    nseq = ckt.shape[0]
    rows = nb * ts
    assert rows <= LANES
    row = lambda w: pl.BlockSpec((rows, w), lambda i: (i, 0))
    win = pl.BlockSpec((nb, N_KV_HEADS, HEAD_DIM, WINDOW), lambda i: (i, 0, 0, 0))
    win_shape = jax.ShapeDtypeStruct((nseq, N_KV_HEADS, HEAD_DIM, WINDOW), F32)
    return pl.pallas_call(
        functools.partial(_attn_sample_kernel, nb=nb, ts=ts),
        grid=(nseq // nb,),
        in_specs=[pl.BlockSpec(memory_space=pltpu.SMEM), row(ATTN_WIDTH), row(KV_WIDTH), row(KV_WIDTH), win, win],
        out_specs=[row(ATTN_WIDTH), win, win],
        out_shape=[jax.ShapeDtypeStruct((nseq * ts, ATTN_WIDTH), BF16), win_shape, win_shape],
        compiler_params=_cparams(("parallel",)),
        name="attn_sample",
    )(sinks, q, k, v, ckt, cvt)


def _ssm_prep_kernel(are_ref, aim_ref, ldt_ref, bre_ref, bim_ref, lre_ref, lim_ref, bbre_ref, bbim_ref):
    a_re = are_ref[...]
    a_im = aim_ref[...]
    dt = jnp.exp(ldt_ref[...])
    mag = jnp.exp(a_re * dt)
    lb_re = mag * jnp.cos(a_im * dt)
    lb_im = mag * jnp.sin(a_im * dt)
    den = a_re * a_re + a_im * a_im
    nr = lb_re - 1.0
    ni = lb_im
    k_re = (nr * a_re + ni * a_im) / den
    k_im = (ni * a_re - nr * a_im) / den
    lre_ref[...] = lb_re
    lim_ref[...] = lb_im
    b_re = bre_ref[...]
    b_im = bim_ref[...]
    bbre_ref[...] = k_re[:, None, :] * b_re - k_im[:, None, :] * b_im
    bbim_ref[...] = k_re[:, None, :] * b_im + k_im[:, None, :] * b_re


def _ssm_prep(a_re, a_im, log_dt, b_re, b_im):
    g, n, gs = b_re.shape
    sds = jax.ShapeDtypeStruct
    return pl.pallas_call(
        _ssm_prep_kernel,
        out_shape=[sds((g, n), F32), sds((g, n), F32), sds((g, gs, n), F32), sds((g, gs, n), F32)],
        name="ssm_prep",
    )(a_re, a_im, log_dt.reshape(g, 1), jnp.swapaxes(b_re, 1, 2), jnp.swapaxes(b_im, 1, 2))


def _block_diag_halves(m):
    g, a, b = m.shape
    gh = g // 2
    eye = jnp.eye(gh, dtype=m.dtype)
    return jnp.einsum('hgab,gk->hgakb', m.reshape(2, gh, a, b), eye).reshape(2, gh * a, gh * b)


def _ssm_y(u, scan_block, cc_ref, d_ref):
    ys = []
    for half in range(2):
        hc = jnp.concatenate([scan_block(half, jj) for jj in range(BLK_PER_HALF)], axis=1).astype(BF16)
        ys.append(_dot(hc, cc_ref[half]))
    y = jnp.concatenate(ys, axis=1) + d_ref[...] * u
    return jax.nn.gelu(y).astype(BF16)


def _ssm_prompt_kernel(u_ref, bb_ref, cc_ref, lam_ref, d_ref, h0_ref, z_ref, hout_ref, xs_ref, hst_ref):
    tb, pitch = SSM_TB, SSM_PITCH

    @pl.when(pl.program_id(0) == 0)
    def _():
        hst_ref[...] = h0_ref[...]

    u = u_ref[...]
    ub = u.astype(BF16)
    for half in range(2):
        x = _dot(ub[:, half * HALF_U:(half + 1) * HALF_U], bb_ref[half])
        for jj in range(BLK_PER_HALF):
            xs_ref[pl.ds((half * BLK_PER_HALF + jj) * pitch, tb), :] = x[:, jj * LANES:(jj + 1) * LANES]

    lam = [lam_ref[g] for g in range(4)]

    def step(t, carry):
        new = []
        for half in range(2):
            hr, hi = carry[2 * half], carry[2 * half + 1]
            lr, li = lam[2 * half], lam[2 * half + 1]
            rows_re = pl.ds((half * BLK_PER_HALF) * pitch + t, SUBLANES, stride=pitch)
            rows_im = pl.ds((half * BLK_PER_HALF + SUBLANES) * pitch + t, SUBLANES, stride=pitch)
            nr = lr * hr - li * hi + xs_ref[rows_re, :]
            ni = lr * hi + li * hr + xs_ref[rows_im, :]
            xs_ref[rows_re, :] = nr
            xs_ref[rows_im, :] = ni
            new += [nr, ni]
        return tuple(new)

    carry = lax.fori_loop(0, tb, step, tuple(hst_ref[g] for g in range(4)), unroll=8)
    for g in range(4):
        hst_ref[g] = carry[g]
        hout_ref[g] = carry[g]

    z_ref[...] = _ssm_y(u, lambda half, jj: xs_ref[pl.ds((half * BLK_PER_HALF + jj) * pitch, tb), :], cc_ref, d_ref)


def _ssm_prompt(u, bb, cc, lam, d, h0):
    t = u.shape[0]
    return pl.pallas_call(
        _ssm_prompt_kernel,
        grid=(t // SSM_TB,),
        in_specs=[pl.BlockSpec((SSM_TB, SSM_WIDTH), lambda i: (i, 0)), _const_spec(bb.shape), _const_spec(cc.shape),
                  _const_spec(lam.shape), _const_spec(d.shape), _const_spec(h0.shape)],
        out_specs=[pl.BlockSpec((SSM_TB, SSM_WIDTH), lambda i: (i, 0)),
                   pl.BlockSpec((4, SUBLANES, LANES), lambda i: (0, 0, 0))],
        out_shape=[jax.ShapeDtypeStruct((t, SSM_WIDTH), BF16), jax.ShapeDtypeStruct((4, SUBLANES, LANES), F32)],
        scratch_shapes=[pltpu.VMEM((2 * BLK_PER_HALF * SSM_PITCH, LANES), F32),
                        pltpu.VMEM((4, SUBLANES, LANES), F32)],
        compiler_params=_cparams(("arbitrary",)),
        name="ssm_prompt",
    )(u, bb, cc, lam, d, h0)


def _ssm_sample_kernel(u_ref, bb_ref, cc_ref, lre_ref, lim_ref, d_ref, h0re_ref, h0im_ref,
                       z_ref, hre_ref, him_ref, xs_ref, *, nb, ts):
    u = u_ref[...]
    ub = u.astype(BF16)
    for half in range(2):
        x = _dot(ub[:, half * HALF_U:(half + 1) * HALF_U], bb_ref[half])
        for jj in range(BLK_PER_HALF):
            xs_ref[half * BLK_PER_HALF + jj] = x[:, jj * LANES:(jj + 1) * LANES]
    for cb in range(SSM_CH // LANES):
        half, jj = divmod(cb, SUBLANES)
        j_re = half * BLK_PER_HALF + jj
        j_im = j_re + SUBLANES
        csl = slice(cb * LANES, (cb + 1) * LANES)
        lr = lre_ref[:, csl]
        li = lim_ref[:, csl]
        hr = h0re_ref[:, csl]
        hi = h0im_ref[:, csl]
        for t in range(ts):
            rows = pl.ds(t, nb, stride=ts)
            nr = lr * hr - li * hi + xs_ref[j_re, rows, :]
            ni = lr * hi + li * hr + xs_ref[j_im, rows, :]
            xs_ref[j_re, rows, :] = nr
            xs_ref[j_im, rows, :] = ni
            hr, hi = nr, ni
        hre_ref[:, csl] = hr
        him_ref[:, csl] = hi
    z_ref[...] = _ssm_y(u, lambda half, jj: xs_ref[half * BLK_PER_HALF + jj], cc_ref, d_ref)


def _ssm_sample(u, bb, cc, lre, lim, d, h0re, h0im, *, nb, ts):
    nseq = h0re.shape[0]
    rows = nb * ts
    st = pl.BlockSpec((nb, SSM_CH), lambda i: (i, 0))
    return pl.pallas_call(
        functools.partial(_ssm_sample_kernel, nb=nb, ts=ts),
        grid=(nseq // nb,),
        in_specs=[pl.BlockSpec((rows, SSM_WIDTH), lambda i: (i, 0)), _const_spec(bb.shape), _const_spec(cc.shape),
                  _const_spec(lre.shape), _const_spec(lim.shape), _const_spec(d.shape), st, st],
        out_specs=[pl.BlockSpec((rows, SSM_WIDTH), lambda i: (i, 0)), st, st],
        out_shape=[jax.ShapeDtypeStruct((nseq * ts, SSM_WIDTH), BF16),
                   jax.ShapeDtypeStruct((nseq, SSM_CH), F32), jax.ShapeDtypeStruct((nseq, SSM_CH), F32)],
        scratch_shapes=[pltpu.VMEM((2 * BLK_PER_HALF, rows, LANES), F32)],
        compiler_params=_cparams(("parallel",)),
        name="ssm_sample",
    )(u, bb, cc, lre, lim, d, h0re, h0im)


def _merge_kernel(x1_ref, o_ref, z_ref, ga_ref, gb_ref, wup_ref, wglu_ref, wout_ref, gx_ref, wxq_ref,
                  x2_ref, qx_ref):
    ya = _dot(o_ref[...], wup_ref[...])
    glu = _dot(z_ref[...], wglu_ref[...])
    yb = glu[:, :D_MODEL] * jax.nn.sigmoid(glu[:, D_MODEL:])
    merged = ga_ref[...].astype(F32) * ya + gb_ref[...].astype(F32) * yb
    x2 = x1_ref[...] + _dot(merged.astype(BF16), wout_ref[...])
    x2_ref[...] = x2
    hx = _rms(x2, gx_ref[...]).astype(BF16)
    qx_ref[...] = (_dot(hx, wxq_ref[...]) * (X_HEAD_DIM ** -0.5)).astype(BF16)


def _merge(x1, o, z, ga, gb, wup, wglu, wout, gx, wxq, *, tm):
    t = x1.shape[0]
    row = lambda w: pl.BlockSpec((tm, w), lambda i: (i, 0))
    return pl.pallas_call(
        _merge_kernel,
        grid=(t // tm,),
        in_specs=[row(D_MODEL), row(ATTN_WIDTH), row(SSM_WIDTH), row(D_MODEL), row(D_MODEL),
                  _const_spec(wup.shape), _const_spec(wglu.shape), _const_spec(wout.shape), _const_spec(gx.shape),
                  _const_spec(wxq.shape)],
        out_specs=[row(D_MODEL), row(D_MODEL)],
        out_shape=[jax.ShapeDtypeStruct((t, D_MODEL), F32), jax.ShapeDtypeStruct((t, D_MODEL), BF16)],
        compiler_params=_cparams(("parallel",)),
        name="merge",
    )(x1, o, z, ga, gb, wup, wglu, wout, gx, wxq)


def _memkv_kernel(mem_ref, g_ref, wk_ref, wv_ref, mk_ref, mv_ref):
    mn = _rms(mem_ref[...], g_ref[...]).astype(BF16)
    mk_ref[...] = _dot(mn, wk_ref[...])
    mv_ref[...] = _dot(mn, wv_ref[...])


def _memkv(mem, g, wk, wv):
    sds = jax.ShapeDtypeStruct((mem.shape[0], D_MODEL), F32)
    return pl.pallas_call(_memkv_kernel, out_shape=[sds, sds], name="memkv",
                          compiler_params=pltpu.CompilerParams(vmem_limit_bytes=VMEM_LIMIT))(mem, g, wk, wv)


def _xattn_heads(q, mk, mv):
    outs = []
    for h in range(N_X_HEADS):
        sl = slice(h * X_HEAD_DIM, (h + 1) * X_HEAD_DIM)
        s = _dot_nt(q[:, sl], mk[:, sl])
        p = jnp.exp(s - jnp.max(s, axis=-1, keepdims=True))
        outs.append(_dot(p.astype(BF16), mv[:, sl]) / jnp.sum(p, axis=-1, keepdims=True))
    return jnp.concatenate(outs, axis=1)


def _xattn_prompt_kernel(q_ref, mk_ref, mv_ref, o_ref):
    o_ref[...] = _xattn_heads(q_ref[...], mk_ref[...].astype(BF16), mv_ref[...].astype(BF16)).astype(BF16)


def _xattn_prompt(qx, mk, mv, *, tm):
    t = qx.shape[0]
    row = pl.BlockSpec((tm, D_MODEL), lambda i: (i, 0))
    return pl.pallas_call(
        _xattn_prompt_kernel,
        grid=(t // tm,),
        in_specs=[row, _const_spec(mk.shape), _const_spec(mv.shape)],
        out_specs=row,
        out_shape=jax.ShapeDtypeStruct((t, D_MODEL), BF16),
        compiler_params=_cparams(("parallel",)),
        name="xattn_prompt",
    )(qx, mk, mv)


def _xattn_sample_kernel(q_ref, mk_ref, mv_ref, o_ref, *, nb, ts):
    qrows = ts * N_X_HEADS
    krows = N_MEM * N_X_HEADS
    row = lax.broadcasted_iota(jnp.int32, (qrows, krows), 0)
    col = lax.broadcasted_iota(jnp.int32, (qrows, krows), 1)
    same_head = (row & (N_X_HEADS - 1)) == (col & (N_X_HEADS - 1))
    for b in range(nb):
        kf = mk_ref[b].reshape(krows, X_HEAD_DIM).astype(BF16)
        vf = mv_ref[b].reshape(krows, X_HEAD_DIM).astype(BF16)
        s = jnp.where(same_head, _dot_nt(q_ref[b], kf), NEG_INF)
        p = jnp.exp(s - jnp.max(s, axis=-1, keepdims=True))
        o = _dot(p.astype(BF16), vf) / jnp.sum(p, axis=-1, keepdims=True)
        o_ref[b] = o.astype(BF16)


def _xattn_sample(qx, mk, mv, *, nb, ts):
    nseq = mk.shape[0]
    row = pl.BlockSpec((nb, ts * N_X_HEADS, X_HEAD_DIM), lambda i: (i, 0, 0))
    mem = pl.BlockSpec((nb, N_MEM, N_X_HEADS, X_HEAD_DIM), lambda i: (i, 0, 0, 0))
    return pl.pallas_call(
        functools.partial(_xattn_sample_kernel, nb=nb, ts=ts),
        grid=(nseq // nb,),
        in_specs=[row, mem, mem],
        out_specs=row,
        out_shape=jax.ShapeDtypeStruct((nseq, ts * N_X_HEADS, X_HEAD_DIM), BF16),
        compiler_params=_cparams(("parallel",)),
        name="xattn_sample",
    )(qx, mk, mv)


def kernel(x_prompt, x_sample, cache_win_k, cache_win_v, state_ssm_re, state_ssm_im, cache_mem_k, cache_mem_v, mem_prompt, g_ffn1, w_ffn1_in, w_ffn1_out, g_mix, w_in, attn_sinks, ssm_a_re, ssm_a_im, ssm_log_dt, ssm_b_re, ssm_b_im, ssm_c_re, ssm_c_im, ssm_d, w_attn_up, w_ssm_glu, w_out, g_xattn, g_mem, w_xq, w_xk, w_xv, w_xo, g_ffn2, w_ffn2_in, w_ffn2_out, g_final):
    assert x_prompt.shape[0] == 1 and g_ffn1.shape[0] == 1
    seq = x_prompt.shape[1]
    nseq, ts = x_sample.shape[0], x_sample.shape[1]
    past_len = seq
    l = 0
    bf = lambda w: w[l].astype(BF16)
    vec = lambda g: g[l].reshape(1, -1)
    w1i, w1o, w2i, w2o = bf(w_ffn1_in), bf(w_ffn1_out), bf(w_ffn2_in), bf(w_ffn2_out)
    win, wup, wglu, wout = bf(w_in), bf(w_attn_up), bf(w_ssm_glu), bf(w_out)
    wxq, wxk, wxv, wxo = bf(w_xq), bf(w_xk), bf(w_xv), bf(w_xo)
    gf = g_final.reshape(1, -1)
    sinks = attn_sinks[l]

    lam_re, lam_im, bbt_re, bbt_im = _ssm_prep(ssm_a_re[l], ssm_a_im[l], ssm_log_dt[l], ssm_b_re[l], ssm_b_im[l])
    bb = jnp.concatenate([_block_diag_halves(bbt_re), _block_diag_halves(bbt_im)], axis=-1).astype(BF16)
    ct_re = jnp.swapaxes(ssm_c_re[l], 1, 2)
    ct_im = jnp.swapaxes(ssm_c_im[l], 1, 2)
    cc = jnp.concatenate([_block_diag_halves(ct_re), -_block_diag_halves(ct_im)], axis=1).astype(BF16)
    d_skip = ssm_d[l].reshape(1, -1)
    lre16 = lam_re.reshape(2, SUBLANES, LANES)
    lim16 = lam_im.reshape(2, SUBLANES, LANES)
    lam_tm = jnp.stack([lre16[0], lim16[0], lre16[1], lim16[1]])
    lre_row = lam_re.reshape(1, SSM_CH)
    lim_row = lam_im.reshape(1, SSM_CH)

    mk_p, mv_p = _memkv(mem_prompt[0], vec(g_mem), wxk, wxv)

    def group(x, cos, sin, attn_fn, ssm_fn, xattn_fn, tm):
        x1 = _ffn(x, vec(g_ffn1), w1i, w1o, gf, tm=tm)
        q, k, v, u, ga, gb = _inproj(x1, vec(g_mix), win, cos, sin, tm=tm)
        o, extra_attn = attn_fn(q, k, v)
        z, extra_ssm = ssm_fn(u)
        x2, qx = _merge(x1, o, z, ga, gb, wup, wglu, wout, vec(g_xattn), wxq, tm=tm)
        ox = xattn_fn(qx)
        y = _ffn(x2, vec(g_ffn2), w2i, w2o, gf, tm=tm, pre=(ox, wxo), final=True)
        return y, extra_attn, extra_ssm

    cos_p, sin_p = _rope_tables(seq, 0.0, 1)

    def attn_p(q, k, v):
        return _attn_prompt(sinks, q, k, v), (k[-WINDOW:], v[-WINDOW:])

    def ssm_p(u):
        z, hout = _ssm_prompt(u, bb, cc, lam_tm, d_skip, jnp.zeros((4, SUBLANES, LANES), F32))
        return z, hout

    y_p, (nk_p, nv_p), hout_p = group(x_prompt[0], cos_p, sin_p, attn_p, ssm_p,
                                      lambda qx: _xattn_prompt(qx, mk_p, mv_p, tm=512), 512)
    hre_p = jnp.concatenate([hout_p[0], hout_p[2]], axis=0)
    him_p = jnp.concatenate([hout_p[1], hout_p[3]], axis=0)

    cos_s, sin_s = _rope_tables(ts, float(past_len), nseq)

    def attn_s(q, k, v):
        to_t = lambda c: jnp.transpose(c[l], (0, 2, 3, 1))
        o, nkt, nvt = _attn_sample(sinks, q, k, v, to_t(cache_win_k), to_t(cache_win_v), nb=8, ts=ts)
        return o, (jnp.transpose(nkt, (0, 3, 1, 2)), jnp.transpose(nvt, (0, 3, 1, 2)))

    def ssm_s(u):
        z, hre, him = _ssm_sample(u, bb, cc, lre_row, lim_row, d_skip, state_ssm_re[l].reshape(nseq, SSM_CH),
                                  state_ssm_im[l].reshape(nseq, SSM_CH), nb=64, ts=ts)
        return z, (hre, him)

    y_s, (nk_s, nv_s), (hre_s, him_s) = group(
        x_sample.reshape(nseq * ts, D_MODEL), cos_s, sin_s, attn_s, ssm_s,
        lambda qx: _xattn_sample(qx.reshape(nseq, ts * N_X_HEADS, X_HEAD_DIM), cache_mem_k[l], cache_mem_v[l],
                                 nb=4, ts=ts).reshape(nseq * ts, D_MODEL), 512)

    kvshape = (1, 1, WINDOW, N_KV_HEADS, HEAD_DIM)
    stshape = (1, 1, N_SSM_GROUPS, SSM_STATE)
    memshape = (1, 1, N_MEM, N_X_HEADS, X_HEAD_DIM)
    return (y_p.reshape(1, seq, D_MODEL), y_s.reshape(nseq, ts, D_MODEL),
            nk_p.reshape(kvshape), nv_p.reshape(kvshape), hre_p.reshape(stshape), him_p.reshape(stshape),
            mk_p.reshape(memshape), mv_p.reshape(memshape),
            nk_s.reshape(1, nseq, WINDOW, N_KV_HEADS, HEAD_DIM), nv_s.reshape(1, nseq, WINDOW, N_KV_HEADS, HEAD_DIM),
            hre_s.reshape(1, nseq, N_SSM_GROUPS, SSM_STATE), him_s.reshape(1, nseq, N_SSM_GROUPS, SSM_STATE))
```

```python
import functools
import math

import jax
import jax.numpy as jnp
from jax import lax
from jax.experimental import pallas as pl
from jax.experimental.pallas import tpu as pltpu

F32 = jnp.float32
BF16 = jnp.bfloat16

D_MODEL = 1024
N_Q_HEADS = 8
N_KV_HEADS = 2
HEAD_DIM = 64
Q_PER_KV = N_Q_HEADS // N_KV_HEADS
ATTN_WIDTH = N_Q_HEADS * HEAD_DIM
KV_WIDTH = N_KV_HEADS * HEAD_DIM
WINDOW = 128
ROPE_THETA = 10000.0
SSM_WIDTH = D_MODEL // 2
SSM_GROUP = 16
N_SSM_GROUPS = SSM_WIDTH // SSM_GROUP
SSM_STATE = 64
N_MEM = 256
N_X_HEADS = 4
X_HEAD_DIM = D_MODEL // N_X_HEADS
D_FF = 2816
RMS_EPS = 1e-6
NEG_INF = -1e30
IN_SPLITS = (ATTN_WIDTH, KV_WIDTH, KV_WIDTH, SSM_WIDTH, D_MODEL, D_MODEL)
IN_WIDTH = sum(IN_SPLITS)
OFF_Q, OFF_K, OFF_V, OFF_U, OFF_GA, OFF_GB = (0, 512, 640, 768, 1280, 2304)

LANES = 128
SUBLANES = 8
VMEM_LIMIT = 56 * 1024 * 1024

SSM_CH = N_SSM_GROUPS * SSM_STATE
HALF_CH = SSM_CH // 2
HALF_U = SSM_WIDTH // 2
BLK_PER_HALF = 2 * HALF_CH // LANES
ATTN_QBLOCKS = 4
SSM_TB = 256
SSM_NB = 2
SSM_PITCH = SSM_TB + 4


def _cparams(sem):
    return pltpu.CompilerParams(dimension_semantics=sem, vmem_limit_bytes=VMEM_LIMIT)


def _const_spec(shape):
    nd = len(shape)
    return pl.BlockSpec(shape, lambda *_: (0,) * nd, pipeline_mode=pl.Buffered(1))


def _rms(x, g):
    return x * lax.rsqrt(jnp.mean(x * x, axis=-1, keepdims=True) + RMS_EPS) * g


def _dot(a, b):
    return jnp.dot(a, b, preferred_element_type=F32)


def _dot_nt(a, b):
    return lax.dot_general(a, b, (((1,), (1,)), ((), ())), preferred_element_type=F32)


def _ffn_kernel(*refs, has_pre, final):
    if has_pre:
        x_ref, ox_ref, wxo_ref, g_ref, wi_ref, wo_ref, gf_ref, o_ref = refs
        x = x_ref[...] + _dot(ox_ref[...], wxo_ref[...])
    else:
        x_ref, g_ref, wi_ref, wo_ref, gf_ref, o_ref = refs
        x = x_ref[...]
    h = _rms(x, g_ref[...]).astype(BF16)
    a = _dot(h, wi_ref[:, :D_FF])
    b = _dot(h, wi_ref[:, D_FF:])
    act = (a * jax.nn.sigmoid(a) * b).astype(BF16)
    y = x + 0.5 * _dot(act, wo_ref[...])
    if final:
        y = _rms(y, gf_ref[...])
    o_ref[...] = y


def _ffn(x, g, wi, wo, gf, *, tm, pre=None, final=False):
    t = x.shape[0]
    row = lambda w: pl.BlockSpec((tm, w), lambda i: (i, 0))
    in_specs = [row(D_MODEL)]
    args = [x]
    if pre is not None:
        ox, wxo = pre
        in_specs += [row(D_MODEL), _const_spec((D_MODEL, D_MODEL))]
        args += [ox, wxo]
    in_specs += [_const_spec((1, D_MODEL)), _const_spec((D_MODEL, 2 * D_FF)), _const_spec((D_FF, D_MODEL)),
                 _const_spec((1, D_MODEL))]
    args += [g, wi, wo, gf]
    return pl.pallas_call(
        functools.partial(_ffn_kernel, has_pre=pre is not None, final=final),
        grid=(t // tm,),
        in_specs=in_specs,
        out_specs=row(D_MODEL),
        out_shape=jax.ShapeDtypeStruct((t, D_MODEL), F32),
        compiler_params=_cparams(("parallel",)),
        name="tail" if pre is not None else "ffn",
    )(*args)


def _inproj_kernel(x_ref, g_ref, w_ref, cos_ref, sin_ref, q_ref, k_ref, v_ref, u_ref, ga_ref, gb_ref):
    h = _rms(x_ref[...], g_ref[...]).astype(BF16)
    cos = cos_ref[...]
    sin = sin_ref[...]
    lane = lax.broadcasted_iota(jnp.int32, cos.shape, 1)
    first_half = (lane & (HEAD_DIM - 1)) < (HEAD_DIM // 2)

    def rope(xc):
        rot = jnp.where(first_half, pltpu.roll(xc, LANES - HEAD_DIM // 2, 1), pltpu.roll(xc, HEAD_DIM // 2, 1))
        return xc * cos + rot * sin

    yq = _dot(h, w_ref[:, OFF_Q:OFF_K])
    for c in range(ATTN_WIDTH // LANES):
        q_ref[:, c * LANES:(c + 1) * LANES] = (rope(yq[:, c * LANES:(c + 1) * LANES]) * (HEAD_DIM ** -0.5)).astype(BF16)
    ykvu = _dot(h, w_ref[:, OFF_K:OFF_GA])
    k_ref[...] = rope(ykvu[:, :KV_WIDTH])
    v_ref[...] = ykvu[:, KV_WIDTH:2 * KV_WIDTH]
    u_ref[...] = ykvu[:, 2 * KV_WIDTH:]
    ga_ref[...] = jax.nn.sigmoid(_dot(h, w_ref[:, OFF_GA:OFF_GB])).astype(BF16)
    gb_ref[...] = jax.nn.sigmoid(_dot(h, w_ref[:, OFF_GB:])).astype(BF16)


def _inproj(x, g, w, cos, sin, *, tm):
    t = x.shape[0]
    row = lambda wd: pl.BlockSpec((tm, wd), lambda i: (i, 0))
    widths = (ATTN_WIDTH, KV_WIDTH, KV_WIDTH, SSM_WIDTH, D_MODEL, D_MODEL)
    dtypes = (BF16, F32, F32, F32, BF16, BF16)
    return pl.pallas_call(
        _inproj_kernel,
        grid=(t // tm,),
        in_specs=[row(D_MODEL), _const_spec((1, D_MODEL)), _const_spec((D_MODEL, IN_WIDTH)), row(LANES), row(LANES)],
        out_specs=[row(wd) for wd in widths],
        out_shape=[jax.ShapeDtypeStruct((t, wd), dt) for wd, dt in zip(widths, dtypes)],
        compiler_params=_cparams(("parallel",)),
        name="inproj",
    )(x, g, w, cos, sin)


def _rope_tables(t, offset, repeat):
    half = HEAD_DIM // 2
    inv = ROPE_THETA ** (-jnp.arange(half, dtype=F32) / half)
    pos = jnp.arange(t, dtype=F32) + offset
    ang = pos[:, None] * inv[None, :]
    cos = jnp.cos(ang)
    sin = jnp.sin(ang)
    cos = jnp.tile(cos, (repeat, LANES // half))
    sin = jnp.tile(jnp.concatenate([-sin, sin], axis=1), (repeat, LANES // HEAD_DIM))
    return cos, sin


def _attn_prompt_kernel(sinks_ref, q_ref, kp_ref, kc_ref, vp_ref, vc_ref, o_ref):
    i = pl.program_id(0)
    r = lax.broadcasted_iota(jnp.int32, (WINDOW, 2 * WINDOW), 0)
    c = lax.broadcasted_iota(jnp.int32, (WINDOW, 2 * WINDOW), 1)
    band = (c >= r) & (c <= WINDOW + r)
    first_mask = band & ((c >= WINDOW) | (i > 0))
    kcat = jnp.concatenate([kp_ref[...], kc_ref[...]], axis=0)
    vcat = jnp.concatenate([vp_ref[...], vc_ref[...]], axis=0)
    kswap = pltpu.roll(kcat, HEAD_DIM, 1)
    vswap = pltpu.roll(vcat, HEAD_DIM, 1)
    lo = lax.broadcasted_iota(jnp.int32, kcat.shape, 1) < HEAD_DIM
    lo_q = lax.broadcasted_iota(jnp.int32, (WINDOW, LANES), 1) < HEAD_DIM
    ones = jnp.ones(kcat.shape, F32)
    for kv in range(N_KV_HEADS):
        k_own, k_other = (kcat, kswap) if kv == 0 else (kswap, kcat)
        v_own, v_other = (vcat, vswap) if kv == 0 else (vswap, vcat)
        k_half = (jnp.where(lo, k_own, 0.0).astype(BF16), jnp.where(lo, 0.0, k_other).astype(BF16))
        rhs_half = (jnp.concatenate([jnp.where(lo, v_own, 0.0), jnp.where(lo, ones, 0.0)], axis=1).astype(BF16),
                    jnp.concatenate([jnp.where(lo, 0.0, v_other), jnp.where(lo, 0.0, ones)], axis=1).astype(BF16))
        for j in range(ATTN_QBLOCKS):
            qrows = slice(j * WINDOW, (j + 1) * WINDOW)
            krows = slice(j * WINDOW, (j + 2) * WINDOW)
            mask = first_mask if j == 0 else band
            for c in range(Q_PER_KV // 2):
                tile = kv * (Q_PER_KV // 2) + c
                qt = q_ref[qrows, tile * LANES:(tile + 1) * LANES]
                acc = None
                sink_terms = []
                for side in range(2):
                    s = jnp.where(mask, _dot_nt(qt, k_half[side][krows]), NEG_INF)
                    sink = sinks_ref[2 * tile + side]
                    m = jnp.maximum(jnp.max(s, axis=-1, keepdims=True), sink)
                    pv = _dot(jnp.exp(s - m).astype(BF16), rhs_half[side][krows])
                    acc = pv if acc is None else acc + pv
                    sink_terms.append(jnp.exp(sink - m))
                denom = acc[:, LANES:] + jnp.where(lo_q, sink_terms[0], sink_terms[1])
                o_ref[qrows, tile * LANES:(tile + 1) * LANES] = (acc[:, :LANES] / denom).astype(BF16)


def _attn_prompt(sinks, q, k, v):
    t = q.shape[0]
    rows = ATTN_QBLOCKS * WINDOW
    cur = lambda w: pl.BlockSpec((rows, w), lambda i: (i, 0))
    prev = lambda w: pl.BlockSpec((WINDOW, w), lambda i: (jnp.maximum(i * ATTN_QBLOCKS - 1, 0), 0))
    return pl.pallas_call(
        _attn_prompt_kernel,
        grid=(t // rows,),
        in_specs=[pl.BlockSpec(memory_space=pltpu.SMEM), cur(ATTN_WIDTH), prev(KV_WIDTH), cur(KV_WIDTH),
                  prev(KV_WIDTH), cur(KV_WIDTH)],
        out_specs=cur(ATTN_WIDTH),
        out_shape=jax.ShapeDtypeStruct((t, ATTN_WIDTH), BF16),
        compiler_params=_cparams(("parallel",)),
        name="attn_prompt",
    )(sinks, q, k, k, v, v)


def _attn_sample_kernel(sinks_ref, q_ref, k_ref, v_ref, ckt_ref, cvt_ref, o_ref, nkt_ref, nvt_ref, *, nb, ts):
    rows = nb * ts
    q = q_ref[...]
    pad = jnp.zeros((LANES - rows, KV_WIDTH), F32)
    knt = jnp.concatenate([k_ref[...], pad], axis=0).T
    vnt = jnp.concatenate([v_ref[...], pad], axis=0).T
    lane = lax.broadcasted_iota(jnp.int32, (HEAD_DIM, WINDOW), 1)
    is_new = lane >= WINDOW - ts
    for b in range(nb):
        knt_b = pltpu.roll(knt, (WINDOW - ts - b * ts) % LANES, 1)
        vnt_b = pltpu.roll(vnt, (WINDOW - ts - b * ts) % LANES, 1)
        for kv in range(N_KV_HEADS):
            ksl = slice(kv * HEAD_DIM, (kv + 1) * HEAD_DIM)
            nkt_ref[b, kv] = jnp.where(is_new, knt_b[ksl], pltpu.roll(ckt_ref[b, kv], WINDOW - ts, 1))
            nvt_ref[b, kv] = jnp.where(is_new, vnt_b[ksl], pltpu.roll(cvt_ref[b, kv], WINDOW - ts, 1))

    knt = knt[:, :rows].astype(BF16)
    vnt = vnt[:, :rows].astype(BF16)
    grows = Q_PER_KV * rows
    row1 = lax.broadcasted_iota(jnp.int32, (grows, nb * WINDOW), 0)
    col1 = lax.broadcasted_iota(jnp.int32, (grows, nb * WINDOW), 1)
    rseq1 = (row1 % rows) // ts
    mask_ctx = (rseq1 == col1 // WINDOW) & (col1 % WINDOW >= row1 % ts)
    row2 = lax.broadcasted_iota(jnp.int32, (grows, rows), 0)
    col2 = lax.broadcasted_iota(jnp.int32, (grows, rows), 1)
    mask_new = ((row2 % rows) // ts == col2 // ts) & (col2 % ts <= row2 % ts)
    ghead = lax.broadcasted_iota(jnp.int32, (grows, 1), 0) // rows
    for kv in range(N_KV_HEADS):
        ksl = slice(kv * HEAD_DIM, (kv + 1) * HEAD_DIM)
        qg = jnp.concatenate([q[:, (kv * Q_PER_KV + g) * HEAD_DIM:(kv * Q_PER_KV + g + 1) * HEAD_DIM]
                              for g in range(Q_PER_KV)], axis=0)
        kctx = jnp.concatenate([ckt_ref[b, kv] for b in range(nb)], axis=1).astype(BF16)
        vctx = jnp.concatenate([cvt_ref[b, kv] for b in range(nb)], axis=1).astype(BF16)
        s1 = jnp.where(mask_ctx, _dot(qg, kctx), NEG_INF)
        s2 = jnp.where(mask_new, _dot(qg, knt[ksl]), NEG_INF)
        sink = jnp.zeros((grows, 1), F32)
        for g in range(Q_PER_KV):
            sink = jnp.where(ghead == g, sinks_ref[kv * Q_PER_KV + g], sink)
        m = jnp.maximum(jnp.maximum(jnp.max(s1, axis=-1, keepdims=True), jnp.max(s2, axis=-1, keepdims=True)), sink)
        p1 = jnp.exp(s1 - m)
        p2 = jnp.exp(s2 - m)
        denom = jnp.sum(p1, axis=-1, keepdims=True) + jnp.sum(p2, axis=-1, keepdims=True) + jnp.exp(sink - m)
        o = (_dot_nt(p1.astype(BF16), vctx) + _dot_nt(p2.astype(BF16), vnt[ksl])) / denom
        for g in range(Q_PER_KV):
            h = kv * Q_PER_KV + g
            o_ref[:, h * HEAD_DIM:(h + 1) * HEAD_DIM] = o[g * rows:(g + 1) * rows].astype(BF16)


def _attn_sample(sinks, q, k, v, ckt, cvt, *, nb, ts):
    nseq = ckt.shape[0]
    rows = nb * ts
    assert rows <= LANES
    row = lambda w: pl.BlockSpec((rows, w), lambda i: (i, 0))
    win = pl.BlockSpec((nb, N_KV_HEADS, HEAD_DIM, WINDOW), lambda i: (i, 0, 0, 0))
    win_shape = jax.ShapeDtypeStruct((nseq, N_KV_HEADS, HEAD_DIM, WINDOW), F32)
    return pl.pallas_call(
        functools.partial(_attn_sample_kernel, nb=nb, ts=ts),
        grid=(nseq // nb,),
        in_specs=[pl.BlockSpec(memory_space=pltpu.SMEM), row(ATTN_WIDTH), row(KV_WIDTH), row(KV_WIDTH), win, win],
        out_specs=[row(ATTN_WIDTH), win, win],
        out_shape=[jax.ShapeDtypeStruct((nseq * ts, ATTN_WIDTH), BF16), win_shape, win_shape],
        compiler_params=_cparams(("parallel",)),
        name="attn_sample",
    )(sinks, q, k, v, ckt, cvt)


def _ssm_prep_kernel(are_ref, aim_ref, ldt_ref, bre_ref, bim_ref, lre_ref, lim_ref, bbre_ref, bbim_ref):
    a_re = are_ref[...]
    a_im = aim_ref[...]
    dt = jnp.exp(ldt_ref[...])
    mag = jnp.exp(a_re * dt)
    lb_re = mag * jnp.cos(a_im * dt)
    lb_im = mag * jnp.sin(a_im * dt)
    den = a_re * a_re + a_im * a_im
    nr = lb_re - 1.0
    ni = lb_im
    k_re = (nr * a_re + ni * a_im) / den
    k_im = (ni * a_re - nr * a_im) / den
    lre_ref[...] = lb_re
    lim_ref[...] = lb_im
    b_re = bre_ref[...]
    b_im = bim_ref[...]
    bbre_ref[...] = k_re[:, None, :] * b_re - k_im[:, None, :] * b_im
    bbim_ref[...] = k_re[:, None, :] * b_im + k_im[:, None, :] * b_re


def _ssm_prep(a_re, a_im, log_dt, b_re, b_im):
    g, n, gs = b_re.shape
    sds = jax.ShapeDtypeStruct
    return pl.pallas_call(
        _ssm_prep_kernel,
        out_shape=[sds((g, n), F32), sds((g, n), F32), sds((g, gs, n), F32), sds((g, gs, n), F32)],
        name="ssm_prep",
    )(a_re, a_im, log_dt.reshape(g, 1), jnp.swapaxes(b_re, 1, 2), jnp.swapaxes(b_im, 1, 2))


def _block_diag_halves(m):
    g, a, b = m.shape
    gh = g // 2
    eye = jnp.eye(gh, dtype=m.dtype)
    return jnp.einsum('hgab,gk->hgakb', m.reshape(2, gh, a, b), eye).reshape(2, gh * a, gh * b)


def _ssm_y(u, scan_block, cc_ref, d_ref):
    ys = []
    for half in range(2):
        hc = jnp.concatenate([scan_block(half, jj) for jj in range(BLK_PER_HALF)], axis=1).astype(BF16)
        ys.append(_dot(hc, cc_ref[half]))
    y = jnp.concatenate(ys, axis=1) + d_ref[...] * u
    return jax.nn.gelu(y).astype(BF16)


def _ssm_prompt_kernel(u_ref, bb_ref, cc_ref, lam_ref, d_ref, h0_ref, z_ref, hout_ref,
                       x0_ref, x1_ref, s0_ref, s1_ref, hst_ref):
    tb, pitch = SSM_TB, SSM_PITCH
    x_bufs = (x0_ref, x1_ref)
    s_bufs = (s0_ref, s1_ref)

    @pl.when(pl.program_id(0) == 0)
    def _():
        hst_ref[...] = h0_ref[...]

    def b_proj(k):
        ub = u_ref[k * tb:(k + 1) * tb, :].astype(BF16)
        for half in range(2):
            x = _dot(ub[:, half * HALF_U:(half + 1) * HALF_U], bb_ref[half])
            for jj in range(BLK_PER_HALF):
                x_bufs[k % 2][pl.ds((half * BLK_PER_HALF + jj) * pitch, tb), :] = x[:, jj * LANES:(jj + 1) * LANES]

    lam = [lam_ref[g] for g in range(4)]
    carry = [hst_ref[g] for g in range(4)]

    def scan(k):
        for t in range(tb):
            for half in range(2):
                hr, hi = carry[2 * half], carry[2 * half + 1]
                lr, li = lam[2 * half], lam[2 * half + 1]
                rows_re = pl.ds((half * BLK_PER_HALF) * pitch + t, SUBLANES, stride=pitch)
                rows_im = pl.ds((half * BLK_PER_HALF + SUBLANES) * pitch + t, SUBLANES, stride=pitch)
                nr = lr * hr - li * hi + x_bufs[k % 2][rows_re, :]
                ni = lr * hi + li * hr + x_bufs[k % 2][rows_im, :]
                s_bufs[k % 2][rows_re, :] = nr
                s_bufs[k % 2][rows_im, :] = ni
                carry[2 * half], carry[2 * half + 1] = nr, ni

    def c_proj(k):
        z_ref[k * tb:(k + 1) * tb, :] = _ssm_y(
            u_ref[k * tb:(k + 1) * tb, :],
            lambda half, jj: s_bufs[k % 2][pl.ds((half * BLK_PER_HALF + jj) * pitch, tb), :], cc_ref, d_ref)

    b_proj(0)
    for k in range(SSM_NB):
        if k + 1 < SSM_NB:
            b_proj(k + 1)
        scan(k)
        c_proj(k)
    for g in range(4):
        hst_ref[g] = carry[g]
        hout_ref[g] = carry[g]


def _ssm_prompt(u, bb, cc, lam, d, h0):
    t = u.shape[0]
    rows = SSM_NB * SSM_TB
    return pl.pallas_call(
        _ssm_prompt_kernel,
        grid=(t // rows,),
        in_specs=[pl.BlockSpec((rows, SSM_WIDTH), lambda i: (i, 0)), _const_spec(bb.shape), _const_spec(cc.shape),
                  _const_spec(lam.shape), _const_spec(d.shape), _const_spec(h0.shape)],
        out_specs=[pl.BlockSpec((rows, SSM_WIDTH), lambda i: (i, 0)),
                   pl.BlockSpec((4, SUBLANES, LANES), lambda i: (0, 0, 0))],
        out_shape=[jax.ShapeDtypeStruct((t, SSM_WIDTH), BF16), jax.ShapeDtypeStruct((4, SUBLANES, LANES), F32)],
        scratch_shapes=[pltpu.VMEM((2 * BLK_PER_HALF * SSM_PITCH, LANES), F32)] * 4
                       + [pltpu.VMEM((4, SUBLANES, LANES), F32)],
        compiler_params=_cparams(("arbitrary",)),
        name="ssm_prompt",
    )(u, bb, cc, lam, d, h0)


def _ssm_sample_kernel(u_ref, bb_ref, cc_ref, lre_ref, lim_ref, d_ref, h0re_ref, h0im_ref,
                       z_ref, hre_ref, him_ref, xs_ref, *, nb, ts):
    u = u_ref[...]
    ub = u.astype(BF16)
    for half in range(2):
        x = _dot(ub[:, half * HALF_U:(half + 1) * HALF_U], bb_ref[half])
        for jj in range(BLK_PER_HALF):
            xs_ref[half * BLK_PER_HALF + jj] = x[:, jj * LANES:(jj + 1) * LANES]
    for cb in range(SSM_CH // LANES):
        half, jj = divmod(cb, SUBLANES)
        j_re = half * BLK_PER_HALF + jj
        j_im = j_re + SUBLANES
        csl = slice(cb * LANES, (cb + 1) * LANES)
        lr = lre_ref[:, csl]
        li = lim_ref[:, csl]
        hr = h0re_ref[:, csl]
        hi = h0im_ref[:, csl]
        for t in range(ts):
            rows = pl.ds(t, nb, stride=ts)
            nr = lr * hr - li * hi + xs_ref[j_re, rows, :]
            ni = lr * hi + li * hr + xs_ref[j_im, rows, :]
            xs_ref[j_re, rows, :] = nr
            xs_ref[j_im, rows, :] = ni
            hr, hi = nr, ni
        hre_ref[:, csl] = hr
        him_ref[:, csl] = hi
    z_ref[...] = _ssm_y(u, lambda half, jj: xs_ref[half * BLK_PER_HALF + jj], cc_ref, d_ref)


def _ssm_sample(u, bb, cc, lre, lim, d, h0re, h0im, *, nb, ts):
    nseq = h0re.shape[0]
    rows = nb * ts
    st = pl.BlockSpec((nb, SSM_CH), lambda i: (i, 0))
    return pl.pallas_call(
        functools.partial(_ssm_sample_kernel, nb=nb, ts=ts),
        grid=(nseq // nb,),
        in_specs=[pl.BlockSpec((rows, SSM_WIDTH), lambda i: (i, 0)), _const_spec(bb.shape), _const_spec(cc.shape),
                  _const_spec(lre.shape), _const_spec(lim.shape), _const_spec(d.shape), st, st],
        out_specs=[pl.BlockSpec((rows, SSM_WIDTH), lambda i: (i, 0)), st, st],
        out_shape=[jax.ShapeDtypeStruct((nseq * ts, SSM_WIDTH), BF16),
                   jax.ShapeDtypeStruct((nseq, SSM_CH), F32), jax.ShapeDtypeStruct((nseq, SSM_CH), F32)],
        scratch_shapes=[pltpu.VMEM((2 * BLK_PER_HALF, rows, LANES), F32)],
        compiler_params=_cparams(("parallel",)),
        name="ssm_sample",
    )(u, bb, cc, lre, lim, d, h0re, h0im)


def _merge_kernel(x1_ref, o_ref, z_ref, ga_ref, gb_ref, wup_ref, wglu_ref, wout_ref, gx_ref, wxq_ref,
                  x2_ref, qx_ref):
    ya = _dot(o_ref[...], wup_ref[...])
    glu = _dot(z_ref[...], wglu_ref[...])
    yb = glu[:, :D_MODEL] * jax.nn.sigmoid(glu[:, D_MODEL:])
    merged = ga_ref[...].astype(F32) * ya + gb_ref[...].astype(F32) * yb
    x2 = x1_ref[...] + _dot(merged.astype(BF16), wout_ref[...])
    x2_ref[...] = x2
    hx = _rms(x2, gx_ref[...]).astype(BF16)
    qx_ref[...] = (_dot(hx, wxq_ref[...]) * (X_HEAD_DIM ** -0.5)).astype(BF16)


def _merge(x1, o, z, ga, gb, wup, wglu, wout, gx, wxq, *, tm):
    t = x1.shape[0]
    row = lambda w: pl.BlockSpec((tm, w), lambda i: (i, 0))
    return pl.pallas_call(
        _merge_kernel,
        grid=(t // tm,),
        in_specs=[row(D_MODEL), row(ATTN_WIDTH), row(SSM_WIDTH), row(D_MODEL), row(D_MODEL),
                  _const_spec(wup.shape), _const_spec(wglu.shape), _const_spec(wout.shape), _const_spec(gx.shape),
                  _const_spec(wxq.shape)],
        out_specs=[row(D_MODEL), row(D_MODEL)],
        out_shape=[jax.ShapeDtypeStruct((t, D_MODEL), F32), jax.ShapeDtypeStruct((t, D_MODEL), BF16)],
        compiler_params=_cparams(("parallel",)),
        name="merge",
    )(x1, o, z, ga, gb, wup, wglu, wout, gx, wxq)


def _memkv_kernel(mem_ref, g_ref, wk_ref, wv_ref, mk_ref, mv_ref):
    mn = _rms(mem_ref[...], g_ref[...]).astype(BF16)
    mk_ref[...] = _dot(mn, wk_ref[...])
    mv_ref[...] = _dot(mn, wv_ref[...])


def _memkv(mem, g, wk, wv):
    sds = jax.ShapeDtypeStruct((mem.shape[0], D_MODEL), F32)
    return pl.pallas_call(_memkv_kernel, out_shape=[sds, sds], name="memkv",
                          compiler_params=pltpu.CompilerParams(vmem_limit_bytes=VMEM_LIMIT))(mem, g, wk, wv)


def _xattn_heads(q, mk, mv):
    outs = []
    for h in range(N_X_HEADS):
        sl = slice(h * X_HEAD_DIM, (h + 1) * X_HEAD_DIM)
        s = _dot_nt(q[:, sl], mk[:, sl])
        p = jnp.exp(s - jnp.max(s, axis=-1, keepdims=True))
        outs.append(_dot(p.astype(BF16), mv[:, sl]) / jnp.sum(p, axis=-1, keepdims=True))
    return jnp.concatenate(outs, axis=1)


def _xattn_prompt_kernel(q_ref, mk_ref, mv_ref, o_ref):
    o_ref[...] = _xattn_heads(q_ref[...], mk_ref[...].astype(BF16), mv_ref[...].astype(BF16)).astype(BF16)


def _xattn_prompt(qx, mk, mv, *, tm):
    t = qx.shape[0]
    row = pl.BlockSpec((tm, D_MODEL), lambda i: (i, 0))
    return pl.pallas_call(
        _xattn_prompt_kernel,
        grid=(t // tm,),
        in_specs=[row, _const_spec(mk.shape), _const_spec(mv.shape)],
        out_specs=row,
        out_shape=jax.ShapeDtypeStruct((t, D_MODEL), BF16),
        compiler_params=_cparams(("parallel",)),
        name="xattn_prompt",
    )(qx, mk, mv)


def _xattn_sample_kernel(q_ref, mk_ref, mv_ref, o_ref, *, nb, ts):
    qrows = ts * N_X_HEADS
    krows = N_MEM * N_X_HEADS
    row = lax.broadcasted_iota(jnp.int32, (qrows, krows), 0)
    col = lax.broadcasted_iota(jnp.int32, (qrows, krows), 1)
    same_head = (row & (N_X_HEADS - 1)) == (col & (N_X_HEADS - 1))
    for b in range(nb):
        kf = mk_ref[b].reshape(krows, X_HEAD_DIM).astype(BF16)
        vf = mv_ref[b].reshape(krows, X_HEAD_DIM).astype(BF16)
        s = jnp.where(same_head, _dot_nt(q_ref[b], kf), NEG_INF)
        p = jnp.exp(s - jnp.max(s, axis=-1, keepdims=True))
        o = _dot(p.astype(BF16), vf) / jnp.sum(p, axis=-1, keepdims=True)
        o_ref[b] = o.astype(BF16)


def _xattn_sample(qx, mk, mv, *, nb, ts):
    nseq = mk.shape[0]
    row = pl.BlockSpec((nb, ts * N_X_HEADS, X_HEAD_DIM), lambda i: (i, 0, 0))
    mem = pl.BlockSpec((nb, N_MEM, N_X_HEADS, X_HEAD_DIM), lambda i: (i, 0, 0, 0))
    return pl.pallas_call(
        functools.partial(_xattn_sample_kernel, nb=nb, ts=ts),
        grid=(nseq // nb,),
        in_specs=[row, mem, mem],
        out_specs=row,
        out_shape=jax.ShapeDtypeStruct((nseq, ts * N_X_HEADS, X_HEAD_DIM), BF16),
        compiler_params=_cparams(("parallel",)),
        name="xattn_sample",
    )(qx, mk, mv)


def kernel(x_prompt, x_sample, cache_win_k, cache_win_v, state_ssm_re, state_ssm_im, cache_mem_k, cache_mem_v, mem_prompt, g_ffn1, w_ffn1_in, w_ffn1_out, g_mix, w_in, attn_sinks, ssm_a_re, ssm_a_im, ssm_log_dt, ssm_b_re, ssm_b_im, ssm_c_re, ssm_c_im, ssm_d, w_attn_up, w_ssm_glu, w_out, g_xattn, g_mem, w_xq, w_xk, w_xv, w_xo, g_ffn2, w_ffn2_in, w_ffn2_out, g_final):
    assert x_prompt.shape[0] == 1 and g_ffn1.shape[0] == 1
    seq = x_prompt.shape[1]
    nseq, ts = x_sample.shape[0], x_sample.shape[1]
    past_len = seq
    l = 0
    bf = lambda w: w[l].astype(BF16)
    vec = lambda g: g[l].reshape(1, -1)
    w1i, w1o, w2i, w2o = bf(w_ffn1_in), bf(w_ffn1_out), bf(w_ffn2_in), bf(w_ffn2_out)
    win, wup, wglu, wout = bf(w_in), bf(w_attn_up), bf(w_ssm_glu), bf(w_out)
    wxq, wxk, wxv, wxo = bf(w_xq), bf(w_xk), bf(w_xv), bf(w_xo)
    gf = g_final.reshape(1, -1)
    sinks = attn_sinks[l]

    lam_re, lam_im, bbt_re, bbt_im = _ssm_prep(ssm_a_re[l], ssm_a_im[l], ssm_log_dt[l], ssm_b_re[l], ssm_b_im[l])
    bb = jnp.concatenate([_block_diag_halves(bbt_re), _block_diag_halves(bbt_im)], axis=-1).astype(BF16)
    ct_re = jnp.swapaxes(ssm_c_re[l], 1, 2)
    ct_im = jnp.swapaxes(ssm_c_im[l], 1, 2)
    cc = jnp.concatenate([_block_diag_halves(ct_re), -_block_diag_halves(ct_im)], axis=1).astype(BF16)
    d_skip = ssm_d[l].reshape(1, -1)
    lre16 = lam_re.reshape(2, SUBLANES, LANES)
    lim16 = lam_im.reshape(2, SUBLANES, LANES)
    lam_tm = jnp.stack([lre16[0], lim16[0], lre16[1], lim16[1]])
    lre_row = lam_re.reshape(1, SSM_CH)
    lim_row = lam_im.reshape(1, SSM_CH)

    mk_p, mv_p = _memkv(mem_prompt[0], vec(g_mem), wxk, wxv)

    def group(x, cos, sin, attn_fn, ssm_fn, xattn_fn, tm):
        x1 = _ffn(x, vec(g_ffn1), w1i, w1o, gf, tm=tm)
        q, k, v, u, ga, gb = _inproj(x1, vec(g_mix), win, cos, sin, tm=tm)
        o, extra_attn = attn_fn(q, k, v)
        z, extra_ssm = ssm_fn(u)
        x2, qx = _merge(x1, o, z, ga, gb, wup, wglu, wout, vec(g_xattn), wxq, tm=tm)
        ox = xattn_fn(qx)
        y = _ffn(x2, vec(g_ffn2), w2i, w2o, gf, tm=tm, pre=(ox, wxo), final=True)
        return y, extra_attn, extra_ssm

    cos_p, sin_p = _rope_tables(seq, 0.0, 1)

    def attn_p(q, k, v):
        return _attn_prompt(sinks, q, k, v), (k[-WINDOW:], v[-WINDOW:])

    def ssm_p(u):
        z, hout = _ssm_prompt(u, bb, cc, lam_tm, d_skip, jnp.zeros((4, SUBLANES, LANES), F32))
        return z, hout

    y_p, (nk_p, nv_p), hout_p = group(x_prompt[0], cos_p, sin_p, attn_p, ssm_p,
                                      lambda qx: _xattn_prompt(qx, mk_p, mv_p, tm=512), 512)
    hre_p = jnp.concatenate([hout_p[0], hout_p[2]], axis=0)
    him_p = jnp.concatenate([hout_p[1], hout_p[3]], axis=0)

    cos_s, sin_s = _rope_tables(ts, float(past_len), nseq)

    def attn_s(q, k, v):
        to_t = lambda c: jnp.transpose(c[l], (0, 2, 3, 1))
        o, nkt, nvt = _attn_sample(sinks, q, k, v, to_t(cache_win_k), to_t(cache_win_v), nb=8, ts=ts)
        return o, (jnp.transpose(nkt, (0, 3, 1, 2)), jnp.transpose(nvt, (0, 3, 1, 2)))

    def ssm_s(u):
        z, hre, him = _ssm_sample(u, bb, cc, lre_row, lim_row, d_skip, state_ssm_re[l].reshape(nseq, SSM_CH),
                                  state_ssm_im[l].reshape(nseq, SSM_CH), nb=64, ts=ts)
        return z, (hre, him)

    y_s, (nk_s, nv_s), (hre_s, him_s) = group(
        x_sample.reshape(nseq * ts, D_MODEL), cos_s, sin_s, attn_s, ssm_s,
        lambda qx: _xattn_sample(qx.reshape(nseq, ts * N_X_HEADS, X_HEAD_DIM), cache_mem_k[l], cache_mem_v[l],
                                 nb=4, ts=ts).reshape(nseq * ts, D_MODEL), 512)

    kvshape = (1, 1, WINDOW, N_KV_HEADS, HEAD_DIM)
    stshape = (1, 1, N_SSM_GROUPS, SSM_STATE)
    memshape = (1, 1, N_MEM, N_X_HEADS, X_HEAD_DIM)
    return (y_p.reshape(1, seq, D_MODEL), y_s.reshape(nseq, ts, D_MODEL),
            nk_p.reshape(kvshape), nv_p.reshape(kvshape), hre_p.reshape(stshape), him_p.reshape(stshape),
            mk_p.reshape(memshape), mv_p.reshape(memshape),
            nk_s.reshape(1, nseq, WINDOW, N_KV_HEADS, HEAD_DIM), nv_s.reshape(1, nseq, WINDOW, N_KV_HEADS, HEAD_DIM),
            hre_s.reshape(1, nseq, N_SSM_GROUPS, SSM_STATE), him_s.reshape(1, nseq, N_SSM_GROUPS, SSM_STATE))
```

```python
import functools
import math

import jax
import jax.numpy as jnp
from jax import lax
from jax.experimental import pallas as pl
from jax.experimental.pallas import tpu as pltpu

F32 = jnp.float32
BF16 = jnp.bfloat16

D_MODEL = 1024
N_Q_HEADS = 8
N_KV_HEADS = 2
HEAD_DIM = 64
Q_PER_KV = N_Q_HEADS // N_KV_HEADS
ATTN_WIDTH = N_Q_HEADS * HEAD_DIM
KV_WIDTH = N_KV_HEADS * HEAD_DIM
WINDOW = 128
ROPE_THETA = 10000.0
SSM_WIDTH = D_MODEL // 2
SSM_GROUP = 16
N_SSM_GROUPS = SSM_WIDTH // SSM_GROUP
SSM_STATE = 64
N_MEM = 256
N_X_HEADS = 4
X_HEAD_DIM = D_MODEL // N_X_HEADS
D_FF = 2816
RMS_EPS = 1e-6
NEG_INF = -1e30
IN_SPLITS = (ATTN_WIDTH, KV_WIDTH, KV_WIDTH, SSM_WIDTH, D_MODEL, D_MODEL)
IN_WIDTH = sum(IN_SPLITS)
OFF_Q, OFF_K, OFF_V, OFF_U, OFF_GA, OFF_GB = (0, 512, 640, 768, 1280, 2304)

LANES = 128
SUBLANES = 8
VMEM_LIMIT = 56 * 1024 * 1024

SSM_CH = N_SSM_GROUPS * SSM_STATE
HALF_CH = SSM_CH // 2
HALF_U = SSM_WIDTH // 2
BLK_PER_HALF = 2 * HALF_CH // LANES
TM_FFN = 512
TM_MIX = 1024
ATTN_QBLOCKS = 4
SSM_TB = 256
SSM_NB = 2
SSM_PITCH = SSM_TB + 4


def _cparams(sem):
    return pltpu.CompilerParams(dimension_semantics=sem, vmem_limit_bytes=VMEM_LIMIT)


def _const_spec(shape):
    nd = len(shape)
    return pl.BlockSpec(shape, lambda *_: (0,) * nd, pipeline_mode=pl.Buffered(1))


def _rms(x, g):
    return x * lax.rsqrt(jnp.mean(x * x, axis=-1, keepdims=True) + RMS_EPS) * g


def _dot(a, b):
    return jnp.dot(a, b, preferred_element_type=F32)


def _dot_nt(a, b):
    return lax.dot_general(a, b, (((1,), (1,)), ((), ())), preferred_element_type=F32)


def _ffn_kernel(*refs, has_pre, final):
    if has_pre:
        x_ref, ox_ref, wxo_ref, g_ref, wi_ref, wo_ref, gf_ref, o_ref = refs
        x = x_ref[...] + _dot(ox_ref[...], wxo_ref[...])
    else:
        x_ref, g_ref, wi_ref, wo_ref, gf_ref, o_ref = refs
        x = x_ref[...]
    h = _rms(x, g_ref[...]).astype(BF16)
    a = _dot(h, wi_ref[:, :D_FF])
    b = _dot(h, wi_ref[:, D_FF:])
    act = (a * jax.nn.sigmoid(a) * b).astype(BF16)
    y = x + 0.5 * _dot(act, wo_ref[...])
    if final:
        y = _rms(y, gf_ref[...])
    o_ref[...] = y


def _ffn(x, g, wi, wo, gf, *, tm, pre=None, final=False):
    t = x.shape[0]
    row = lambda w: pl.BlockSpec((tm, w), lambda i: (i, 0))
    in_specs = [row(D_MODEL)]
    args = [x]
    if pre is not None:
        ox, wxo = pre
        in_specs += [row(D_MODEL), _const_spec((D_MODEL, D_MODEL))]
        args += [ox, wxo]
    in_specs += [_const_spec((1, D_MODEL)), _const_spec((D_MODEL, 2 * D_FF)), _const_spec((D_FF, D_MODEL)),
                 _const_spec((1, D_MODEL))]
    args += [g, wi, wo, gf]
    return pl.pallas_call(
        functools.partial(_ffn_kernel, has_pre=pre is not None, final=final),
        grid=(t // tm,),
        in_specs=in_specs,
        out_specs=row(D_MODEL),
        out_shape=jax.ShapeDtypeStruct((t, D_MODEL), F32),
        compiler_params=_cparams(("parallel",)),
        name="tail" if pre is not None else "ffn",
    )(*args)


def _inproj_kernel(x_ref, g_ref, w_ref, ca_ref, sa_ref, cb_ref, sb_ref, q_ref, k_ref, v_ref, u_ref, ga_ref, gb_ref):
    h = _rms(x_ref[...], g_ref[...]).astype(BF16)
    ca, sa, cb, sb = ca_ref[0], sa_ref[0], cb_ref[...], sb_ref[...]
    cos = ca * cb - sa * sb
    sin = sa * cb + ca * sb
    lane = lax.broadcasted_iota(jnp.int32, cos.shape, 1)
    first_half = (lane & (HEAD_DIM - 1)) < (HEAD_DIM // 2)

    def rope(xc):
        rot = jnp.where(first_half, pltpu.roll(xc, LANES - HEAD_DIM // 2, 1), pltpu.roll(xc, HEAD_DIM // 2, 1))
        return xc * cos + rot * sin

    yq = _dot(h, w_ref[:, OFF_Q:OFF_K])
    for c in range(ATTN_WIDTH // LANES):
        q_ref[:, c * LANES:(c + 1) * LANES] = (rope(yq[:, c * LANES:(c + 1) * LANES]) * (HEAD_DIM ** -0.5)).astype(BF16)
    ykvu = _dot(h, w_ref[:, OFF_K:OFF_GA])
    k_ref[...] = rope(ykvu[:, :KV_WIDTH])
    v_ref[...] = ykvu[:, KV_WIDTH:2 * KV_WIDTH]
    u_ref[...] = ykvu[:, 2 * KV_WIDTH:]
    ga_ref[...] = jax.nn.sigmoid(_dot(h, w_ref[:, OFF_GA:OFF_GB])).astype(BF16)
    gb_ref[...] = jax.nn.sigmoid(_dot(h, w_ref[:, OFF_GB:])).astype(BF16)


def _inproj(x, g, w, rope_tabs, *, tm):
    t = x.shape[0]
    row = lambda wd: pl.BlockSpec((tm, wd), lambda i: (i, 0))
    step_tab = pl.BlockSpec((1, 1, LANES), lambda i: (i, 0, 0))
    widths = (ATTN_WIDTH, KV_WIDTH, KV_WIDTH, SSM_WIDTH, D_MODEL, D_MODEL)
    dtypes = (BF16, F32, F32, F32, BF16, BF16)
    return pl.pallas_call(
        _inproj_kernel,
        grid=(t // tm,),
        in_specs=[row(D_MODEL), _const_spec((1, D_MODEL)), _const_spec((D_MODEL, IN_WIDTH)), step_tab, step_tab,
                  _const_spec((tm, LANES)), _const_spec((tm, LANES))],
        out_specs=[row(wd) for wd in widths],
        out_shape=[jax.ShapeDtypeStruct((t, wd), dt) for wd, dt in zip(widths, dtypes)],
        compiler_params=_cparams(("parallel",)),
        name="inproj",
    )(x, g, w, *rope_tabs)


def _rope_tables(step_pos, row_pos):
    half = HEAD_DIM // 2
    inv = ROPE_THETA ** (-jnp.arange(half, dtype=F32) / half)
    sign = jnp.concatenate([-jnp.ones((half,), F32), jnp.ones((half,), F32)])

    def tabs(pos):
        ang = pos.astype(F32)[:, None] * inv[None, :]
        cos = jnp.tile(jnp.cos(ang), (1, LANES // half))
        sin = jnp.tile(jnp.tile(jnp.sin(ang), (1, 2)) * sign[None, :], (1, LANES // HEAD_DIM))
        return cos, sin

    ca, sa = tabs(step_pos)
    cb, sb = tabs(row_pos)
    return ca[:, None, :], sa[:, None, :], cb, sb


def _attn_prompt_kernel(sinks_ref, q_ref, kp_ref, kc_ref, vp_ref, vc_ref, o_ref):
    i = pl.program_id(0)
    r = lax.broadcasted_iota(jnp.int32, (WINDOW, 2 * WINDOW), 0)
    c = lax.broadcasted_iota(jnp.int32, (WINDOW, 2 * WINDOW), 1)
    band = (c >= r) & (c <= WINDOW + r)
    first_mask = band & ((c >= WINDOW) | (i > 0))
    kcat = jnp.concatenate([kp_ref[...], kc_ref[...]], axis=0)
    vcat = jnp.concatenate([vp_ref[...], vc_ref[...]], axis=0)
    kswap = pltpu.roll(kcat, HEAD_DIM, 1)
    vswap = pltpu.roll(vcat, HEAD_DIM, 1)
    lo = lax.broadcasted_iota(jnp.int32, kcat.shape, 1) < HEAD_DIM
    lo_q = lax.broadcasted_iota(jnp.int32, (WINDOW, LANES), 1) < HEAD_DIM
    ones = jnp.ones(kcat.shape, F32)
    for kv in range(N_KV_HEADS):
        k_own, k_other = (kcat, kswap) if kv == 0 else (kswap, kcat)
        v_own, v_other = (vcat, vswap) if kv == 0 else (vswap, vcat)
        k_half = (jnp.where(lo, k_own, 0.0).astype(BF16), jnp.where(lo, 0.0, k_other).astype(BF16))
        rhs_half = (jnp.concatenate([jnp.where(lo, v_own, 0.0), jnp.where(lo, ones, 0.0)], axis=1).astype(BF16),
                    jnp.concatenate([jnp.where(lo, 0.0, v_other), jnp.where(lo, 0.0, ones)], axis=1).astype(BF16))
        for j in range(ATTN_QBLOCKS):
            qrows = slice(j * WINDOW, (j + 1) * WINDOW)
            krows = slice(j * WINDOW, (j + 2) * WINDOW)
            mask = first_mask if j == 0 else band
            for c in range(Q_PER_KV // 2):
                tile = kv * (Q_PER_KV // 2) + c
                qt = q_ref[qrows, tile * LANES:(tile + 1) * LANES]
                acc = None
                sink_terms = []
                for side in range(2):
                    s = jnp.where(mask, _dot_nt(qt, k_half[side][krows]), NEG_INF)
                    sink = sinks_ref[2 * tile + side]
                    m = jnp.maximum(jnp.max(s, axis=-1, keepdims=True), sink)
                    pv = _dot(jnp.exp(s - m).astype(BF16), rhs_half[side][krows])
                    acc = pv if acc is None else acc + pv
                    sink_terms.append(jnp.exp(sink - m))
                denom = acc[:, LANES:] + jnp.where(lo_q, sink_terms[0], sink_terms[1])
                o_ref[qrows, tile * LANES:(tile + 1) * LANES] = (acc[:, :LANES] / denom).astype(BF16)


def _attn_prompt(sinks, q, k, v):
    t = q.shape[0]
    rows = ATTN_QBLOCKS * WINDOW
    cur = lambda w: pl.BlockSpec((rows, w), lambda i: (i, 0))
    prev = lambda w: pl.BlockSpec((WINDOW, w), lambda i: (jnp.maximum(i * ATTN_QBLOCKS - 1, 0), 0))
    return pl.pallas_call(
        _attn_prompt_kernel,
        grid=(t // rows,),
        in_specs=[pl.BlockSpec(memory_space=pltpu.SMEM), cur(ATTN_WIDTH), prev(KV_WIDTH), cur(KV_WIDTH),
                  prev(KV_WIDTH), cur(KV_WIDTH)],
        out_specs=cur(ATTN_WIDTH),
        out_shape=jax.ShapeDtypeStruct((t, ATTN_WIDTH), BF16),
        compiler_params=_cparams(("parallel",)),
        name="attn_prompt",
    )(sinks, q, k, k, v, v)


def _attn_sample_kernel(sinks_ref, q_ref, k_ref, v_ref, ckt_ref, cvt_ref, o_ref, nkt_ref, nvt_ref, *, nb, ts):
    rows = nb * ts
    q = q_ref[...]
    pad = jnp.zeros((LANES - rows, KV_WIDTH), F32)
    knt = jnp.concatenate([k_ref[...], pad], axis=0).T
    vnt = jnp.concatenate([v_ref[...], pad], axis=0).T
    lane = lax.broadcasted_iota(jnp.int32, (HEAD_DIM, WINDOW), 1)
    is_new = lane >= WINDOW - ts
    for b in range(nb):
        knt_b = pltpu.roll(knt, (WINDOW - ts - b * ts) % LANES, 1)
        vnt_b = pltpu.roll(vnt, (WINDOW - ts - b * ts) % LANES, 1)
        for kv in range(N_KV_HEADS):
            ksl = slice(kv * HEAD_DIM, (kv + 1) * HEAD_DIM)
            nkt_ref[b, kv] = jnp.where(is_new, knt_b[ksl], pltpu.roll(ckt_ref[b, kv], WINDOW - ts, 1))
            nvt_ref[b, kv] = jnp.where(is_new, vnt_b[ksl], pltpu.roll(cvt_ref[b, kv], WINDOW - ts, 1))

    knt = knt[:, :rows].astype(BF16)
    vnt = vnt[:, :rows].astype(BF16)
    grows = Q_PER_KV * rows
    row1 = lax.broadcasted_iota(jnp.int32, (grows, nb * WINDOW), 0)
    col1 = lax.broadcasted_iota(jnp.int32, (grows, nb * WINDOW), 1)
    rseq1 = (row1 % rows) // ts
    mask_ctx = (rseq1 == col1 // WINDOW) & (col1 % WINDOW >= row1 % ts)
    row2 = lax.broadcasted_iota(jnp.int32, (grows, rows), 0)
    col2 = lax.broadcasted_iota(jnp.int32, (grows, rows), 1)
    mask_new = ((row2 % rows) // ts == col2 // ts) & (col2 % ts <= row2 % ts)
    ghead = lax.broadcasted_iota(jnp.int32, (grows, 1), 0) // rows
    for kv in range(N_KV_HEADS):
        ksl = slice(kv * HEAD_DIM, (kv + 1) * HEAD_DIM)
        qg = jnp.concatenate([q[:, (kv * Q_PER_KV + g) * HEAD_DIM:(kv * Q_PER_KV + g + 1) * HEAD_DIM]
                              for g in range(Q_PER_KV)], axis=0)
        kctx = jnp.concatenate([ckt_ref[b, kv] for b in range(nb)], axis=1).astype(BF16)
        vctx = jnp.concatenate([cvt_ref[b, kv] for b in range(nb)], axis=1).astype(BF16)
        s1 = jnp.where(mask_ctx, _dot(qg, kctx), NEG_INF)
        s2 = jnp.where(mask_new, _dot(qg, knt[ksl]), NEG_INF)
        sink = jnp.zeros((grows, 1), F32)
        for g in range(Q_PER_KV):
            sink = jnp.where(ghead == g, sinks_ref[kv * Q_PER_KV + g], sink)
        m = jnp.maximum(jnp.maximum(jnp.max(s1, axis=-1, keepdims=True), jnp.max(s2, axis=-1, keepdims=True)), sink)
        p1 = jnp.exp(s1 - m)
        p2 = jnp.exp(s2 - m)
        denom = jnp.sum(p1, axis=-1, keepdims=True) + jnp.sum(p2, axis=-1, keepdims=True) + jnp.exp(sink - m)
        o = (_dot_nt(p1.astype(BF16), vctx) + _dot_nt(p2.astype(BF16), vnt[ksl])) / denom
        for g in range(Q_PER_KV):
            h = kv * Q_PER_KV + g
            o_ref[:, h * HEAD_DIM:(h + 1) * HEAD_DIM] = o[g * rows:(g + 1) * rows].astype(BF16)


def _attn_sample(sinks, q, k, v, ckt, cvt, *, nb, ts):
    nseq = ckt.shape[0]
    rows = nb * ts
    assert rows <= LANES
    row = lambda w: pl.BlockSpec((rows, w), lambda i: (i, 0))
    win = pl.BlockSpec((nb, N_KV_HEADS, HEAD_DIM, WINDOW), lambda i: (i, 0, 0, 0))
    win_shape = jax.ShapeDtypeStruct((nseq, N_KV_HEADS, HEAD_DIM, WINDOW), F32)
    return pl.pallas_call(
        functools.partial(_attn_sample_kernel, nb=nb, ts=ts),
        grid=(nseq // nb,),
        in_specs=[pl.BlockSpec(memory_space=pltpu.SMEM), row(ATTN_WIDTH), row(KV_WIDTH), row(KV_WIDTH), win, win],
        out_specs=[row(ATTN_WIDTH), win, win],
        out_shape=[jax.ShapeDtypeStruct((nseq * ts, ATTN_WIDTH), BF16), win_shape, win_shape],
        compiler_params=_cparams(("parallel",)),
        name="attn_sample",
    )(sinks, q, k, v, ckt, cvt)


def _ssm_prep_kernel(are_ref, aim_ref, ldt_ref, bre_ref, bim_ref, lre_ref, lim_ref, bbre_ref, bbim_ref):
    a_re = are_ref[...]
    a_im = aim_ref[...]
    dt = jnp.exp(ldt_ref[...])
    mag = jnp.exp(a_re * dt)
    lb_re = mag * jnp.cos(a_im * dt)
    lb_im = mag * jnp.sin(a_im * dt)
    den = a_re * a_re + a_im * a_im
    nr = lb_re - 1.0
    ni = lb_im
    k_re = (nr * a_re + ni * a_im) / den
    k_im = (ni * a_re - nr * a_im) / den
    lre_ref[...] = lb_re
    lim_ref[...] = lb_im
    b_re = bre_ref[...]
    b_im = bim_ref[...]
    bbre_ref[...] = k_re[:, None, :] * b_re - k_im[:, None, :] * b_im
    bbim_ref[...] = k_re[:, None, :] * b_im + k_im[:, None, :] * b_re


def _ssm_prep(a_re, a_im, log_dt, b_re, b_im):
    g, n, gs = b_re.shape
    sds = jax.ShapeDtypeStruct
    return pl.pallas_call(
        _ssm_prep_kernel,
        out_shape=[sds((g, n), F32), sds((g, n), F32), sds((g, gs, n), F32), sds((g, gs, n), F32)],
        name="ssm_prep",
    )(a_re, a_im, log_dt.reshape(g, 1), jnp.swapaxes(b_re, 1, 2), jnp.swapaxes(b_im, 1, 2))


def _block_diag_halves(m):
    g, a, b = m.shape
    gh = g // 2
    eye = jnp.eye(gh, dtype=m.dtype)
    return jnp.einsum('hgab,gk->hgakb', m.reshape(2, gh, a, b), eye).reshape(2, gh * a, gh * b)


def _ssm_y(u, scan_block, cc_ref, d_ref):
    ys = []
    for half in range(2):
        hc = jnp.concatenate([scan_block(half, jj) for jj in range(BLK_PER_HALF)], axis=1).astype(BF16)
        ys.append(_dot(hc, cc_ref[half]))
    y = jnp.concatenate(ys, axis=1) + d_ref[...] * u
    return jax.nn.gelu(y).astype(BF16)


def _ssm_prompt_kernel(u_ref, bb_ref, cc_ref, lam_ref, d_ref, h0_ref, z_ref, hout_ref,
                       x0_ref, x1_ref, s0_ref, s1_ref, hst_ref):
    tb, pitch = SSM_TB, SSM_PITCH
    x_bufs = (x0_ref, x1_ref)
    s_bufs = (s0_ref, s1_ref)

    @pl.when(pl.program_id(0) == 0)
    def _():
        hst_ref[...] = h0_ref[...]

    def b_proj(k):
        ub = u_ref[k * tb:(k + 1) * tb, :].astype(BF16)
        for half in range(2):
            x = _dot(ub[:, half * HALF_U:(half + 1) * HALF_U], bb_ref[half])
            for jj in range(BLK_PER_HALF):
                x_bufs[k % 2][pl.ds((half * BLK_PER_HALF + jj) * pitch, tb), :] = x[:, jj * LANES:(jj + 1) * LANES]

    lam = [lam_ref[g] for g in range(4)]
    carry = [hst_ref[g] for g in range(4)]

    def scan(k):
        for t in range(tb):
            for half in range(2):
                hr, hi = carry[2 * half], carry[2 * half + 1]
                lr, li = lam[2 * half], lam[2 * half + 1]
                rows_re = pl.ds((half * BLK_PER_HALF) * pitch + t, SUBLANES, stride=pitch)
                rows_im = pl.ds((half * BLK_PER_HALF + SUBLANES) * pitch + t, SUBLANES, stride=pitch)
                nr = lr * hr - li * hi + x_bufs[k % 2][rows_re, :]
                ni = lr * hi + li * hr + x_bufs[k % 2][rows_im, :]
                s_bufs[k % 2][rows_re, :] = nr
                s_bufs[k % 2][rows_im, :] = ni
                carry[2 * half], carry[2 * half + 1] = nr, ni

    def c_proj(k):
        z_ref[k * tb:(k + 1) * tb, :] = _ssm_y(
            u_ref[k * tb:(k + 1) * tb, :],
            lambda half, jj: s_bufs[k % 2][pl.ds((half * BLK_PER_HALF + jj) * pitch, tb), :], cc_ref, d_ref)

    b_proj(0)
    for k in range(SSM_NB):
        if k + 1 < SSM_NB:
            b_proj(k + 1)
        scan(k)
        c_proj(k)
    for g in range(4):
        hst_ref[g] = carry[g]
        hout_ref[g] = carry[g]


def _ssm_prompt(u, bb, cc, lam, d, h0):
    t = u.shape[0]
    rows = SSM_NB * SSM_TB
    return pl.pallas_call(
        _ssm_prompt_kernel,
        grid=(t // rows,),
        in_specs=[pl.BlockSpec((rows, SSM_WIDTH), lambda i: (i, 0)), _const_spec(bb.shape), _const_spec(cc.shape),
                  _const_spec(lam.shape), _const_spec(d.shape), _const_spec(h0.shape)],
        out_specs=[pl.BlockSpec((rows, SSM_WIDTH), lambda i: (i, 0)),
                   pl.BlockSpec((4, SUBLANES, LANES), lambda i: (0, 0, 0))],
        out_shape=[jax.ShapeDtypeStruct((t, SSM_WIDTH), BF16), jax.ShapeDtypeStruct((4, SUBLANES, LANES), F32)],
        scratch_shapes=[pltpu.VMEM((2 * BLK_PER_HALF * SSM_PITCH, LANES), F32)] * 4
                       + [pltpu.VMEM((4, SUBLANES, LANES), F32)],
        compiler_params=_cparams(("arbitrary",)),
        name="ssm_prompt",
    )(u, bb, cc, lam, d, h0)


def _ssm_sample_kernel(u_ref, bb_ref, cc_ref, lre_ref, lim_ref, d_ref, h0re_ref, h0im_ref,
                       z_ref, hre_ref, him_ref, xs_ref, *, nb, ts):
    u = u_ref[...]
    ub = u.astype(BF16)
    for half in range(2):
        x = _dot(ub[:, half * HALF_U:(half + 1) * HALF_U], bb_ref[half])
        for jj in range(BLK_PER_HALF):
            xs_ref[half * BLK_PER_HALF + jj] = x[:, jj * LANES:(jj + 1) * LANES]
    for cb in range(SSM_CH // LANES):
        half, jj = divmod(cb, SUBLANES)
        j_re = half * BLK_PER_HALF + jj
        j_im = j_re + SUBLANES
        csl = slice(cb * LANES, (cb + 1) * LANES)
        lr = lre_ref[:, csl]
        li = lim_ref[:, csl]
        hr = h0re_ref[:, csl]
        hi = h0im_ref[:, csl]
        for t in range(ts):
            rows = pl.ds(t, nb, stride=ts)
            nr = lr * hr - li * hi + xs_ref[j_re, rows, :]
            ni = lr * hi + li * hr + xs_ref[j_im, rows, :]
            xs_ref[j_re, rows, :] = nr
            xs_ref[j_im, rows, :] = ni
            hr, hi = nr, ni
        hre_ref[:, csl] = hr
        him_ref[:, csl] = hi
    z_ref[...] = _ssm_y(u, lambda half, jj: xs_ref[half * BLK_PER_HALF + jj], cc_ref, d_ref)


def _ssm_sample(u, bb, cc, lre, lim, d, h0re, h0im, *, nb, ts):
    nseq = h0re.shape[0]
    rows = nb * ts
    st = pl.BlockSpec((nb, SSM_CH), lambda i: (i, 0))
    return pl.pallas_call(
        functools.partial(_ssm_sample_kernel, nb=nb, ts=ts),
        grid=(nseq // nb,),
        in_specs=[pl.BlockSpec((rows, SSM_WIDTH), lambda i: (i, 0)), _const_spec(bb.shape), _const_spec(cc.shape),
                  _const_spec(lre.shape), _const_spec(lim.shape), _const_spec(d.shape), st, st],
        out_specs=[pl.BlockSpec((rows, SSM_WIDTH), lambda i: (i, 0)), st, st],
        out_shape=[jax.ShapeDtypeStruct((nseq * ts, SSM_WIDTH), BF16),
                   jax.ShapeDtypeStruct((nseq, SSM_CH), F32), jax.ShapeDtypeStruct((nseq, SSM_CH), F32)],
        scratch_shapes=[pltpu.VMEM((2 * BLK_PER_HALF, rows, LANES), F32)],
        compiler_params=_cparams(("parallel",)),
        name="ssm_sample",
    )(u, bb, cc, lre, lim, d, h0re, h0im)


def _xattn_heads(q, mk, mv):
    outs = []
    for h in range(N_X_HEADS):
        sl = slice(h * X_HEAD_DIM, (h + 1) * X_HEAD_DIM)
        s = _dot_nt(q[:, sl], mk[:, sl])
        p = jnp.exp(s - jnp.max(s, axis=-1, keepdims=True))
        outs.append(_dot(p.astype(BF16), mv[:, sl]) / jnp.sum(p, axis=-1, keepdims=True))
    return jnp.concatenate(outs, axis=1)


def _merge_kernel(*refs, shared_mem):
    if shared_mem:
        (x1_ref, o_ref, z_ref, ga_ref, gb_ref, wup_ref, wglu_ref, wout_ref, gx_ref, wxq_ref, mk_ref, mv_ref,
         x2_ref, out_ref) = refs
    else:
        x1_ref, o_ref, z_ref, ga_ref, gb_ref, wup_ref, wglu_ref, wout_ref, gx_ref, wxq_ref, x2_ref, out_ref = refs
    ya = _dot(o_ref[...], wup_ref[...])
    glu = _dot(z_ref[...], wglu_ref[...])
    yb = glu[:, :D_MODEL] * jax.nn.sigmoid(glu[:, D_MODEL:])
    merged = ga_ref[...].astype(F32) * ya + gb_ref[...].astype(F32) * yb
    x2 = x1_ref[...] + _dot(merged.astype(BF16), wout_ref[...])
    x2_ref[...] = x2
    hx = _rms(x2, gx_ref[...]).astype(BF16)
    qx = (_dot(hx, wxq_ref[...]) * (X_HEAD_DIM ** -0.5)).astype(BF16)
    if shared_mem:
        out_ref[...] = _xattn_heads(qx, mk_ref[...].astype(BF16), mv_ref[...].astype(BF16)).astype(BF16)
    else:
        out_ref[...] = qx


def _merge(x1, o, z, ga, gb, wup, wglu, wout, gx, wxq, *, tm, mem=None):
    t = x1.shape[0]
    row = lambda w: pl.BlockSpec((tm, w), lambda i: (i, 0))
    consts = [wup, wglu, wout, gx, wxq] + (list(mem) if mem is not None else [])
    return pl.pallas_call(
        functools.partial(_merge_kernel, shared_mem=mem is not None),
        grid=(t // tm,),
        in_specs=[row(D_MODEL), row(ATTN_WIDTH), row(SSM_WIDTH), row(D_MODEL), row(D_MODEL)]
                 + [_const_spec(c.shape) for c in consts],
        out_specs=[row(D_MODEL), row(D_MODEL)],
        out_shape=[jax.ShapeDtypeStruct((t, D_MODEL), F32), jax.ShapeDtypeStruct((t, D_MODEL), BF16)],
        compiler_params=_cparams(("parallel",)),
        name="merge",
    )(x1, o, z, ga, gb, *consts)


def _memkv_kernel(mem_ref, g_ref, wk_ref, wv_ref, mk_ref, mv_ref):
    mn = _rms(mem_ref[...], g_ref[...]).astype(BF16)
    mk_ref[...] = _dot(mn, wk_ref[...])
    mv_ref[...] = _dot(mn, wv_ref[...])


def _memkv(mem, g, wk, wv):
    sds = jax.ShapeDtypeStruct((mem.shape[0], D_MODEL), F32)
    return pl.pallas_call(_memkv_kernel, out_shape=[sds, sds], name="memkv",
                          compiler_params=pltpu.CompilerParams(vmem_limit_bytes=VMEM_LIMIT))(mem, g, wk, wv)


def _xattn_sample_kernel(q_ref, mk_ref, mv_ref, o_ref, *, nb, ts):
    qrows = ts * N_X_HEADS
    krows = N_MEM * N_X_HEADS
    row = lax.broadcasted_iota(jnp.int32, (qrows, krows), 0)
    col = lax.broadcasted_iota(jnp.int32, (qrows, krows), 1)
    same_head = (row & (N_X_HEADS - 1)) == (col & (N_X_HEADS - 1))
    for b in range(nb):
        kf = mk_ref[b].reshape(krows, X_HEAD_DIM).astype(BF16)
        vf = mv_ref[b].reshape(krows, X_HEAD_DIM).astype(BF16)
        s = jnp.where(same_head, _dot_nt(q_ref[b], kf), NEG_INF)
        p = jnp.exp(s - jnp.max(s, axis=-1, keepdims=True))
        o = _dot(p.astype(BF16), vf) / jnp.sum(p, axis=-1, keepdims=True)
        o_ref[b] = o.astype(BF16)


def _xattn_sample(qx, mk, mv, *, nb, ts):
    nseq = mk.shape[0]
    row = pl.BlockSpec((nb, ts * N_X_HEADS, X_HEAD_DIM), lambda i: (i, 0, 0))
    mem = pl.BlockSpec((nb, N_MEM, N_X_HEADS, X_HEAD_DIM), lambda i: (i, 0, 0, 0))
    return pl.pallas_call(
        functools.partial(_xattn_sample_kernel, nb=nb, ts=ts),
        grid=(nseq // nb,),
        in_specs=[row, mem, mem],
        out_specs=row,
        out_shape=jax.ShapeDtypeStruct((nseq, ts * N_X_HEADS, X_HEAD_DIM), BF16),
        compiler_params=_cparams(("parallel",)),
        name="xattn_sample",
    )(qx, mk, mv)


def kernel(x_prompt, x_sample, cache_win_k, cache_win_v, state_ssm_re, state_ssm_im, cache_mem_k, cache_mem_v, mem_prompt, g_ffn1, w_ffn1_in, w_ffn1_out, g_mix, w_in, attn_sinks, ssm_a_re, ssm_a_im, ssm_log_dt, ssm_b_re, ssm_b_im, ssm_c_re, ssm_c_im, ssm_d, w_attn_up, w_ssm_glu, w_out, g_xattn, g_mem, w_xq, w_xk, w_xv, w_xo, g_ffn2, w_ffn2_in, w_ffn2_out, g_final):
    assert x_prompt.shape[0] == 1 and g_ffn1.shape[0] == 1
    seq = x_prompt.shape[1]
    nseq, ts = x_sample.shape[0], x_sample.shape[1]
    past_len = seq
    l = 0
    bf = lambda w: w[l].astype(BF16)
    vec = lambda g: g[l].reshape(1, -1)
    w1i, w1o, w2i, w2o = bf(w_ffn1_in), bf(w_ffn1_out), bf(w_ffn2_in), bf(w_ffn2_out)
    win, wup, wglu, wout = bf(w_in), bf(w_attn_up), bf(w_ssm_glu), bf(w_out)
    wxq, wxk, wxv, wxo = bf(w_xq), bf(w_xk), bf(w_xv), bf(w_xo)
    gf = g_final.reshape(1, -1)
    sinks = attn_sinks[l]

    lam_re, lam_im, bbt_re, bbt_im = _ssm_prep(ssm_a_re[l], ssm_a_im[l], ssm_log_dt[l], ssm_b_re[l], ssm_b_im[l])
    bb = jnp.concatenate([_block_diag_halves(bbt_re), _block_diag_halves(bbt_im)], axis=-1).astype(BF16)
    ct_re = jnp.swapaxes(ssm_c_re[l], 1, 2)
    ct_im = jnp.swapaxes(ssm_c_im[l], 1, 2)
    cc = jnp.concatenate([_block_diag_halves(ct_re), -_block_diag_halves(ct_im)], axis=1).astype(BF16)
    d_skip = ssm_d[l].reshape(1, -1)
    lre16 = lam_re.reshape(2, SUBLANES, LANES)
    lim16 = lam_im.reshape(2, SUBLANES, LANES)
    lam_tm = jnp.stack([lre16[0], lim16[0], lre16[1], lim16[1]])
    lre_row = lam_re.reshape(1, SSM_CH)
    lim_row = lam_im.reshape(1, SSM_CH)

    mk_p, mv_p = _memkv(mem_prompt[0], vec(g_mem), wxk, wxv)

    def group(x, rope_tabs, attn_fn, ssm_fn, shared_mem, xattn_fn):
        x1 = _ffn(x, vec(g_ffn1), w1i, w1o, gf, tm=TM_FFN)
        q, k, v, u, ga, gb = _inproj(x1, vec(g_mix), win, rope_tabs, tm=TM_MIX)
        o, extra_attn = attn_fn(q, k, v)
        z, extra_ssm = ssm_fn(u)
        x2, ox = _merge(x1, o, z, ga, gb, wup, wglu, wout, vec(g_xattn), wxq, tm=TM_MIX, mem=shared_mem)
        if shared_mem is None:
            ox = xattn_fn(ox)
        y = _ffn(x2, vec(g_ffn2), w2i, w2o, gf, tm=TM_FFN, pre=(ox, wxo), final=True)
        return y, extra_attn, extra_ssm

    tm = TM_MIX
    assert tm % ts == 0
    rope_p = _rope_tables(jnp.arange(seq // tm) * tm, jnp.arange(tm))

    def attn_p(q, k, v):
        return _attn_prompt(sinks, q, k, v), (k[-WINDOW:], v[-WINDOW:])

    def ssm_p(u):
        z, hout = _ssm_prompt(u, bb, cc, lam_tm, d_skip, jnp.zeros((4, SUBLANES, LANES), F32))
        return z, hout

    y_p, (nk_p, nv_p), hout_p = group(x_prompt[0], rope_p, attn_p, ssm_p, (mk_p, mv_p), None)
    hre_p = jnp.concatenate([hout_p[0], hout_p[2]], axis=0)
    him_p = jnp.concatenate([hout_p[1], hout_p[3]], axis=0)

    rope_s = _rope_tables(jnp.full((nseq * ts // tm,), past_len), jnp.arange(tm) % ts)

    def attn_s(q, k, v):
        to_t = lambda c: jnp.transpose(c[l], (0, 2, 3, 1))
        o, nkt, nvt = _attn_sample(sinks, q, k, v, to_t(cache_win_k), to_t(cache_win_v), nb=8, ts=ts)
        return o, (jnp.transpose(nkt, (0, 3, 1, 2)), jnp.transpose(nvt, (0, 3, 1, 2)))

    def ssm_s(u):
        z, hre, him = _ssm_sample(u, bb, cc, lre_row, lim_row, d_skip, state_ssm_re[l].reshape(nseq, SSM_CH),
                                  state_ssm_im[l].reshape(nseq, SSM_CH), nb=64, ts=ts)
        return z, (hre, him)

    y_s, (nk_s, nv_s), (hre_s, him_s) = group(
        x_sample.reshape(nseq * ts, D_MODEL), rope_s, attn_s, ssm_s, None,
        lambda qx: _xattn_sample(qx.reshape(nseq, ts * N_X_HEADS, X_HEAD_DIM), cache_mem_k[l], cache_mem_v[l],
                                 nb=4, ts=ts).reshape(nseq * ts, D_MODEL))

    kvshape = (1, 1, WINDOW, N_KV_HEADS, HEAD_DIM)
    stshape = (1, 1, N_SSM_GROUPS, SSM_STATE)
    memshape = (1, 1, N_MEM, N_X_HEADS, X_HEAD_DIM)
    return (y_p.reshape(1, seq, D_MODEL), y_s.reshape(nseq, ts, D_MODEL),
            nk_p.reshape(kvshape), nv_p.reshape(kvshape), hre_p.reshape(stshape), him_p.reshape(stshape),
            mk_p.reshape(memshape), mv_p.reshape(memshape),
            nk_s.reshape(1, nseq, WINDOW, N_KV_HEADS, HEAD_DIM), nv_s.reshape(1, nseq, WINDOW, N_KV_HEADS, HEAD_DIM),
            hre_s.reshape(1, nseq, N_SSM_GROUPS, SSM_STATE), him_s.reshape(1, nseq, N_SSM_GROUPS, SSM_STATE))
```

```python
import functools
import math

import jax
import jax.numpy as jnp
from jax import lax
from jax.experimental import pallas as pl
from jax.experimental.pallas import tpu as pltpu

F32 = jnp.float32
BF16 = jnp.bfloat16

D_MODEL = 1024
N_Q_HEADS = 8
N_KV_HEADS = 2
HEAD_DIM = 64
Q_PER_KV = N_Q_HEADS // N_KV_HEADS
ATTN_WIDTH = N_Q_HEADS * HEAD_DIM
KV_WIDTH = N_KV_HEADS * HEAD_DIM
WINDOW = 128
ROPE_THETA = 10000.0
SSM_WIDTH = D_MODEL // 2
SSM_GROUP = 16
N_SSM_GROUPS = SSM_WIDTH // SSM_GROUP
SSM_STATE = 64
N_MEM = 256
N_X_HEADS = 4
X_HEAD_DIM = D_MODEL // N_X_HEADS
D_FF = 2816
RMS_EPS = 1e-6
NEG_INF = -1e30
IN_SPLITS = (ATTN_WIDTH, KV_WIDTH, KV_WIDTH, SSM_WIDTH, D_MODEL, D_MODEL)
IN_WIDTH = sum(IN_SPLITS)
OFF_Q, OFF_K, OFF_V, OFF_U, OFF_GA, OFF_GB = (0, 512, 640, 768, 1280, 2304)

LANES = 128
SUBLANES = 8
VMEM_LIMIT = 56 * 1024 * 1024

SSM_CH = N_SSM_GROUPS * SSM_STATE
HALF_CH = SSM_CH // 2
HALF_U = SSM_WIDTH // 2
BLK_PER_HALF = 2 * HALF_CH // LANES
TM_FFN = 512
TM_MIX = 1024
ATTN_QBLOCKS = 8
SSM_TB = 256
SSM_NB = 2
SSM_PITCH = SSM_TB + 4


def _cparams(sem):
    return pltpu.CompilerParams(dimension_semantics=sem, vmem_limit_bytes=VMEM_LIMIT)


def _const_spec(shape):
    nd = len(shape)
    return pl.BlockSpec(shape, lambda *_: (0,) * nd, pipeline_mode=pl.Buffered(1))


def _rms(x, g):
    return x * lax.rsqrt(jnp.mean(x * x, axis=-1, keepdims=True) + RMS_EPS) * g


def _dot(a, b):
    return jnp.dot(a, b, preferred_element_type=F32)


def _dot_nt(a, b):
    return lax.dot_general(a, b, (((1,), (1,)), ((), ())), preferred_element_type=F32)


def _ffn_kernel(*refs, has_pre, final):
    if has_pre:
        x_ref, ox_ref, wxo_ref, g_ref, wi_ref, wo_ref, gf_ref, o_ref = refs
        x = x_ref[...] + _dot(ox_ref[...], wxo_ref[...])
    else:
        x_ref, g_ref, wi_ref, wo_ref, gf_ref, o_ref = refs
        x = x_ref[...]
    h = _rms(x, g_ref[...]).astype(BF16)
    a = _dot(h, wi_ref[:, :D_FF])
    b = _dot(h, wi_ref[:, D_FF:])
    act = (a * jax.nn.sigmoid(a) * b).astype(BF16)
    y = x + 0.5 * _dot(act, wo_ref[...])
    if final:
        y = _rms(y, gf_ref[...])
    o_ref[...] = y


def _ffn(x, g, wi, wo, gf, *, tm, pre=None, final=False):
    t = x.shape[0]
    row = lambda w: pl.BlockSpec((tm, w), lambda i: (i, 0))
    in_specs = [row(D_MODEL)]
    args = [x]
    if pre is not None:
        ox, wxo = pre
        in_specs += [row(D_MODEL), _const_spec((D_MODEL, D_MODEL))]
        args += [ox, wxo]
    in_specs += [_const_spec((1, D_MODEL)), _const_spec((D_MODEL, 2 * D_FF)), _const_spec((D_FF, D_MODEL)),
                 _const_spec((1, D_MODEL))]
    args += [g, wi, wo, gf]
    return pl.pallas_call(
        functools.partial(_ffn_kernel, has_pre=pre is not None, final=final),
        grid=(t // tm,),
        in_specs=in_specs,
        out_specs=row(D_MODEL),
        out_shape=jax.ShapeDtypeStruct((t, D_MODEL), F32),
        compiler_params=_cparams(("parallel",)),
        name="tail" if pre is not None else "ffn",
    )(*args)


def _inproj_kernel(x_ref, g_ref, w_ref, ca_ref, sa_ref, cb_ref, sb_ref, q_ref, k_ref, v_ref, u_ref, ga_ref, gb_ref):
    h = _rms(x_ref[...], g_ref[...]).astype(BF16)
    ca, sa, cb, sb = ca_ref[0], sa_ref[0], cb_ref[...], sb_ref[...]
    cos = ca * cb - sa * sb
    sin = sa * cb + ca * sb
    lane = lax.broadcasted_iota(jnp.int32, cos.shape, 1)
    first_half = (lane & (HEAD_DIM - 1)) < (HEAD_DIM // 2)

    def rope(xc):
        rot = jnp.where(first_half, pltpu.roll(xc, LANES - HEAD_DIM // 2, 1), pltpu.roll(xc, HEAD_DIM // 2, 1))
        return xc * cos + rot * sin

    yq = _dot(h, w_ref[:, OFF_Q:OFF_K])
    for c in range(ATTN_WIDTH // LANES):
        q_ref[:, c * LANES:(c + 1) * LANES] = (rope(yq[:, c * LANES:(c + 1) * LANES]) * (HEAD_DIM ** -0.5)).astype(BF16)
    ykvu = _dot(h, w_ref[:, OFF_K:OFF_GA])
    k_ref[...] = rope(ykvu[:, :KV_WIDTH])
    v_ref[...] = ykvu[:, KV_WIDTH:2 * KV_WIDTH]
    u_ref[...] = ykvu[:, 2 * KV_WIDTH:]
    ga_ref[...] = jax.nn.sigmoid(_dot(h, w_ref[:, OFF_GA:OFF_GB])).astype(BF16)
    gb_ref[...] = jax.nn.sigmoid(_dot(h, w_ref[:, OFF_GB:])).astype(BF16)


def _inproj(x, g, w, rope_tabs, *, tm):
    t = x.shape[0]
    row = lambda wd: pl.BlockSpec((tm, wd), lambda i: (i, 0))
    step_tab = pl.BlockSpec((1, 1, LANES), lambda i: (i, 0, 0))
    widths = (ATTN_WIDTH, KV_WIDTH, KV_WIDTH, SSM_WIDTH, D_MODEL, D_MODEL)
    dtypes = (BF16, F32, F32, F32, BF16, BF16)
    return pl.pallas_call(
        _inproj_kernel,
        grid=(t // tm,),
        in_specs=[row(D_MODEL), _const_spec((1, D_MODEL)), _const_spec((D_MODEL, IN_WIDTH)), step_tab, step_tab,
                  _const_spec((tm, LANES)), _const_spec((tm, LANES))],
        out_specs=[row(wd) for wd in widths],
        out_shape=[jax.ShapeDtypeStruct((t, wd), dt) for wd, dt in zip(widths, dtypes)],
        compiler_params=_cparams(("parallel",)),
        name="inproj",
    )(x, g, w, *rope_tabs)


def _rope_tables(step_pos, row_pos):
    half = HEAD_DIM // 2
    inv = ROPE_THETA ** (-jnp.arange(half, dtype=F32) / half)
    sign = jnp.concatenate([-jnp.ones((half,), F32), jnp.ones((half,), F32)])

    def tabs(pos):
        ang = pos.astype(F32)[:, None] * inv[None, :]
        cos = jnp.tile(jnp.cos(ang), (1, LANES // half))
        sin = jnp.tile(jnp.tile(jnp.sin(ang), (1, 2)) * sign[None, :], (1, LANES // HEAD_DIM))
        return cos, sin

    ca, sa = tabs(step_pos)
    cb, sb = tabs(row_pos)
    return ca[:, None, :], sa[:, None, :], cb, sb


def _attn_prompt_kernel(sinks_ref, q_ref, kp_ref, kc_ref, vp_ref, vc_ref, o_ref):
    i = pl.program_id(0)
    r = lax.broadcasted_iota(jnp.int32, (WINDOW, 2 * WINDOW), 0)
    c = lax.broadcasted_iota(jnp.int32, (WINDOW, 2 * WINDOW), 1)
    band = (c >= r) & (c <= WINDOW + r)
    first_mask = band & ((c >= WINDOW) | (i > 0))
    kcat = jnp.concatenate([kp_ref[...], kc_ref[...]], axis=0)
    vcat = jnp.concatenate([vp_ref[...], vc_ref[...]], axis=0)
    kswap = pltpu.roll(kcat, HEAD_DIM, 1)
    vswap = pltpu.roll(vcat, HEAD_DIM, 1)
    lo = lax.broadcasted_iota(jnp.int32, kcat.shape, 1) < HEAD_DIM
    lo_q = lax.broadcasted_iota(jnp.int32, (WINDOW, LANES), 1) < HEAD_DIM
    ones = jnp.ones(kcat.shape, F32)
    for kv in range(N_KV_HEADS):
        k_own, k_other = (kcat, kswap) if kv == 0 else (kswap, kcat)
        v_own, v_other = (vcat, vswap) if kv == 0 else (vswap, vcat)
        k_half = (jnp.where(lo, k_own, 0.0).astype(BF16), jnp.where(lo, 0.0, k_other).astype(BF16))
        rhs_half = (jnp.concatenate([jnp.where(lo, v_own, 0.0), jnp.where(lo, ones, 0.0)], axis=1).astype(BF16),
                    jnp.concatenate([jnp.where(lo, 0.0, v_other), jnp.where(lo, 0.0, ones)], axis=1).astype(BF16))
        for j in range(ATTN_QBLOCKS):
            qrows = slice(j * WINDOW, (j + 1) * WINDOW)
            krows = slice(j * WINDOW, (j + 2) * WINDOW)
            mask = first_mask if j == 0 else band
            for c in range(Q_PER_KV // 2):
                tile = kv * (Q_PER_KV // 2) + c
                qt = q_ref[qrows, tile * LANES:(tile + 1) * LANES]
                acc = None
                sink_terms = []
                for side in range(2):
                    s = jnp.where(mask, _dot_nt(qt, k_half[side][krows]), NEG_INF)
                    sink = sinks_ref[2 * tile + side]
                    m = jnp.maximum(jnp.max(s, axis=-1, keepdims=True), sink)
                    pv = _dot(jnp.exp(s - m).astype(BF16), rhs_half[side][krows])
                    acc = pv if acc is None else acc + pv
                    sink_terms.append(jnp.exp(sink - m))
                denom = acc[:, LANES:] + jnp.where(lo_q, sink_terms[0], sink_terms[1])
                o_ref[qrows, tile * LANES:(tile + 1) * LANES] = (acc[:, :LANES] / denom).astype(BF16)


def _attn_prompt(sinks, q, k, v):
    t = q.shape[0]
    rows = ATTN_QBLOCKS * WINDOW
    cur = lambda w: pl.BlockSpec((rows, w), lambda i: (i, 0))
    prev = lambda w: pl.BlockSpec((WINDOW, w), lambda i: (jnp.maximum(i * ATTN_QBLOCKS - 1, 0), 0))
    return pl.pallas_call(
        _attn_prompt_kernel,
        grid=(t // rows,),
        in_specs=[pl.BlockSpec(memory_space=pltpu.SMEM), cur(ATTN_WIDTH), prev(KV_WIDTH), cur(KV_WIDTH),
                  prev(KV_WIDTH), cur(KV_WIDTH)],
        out_specs=cur(ATTN_WIDTH),
        out_shape=jax.ShapeDtypeStruct((t, ATTN_WIDTH), BF16),
        compiler_params=_cparams(("parallel",)),
        name="attn_prompt",
    )(sinks, q, k, k, v, v)


def _attn_sample_kernel(sinks_ref, q_ref, k_ref, v_ref, ckt_ref, cvt_ref, o_ref, nkt_ref, nvt_ref, *, nb, ts):
    rows = nb * ts
    q = q_ref[...]
    pad = jnp.zeros((LANES - rows, KV_WIDTH), F32)
    knt = jnp.concatenate([k_ref[...], pad], axis=0).T
    vnt = jnp.concatenate([v_ref[...], pad], axis=0).T
    lane = lax.broadcasted_iota(jnp.int32, (HEAD_DIM, WINDOW), 1)
    is_new = lane >= WINDOW - ts
    for b in range(nb):
        knt_b = pltpu.roll(knt, (WINDOW - ts - b * ts) % LANES, 1)
        vnt_b = pltpu.roll(vnt, (WINDOW - ts - b * ts) % LANES, 1)
        for kv in range(N_KV_HEADS):
            ksl = slice(kv * HEAD_DIM, (kv + 1) * HEAD_DIM)
            nkt_ref[b, kv] = jnp.where(is_new, knt_b[ksl], pltpu.roll(ckt_ref[b, kv], WINDOW - ts, 1))
            nvt_ref[b, kv] = jnp.where(is_new, vnt_b[ksl], pltpu.roll(cvt_ref[b, kv], WINDOW - ts, 1))

    knt = knt[:, :rows].astype(BF16)
    vnt = vnt[:, :rows].astype(BF16)
    grows = Q_PER_KV * rows
    row1 = lax.broadcasted_iota(jnp.int32, (grows, nb * WINDOW), 0)
    col1 = lax.broadcasted_iota(jnp.int32, (grows, nb * WINDOW), 1)
    rseq1 = (row1 % rows) // ts
    mask_ctx = (rseq1 == col1 // WINDOW) & (col1 % WINDOW >= row1 % ts)
    row2 = lax.broadcasted_iota(jnp.int32, (grows, rows), 0)
    col2 = lax.broadcasted_iota(jnp.int32, (grows, rows), 1)
    mask_new = ((row2 % rows) // ts == col2 // ts) & (col2 % ts <= row2 % ts)
    ghead = lax.broadcasted_iota(jnp.int32, (grows, 1), 0) // rows
    for kv in range(N_KV_HEADS):
        ksl = slice(kv * HEAD_DIM, (kv + 1) * HEAD_DIM)
        qg = jnp.concatenate([q[:, (kv * Q_PER_KV + g) * HEAD_DIM:(kv * Q_PER_KV + g + 1) * HEAD_DIM]
                              for g in range(Q_PER_KV)], axis=0)
        kctx = jnp.concatenate([ckt_ref[b, kv] for b in range(nb)], axis=1).astype(BF16)
        vctx = jnp.concatenate([cvt_ref[b, kv] for b in range(nb)], axis=1).astype(BF16)
        s1 = jnp.where(mask_ctx, _dot(qg, kctx), NEG_INF)
        s2 = jnp.where(mask_new, _dot(qg, knt[ksl]), NEG_INF)
        sink = jnp.zeros((grows, 1), F32)
        for g in range(Q_PER_KV):
            sink = jnp.where(ghead == g, sinks_ref[kv * Q_PER_KV + g], sink)
        m = jnp.maximum(jnp.maximum(jnp.max(s1, axis=-1, keepdims=True), jnp.max(s2, axis=-1, keepdims=True)), sink)
        p1 = jnp.exp(s1 - m)
        p2 = jnp.exp(s2 - m)
        denom = jnp.sum(p1, axis=-1, keepdims=True) + jnp.sum(p2, axis=-1, keepdims=True) + jnp.exp(sink - m)
        o = (_dot_nt(p1.astype(BF16), vctx) + _dot_nt(p2.astype(BF16), vnt[ksl])) / denom
        for g in range(Q_PER_KV):
            h = kv * Q_PER_KV + g
            o_ref[:, h * HEAD_DIM:(h + 1) * HEAD_DIM] = o[g * rows:(g + 1) * rows].astype(BF16)


def _attn_sample(sinks, q, k, v, ckt, cvt, *, nb, ts):
    nseq = ckt.shape[0]
    rows = nb * ts
    assert rows <= LANES
    row = lambda w: pl.BlockSpec((rows, w), lambda i: (i, 0))
    win = pl.BlockSpec((nb, N_KV_HEADS, HEAD_DIM, WINDOW), lambda i: (i, 0, 0, 0))
    win_shape = jax.ShapeDtypeStruct((nseq, N_KV_HEADS, HEAD_DIM, WINDOW), F32)
    return pl.pallas_call(
        functools.partial(_attn_sample_kernel, nb=nb, ts=ts),
        grid=(nseq // nb,),
        in_specs=[pl.BlockSpec(memory_space=pltpu.SMEM), row(ATTN_WIDTH), row(KV_WIDTH), row(KV_WIDTH), win, win],
        out_specs=[row(ATTN_WIDTH), win, win],
        out_shape=[jax.ShapeDtypeStruct((nseq * ts, ATTN_WIDTH), BF16), win_shape, win_shape],
        compiler_params=_cparams(("parallel",)),
        name="attn_sample",
    )(sinks, q, k, v, ckt, cvt)


def _ssm_prep_kernel(are_ref, aim_ref, ldt_ref, bre_ref, bim_ref, lre_ref, lim_ref, bbre_ref, bbim_ref):
    a_re = are_ref[...]
    a_im = aim_ref[...]
    dt = jnp.exp(ldt_ref[...])
    mag = jnp.exp(a_re * dt)
    lb_re = mag * jnp.cos(a_im * dt)
    lb_im = mag * jnp.sin(a_im * dt)
    den = a_re * a_re + a_im * a_im
    nr = lb_re - 1.0
    ni = lb_im
    k_re = (nr * a_re + ni * a_im) / den
    k_im = (ni * a_re - nr * a_im) / den
    lre_ref[...] = lb_re
    lim_ref[...] = lb_im
    b_re = bre_ref[...]
    b_im = bim_ref[...]
    bbre_ref[...] = k_re[:, None, :] * b_re - k_im[:, None, :] * b_im
    bbim_ref[...] = k_re[:, None, :] * b_im + k_im[:, None, :] * b_re


def _ssm_prep(a_re, a_im, log_dt, b_re, b_im):
    g, n, gs = b_re.shape
    sds = jax.ShapeDtypeStruct
    return pl.pallas_call(
        _ssm_prep_kernel,
        out_shape=[sds((g, n), F32), sds((g, n), F32), sds((g, gs, n), F32), sds((g, gs, n), F32)],
        name="ssm_prep",
    )(a_re, a_im, log_dt.reshape(g, 1), jnp.swapaxes(b_re, 1, 2), jnp.swapaxes(b_im, 1, 2))


def _block_diag_halves(m):
    g, a, b = m.shape
    gh = g // 2
    tiled = jnp.tile(m.reshape(2, gh * a, b), (1, 1, gh))
    row_group = lax.broadcasted_iota(jnp.int32, tiled.shape, 1) // a
    col_group = lax.broadcasted_iota(jnp.int32, tiled.shape, 2) // b
    return jnp.where(row_group == col_group, tiled, 0.0)


def _ssm_y(u, scan_block, cc_ref, d_ref):
    ys = []
    for half in range(2):
        hc = jnp.concatenate([scan_block(half, jj) for jj in range(BLK_PER_HALF)], axis=1).astype(BF16)
        ys.append(_dot(hc, cc_ref[half]))
    y = jnp.concatenate(ys, axis=1) + d_ref[...] * u
    return jax.nn.gelu(y).astype(BF16)


def _ssm_prompt_kernel(u_ref, bb_ref, cc_ref, lam_ref, d_ref, h0_ref, z_ref, hout_ref,
                       x0_ref, x1_ref, s0_ref, s1_ref, hst_ref):
    tb, pitch = SSM_TB, SSM_PITCH
    x_bufs = (x0_ref, x1_ref)
    s_bufs = (s0_ref, s1_ref)

    @pl.when(pl.program_id(0) == 0)
    def _():
        hst_ref[...] = h0_ref[...]

    def b_proj(k):
        ub = u_ref[k * tb:(k + 1) * tb, :].astype(BF16)
        for half in range(2):
            x = _dot(ub[:, half * HALF_U:(half + 1) * HALF_U], bb_ref[half])
            for jj in range(BLK_PER_HALF):
                x_bufs[k % 2][pl.ds((half * BLK_PER_HALF + jj) * pitch, tb), :] = x[:, jj * LANES:(jj + 1) * LANES]

    lam = [lam_ref[g] for g in range(4)]
    carry = [hst_ref[g] for g in range(4)]

    def scan(k):
        for t in range(tb):
            for half in range(2):
                hr, hi = carry[2 * half], carry[2 * half + 1]
                lr, li = lam[2 * half], lam[2 * half + 1]
                rows_re = pl.ds((half * BLK_PER_HALF) * pitch + t, SUBLANES, stride=pitch)
                rows_im = pl.ds((half * BLK_PER_HALF + SUBLANES) * pitch + t, SUBLANES, stride=pitch)
                nr = lr * hr - li * hi + x_bufs[k % 2][rows_re, :]
                ni = lr * hi + li * hr + x_bufs[k % 2][rows_im, :]
                s_bufs[k % 2][rows_re, :] = nr
                s_bufs[k % 2][rows_im, :] = ni
                carry[2 * half], carry[2 * half + 1] = nr, ni

    def c_proj(k):
        z_ref[k * tb:(k + 1) * tb, :] = _ssm_y(
            u_ref[k * tb:(k + 1) * tb, :],
            lambda half, jj: s_bufs[k % 2][pl.ds((half * BLK_PER_HALF + jj) * pitch, tb), :], cc_ref, d_ref)

    b_proj(0)
    for k in range(SSM_NB):
        if k + 1 < SSM_NB:
            b_proj(k + 1)
        scan(k)
        c_proj(k)
    for g in range(4):
        hst_ref[g] = carry[g]
        hout_ref[g] = carry[g]


def _ssm_prompt(u, bb, cc, lam, d, h0):
    t = u.shape[0]
    rows = SSM_NB * SSM_TB
    return pl.pallas_call(
        _ssm_prompt_kernel,
        grid=(t // rows,),
        in_specs=[pl.BlockSpec((rows, SSM_WIDTH), lambda i: (i, 0)), _const_spec(bb.shape), _const_spec(cc.shape),
                  _const_spec(lam.shape), _const_spec(d.shape), _const_spec(h0.shape)],
        out_specs=[pl.BlockSpec((rows, SSM_WIDTH), lambda i: (i, 0)),
                   pl.BlockSpec((4, SUBLANES, LANES), lambda i: (0, 0, 0))],
        out_shape=[jax.ShapeDtypeStruct((t, SSM_WIDTH), BF16), jax.ShapeDtypeStruct((4, SUBLANES, LANES), F32)],
        scratch_shapes=[pltpu.VMEM((2 * BLK_PER_HALF * SSM_PITCH, LANES), F32)] * 4
                       + [pltpu.VMEM((4, SUBLANES, LANES), F32)],
        compiler_params=_cparams(("arbitrary",)),
        name="ssm_prompt",
    )(u, bb, cc, lam, d, h0)


def _ssm_sample_kernel(u_ref, bb_ref, cc_ref, lre_ref, lim_ref, d_ref, h0re_ref, h0im_ref,
                       z_ref, hre_ref, him_ref, xs_ref, *, nb, ts):
    u = u_ref[...]
    ub = u.astype(BF16)
    for half in range(2):
        x = _dot(ub[:, half * HALF_U:(half + 1) * HALF_U], bb_ref[half])
        for jj in range(BLK_PER_HALF):
            xs_ref[half * BLK_PER_HALF + jj] = x[:, jj * LANES:(jj + 1) * LANES]
    for cb in range(SSM_CH // LANES):
        half, jj = divmod(cb, SUBLANES)
        j_re = half * BLK_PER_HALF + jj
        j_im = j_re + SUBLANES
        csl = slice(cb * LANES, (cb + 1) * LANES)
        lr = lre_ref[:, csl]
        li = lim_ref[:, csl]
        hr = h0re_ref[:, csl]
        hi = h0im_ref[:, csl]
        for t in range(ts):
            rows = pl.ds(t, nb, stride=ts)
            nr = lr * hr - li * hi + xs_ref[j_re, rows, :]
            ni = lr * hi + li * hr + xs_ref[j_im, rows, :]
            xs_ref[j_re, rows, :] = nr
            xs_ref[j_im, rows, :] = ni
            hr, hi = nr, ni
        hre_ref[:, csl] = hr
        him_ref[:, csl] = hi
    z_ref[...] = _ssm_y(u, lambda half, jj: xs_ref[half * BLK_PER_HALF + jj], cc_ref, d_ref)


def _ssm_sample(u, bb, cc, lre, lim, d, h0re, h0im, *, nb, ts):
    nseq = h0re.shape[0]
    rows = nb * ts
    st = pl.BlockSpec((nb, SSM_CH), lambda i: (i, 0))
    return pl.pallas_call(
        functools.partial(_ssm_sample_kernel, nb=nb, ts=ts),
        grid=(nseq // nb,),
        in_specs=[pl.BlockSpec((rows, SSM_WIDTH), lambda i: (i, 0)), _const_spec(bb.shape), _const_spec(cc.shape),
                  _const_spec(lre.shape), _const_spec(lim.shape), _const_spec(d.shape), st, st],
        out_specs=[pl.BlockSpec((rows, SSM_WIDTH), lambda i: (i, 0)), st, st],
        out_shape=[jax.ShapeDtypeStruct((nseq * ts, SSM_WIDTH), BF16),
                   jax.ShapeDtypeStruct((nseq, SSM_CH), F32), jax.ShapeDtypeStruct((nseq, SSM_CH), F32)],
        scratch_shapes=[pltpu.VMEM((2 * BLK_PER_HALF, rows, LANES), F32)],
        compiler_params=_cparams(("parallel",)),
        name="ssm_sample",
    )(u, bb, cc, lre, lim, d, h0re, h0im)


def _xattn_heads(q, mk, mv):
    outs = []
    for h in range(N_X_HEADS):
        sl = slice(h * X_HEAD_DIM, (h + 1) * X_HEAD_DIM)
        s = _dot_nt(q[:, sl], mk[:, sl])
        p = jnp.exp(s - jnp.max(s, axis=-1, keepdims=True))
        outs.append(_dot(p.astype(BF16), mv[:, sl]) / jnp.sum(p, axis=-1, keepdims=True))
    return jnp.concatenate(outs, axis=1)


def _merge_kernel(*refs, shared_mem):
    if shared_mem:
        (x1_ref, o_ref, z_ref, ga_ref, gb_ref, wup_ref, wglu_ref, wout_ref, gx_ref, wxq_ref, mk_ref, mv_ref,
         x2_ref, out_ref) = refs
    else:
        x1_ref, o_ref, z_ref, ga_ref, gb_ref, wup_ref, wglu_ref, wout_ref, gx_ref, wxq_ref, x2_ref, out_ref = refs
    ya = _dot(o_ref[...], wup_ref[...])
    glu = _dot(z_ref[...], wglu_ref[...])
    yb = glu[:, :D_MODEL] * jax.nn.sigmoid(glu[:, D_MODEL:])
    merged = ga_ref[...].astype(F32) * ya + gb_ref[...].astype(F32) * yb
    x2 = x1_ref[...] + _dot(merged.astype(BF16), wout_ref[...])
    x2_ref[...] = x2
    hx = _rms(x2, gx_ref[...]).astype(BF16)
    qx = (_dot(hx, wxq_ref[...]) * (X_HEAD_DIM ** -0.5)).astype(BF16)
    if shared_mem:
        out_ref[...] = _xattn_heads(qx, mk_ref[...].astype(BF16), mv_ref[...].astype(BF16)).astype(BF16)
    else:
        out_ref[...] = qx


def _merge(x1, o, z, ga, gb, wup, wglu, wout, gx, wxq, *, tm, mem=None):
    t = x1.shape[0]
    row = lambda w: pl.BlockSpec((tm, w), lambda i: (i, 0))
    consts = [wup, wglu, wout, gx, wxq] + (list(mem) if mem is not None else [])
    return pl.pallas_call(
        functools.partial(_merge_kernel, shared_mem=mem is not None),
        grid=(t // tm,),
        in_specs=[row(D_MODEL), row(ATTN_WIDTH), row(SSM_WIDTH), row(D_MODEL), row(D_MODEL)]
                 + [_const_spec(c.shape) for c in consts],
        out_specs=[row(D_MODEL), row(D_MODEL)],
        out_shape=[jax.ShapeDtypeStruct((t, D_MODEL), F32), jax.ShapeDtypeStruct((t, D_MODEL), BF16)],
        compiler_params=_cparams(("parallel",)),
        name="merge",
    )(x1, o, z, ga, gb, *consts)


def _memkv_kernel(mem_ref, g_ref, wk_ref, wv_ref, mk_ref, mv_ref):
    mn = _rms(mem_ref[...], g_ref[...]).astype(BF16)
    mk_ref[...] = _dot(mn, wk_ref[...])
    mv_ref[...] = _dot(mn, wv_ref[...])


def _memkv(mem, g, wk, wv):
    sds = jax.ShapeDtypeStruct((mem.shape[0], D_MODEL), F32)
    return pl.pallas_call(_memkv_kernel, out_shape=[sds, sds], name="memkv",
                          compiler_params=pltpu.CompilerParams(vmem_limit_bytes=VMEM_LIMIT))(mem, g, wk, wv)


def _xattn_sample_kernel(q_ref, mk_ref, mv_ref, o_ref, *, nb, ts):
    qrows = ts * N_X_HEADS
    krows = N_MEM * N_X_HEADS
    row = lax.broadcasted_iota(jnp.int32, (qrows, krows), 0)
    col = lax.broadcasted_iota(jnp.int32, (qrows, krows), 1)
    same_head = (row // ts) == (col & (N_X_HEADS - 1))
    for b in range(nb):
        qb = q_ref[b * ts:(b + 1) * ts, :]
        qs = jnp.concatenate([qb[:, h * X_HEAD_DIM:(h + 1) * X_HEAD_DIM] for h in range(N_X_HEADS)], axis=0)
        kf = mk_ref[b].reshape(krows, X_HEAD_DIM).astype(BF16)
        vf = mv_ref[b].reshape(krows, X_HEAD_DIM).astype(BF16)
        s = jnp.where(same_head, _dot_nt(qs, kf), NEG_INF)
        p = jnp.exp(s - jnp.max(s, axis=-1, keepdims=True))
        o = (_dot(p.astype(BF16), vf) / jnp.sum(p, axis=-1, keepdims=True)).astype(BF16)
        for h in range(N_X_HEADS):
            o_ref[b * ts:(b + 1) * ts, h * X_HEAD_DIM:(h + 1) * X_HEAD_DIM] = o[h * ts:(h + 1) * ts]


def _xattn_sample(qx, mk, mv, *, nb, ts):
    nseq = mk.shape[0]
    row = pl.BlockSpec((nb * ts, D_MODEL), lambda i: (i, 0))
    mem = pl.BlockSpec((nb, N_MEM, N_X_HEADS, X_HEAD_DIM), lambda i: (i, 0, 0, 0))
    return pl.pallas_call(
        functools.partial(_xattn_sample_kernel, nb=nb, ts=ts),
        grid=(nseq // nb,),
        in_specs=[row, mem, mem],
        out_specs=row,
        out_shape=jax.ShapeDtypeStruct((nseq * ts, D_MODEL), BF16),
        compiler_params=_cparams(("parallel",)),
        name="xattn_sample",
    )(qx, mk, mv)


def kernel(x_prompt, x_sample, cache_win_k, cache_win_v, state_ssm_re, state_ssm_im, cache_mem_k, cache_mem_v, mem_prompt, g_ffn1, w_ffn1_in, w_ffn1_out, g_mix, w_in, attn_sinks, ssm_a_re, ssm_a_im, ssm_log_dt, ssm_b_re, ssm_b_im, ssm_c_re, ssm_c_im, ssm_d, w_attn_up, w_ssm_glu, w_out, g_xattn, g_mem, w_xq, w_xk, w_xv, w_xo, g_ffn2, w_ffn2_in, w_ffn2_out, g_final):
    assert x_prompt.shape[0] == 1 and g_ffn1.shape[0] == 1
    seq = x_prompt.shape[1]
    nseq, ts = x_sample.shape[0], x_sample.shape[1]
    past_len = seq
    l = 0
    bf = lambda w: w[l].astype(BF16)
    vec = lambda g: g[l].reshape(1, -1)
    w1i, w1o, w2i, w2o = bf(w_ffn1_in), bf(w_ffn1_out), bf(w_ffn2_in), bf(w_ffn2_out)
    win, wup, wglu, wout = bf(w_in), bf(w_attn_up), bf(w_ssm_glu), bf(w_out)
    wxq, wxk, wxv, wxo = bf(w_xq), bf(w_xk), bf(w_xv), bf(w_xo)
    gf = g_final.reshape(1, -1)
    sinks = attn_sinks[l]

    lam_re, lam_im, bbt_re, bbt_im = _ssm_prep(ssm_a_re[l], ssm_a_im[l], ssm_log_dt[l], ssm_b_re[l], ssm_b_im[l])
    bb = jnp.concatenate([_block_diag_halves(bbt_re), _block_diag_halves(bbt_im)], axis=-1).astype(BF16)
    ct_re = jnp.swapaxes(ssm_c_re[l], 1, 2)
    ct_im = jnp.swapaxes(ssm_c_im[l], 1, 2)
    cc = jnp.concatenate([_block_diag_halves(ct_re), -_block_diag_halves(ct_im)], axis=1).astype(BF16)
    d_skip = ssm_d[l].reshape(1, -1)
    lre16 = lam_re.reshape(2, SUBLANES, LANES)
    lim16 = lam_im.reshape(2, SUBLANES, LANES)
    lam_tm = jnp.stack([lre16[0], lim16[0], lre16[1], lim16[1]])
    lre_row = lam_re.reshape(1, SSM_CH)
    lim_row = lam_im.reshape(1, SSM_CH)

    mk_p, mv_p = _memkv(mem_prompt[0], vec(g_mem), wxk, wxv)

    def group(x, rope_tabs, attn_fn, ssm_fn, shared_mem, xattn_fn):
        x1 = _ffn(x, vec(g_ffn1), w1i, w1o, gf, tm=TM_FFN)
        q, k, v, u, ga, gb = _inproj(x1, vec(g_mix), win, rope_tabs, tm=TM_MIX)
        o, extra_attn = attn_fn(q, k, v)
        z, extra_ssm = ssm_fn(u)
        x2, ox = _merge(x1, o, z, ga, gb, wup, wglu, wout, vec(g_xattn), wxq, tm=TM_MIX, mem=shared_mem)
        if shared_mem is None:
            ox = xattn_fn(ox)
        y = _ffn(x2, vec(g_ffn2), w2i, w2o, gf, tm=TM_FFN, pre=(ox, wxo), final=True)
        return y, extra_attn, extra_ssm

    tm = TM_MIX
    assert tm % ts == 0
    rope_p = _rope_tables(jnp.arange(seq // tm) * tm, jnp.arange(tm))

    def attn_p(q, k, v):
        return _attn_prompt(sinks, q, k, v), (k[-WINDOW:], v[-WINDOW:])

    def ssm_p(u):
        z, hout = _ssm_prompt(u, bb, cc, lam_tm, d_skip, jnp.zeros((4, SUBLANES, LANES), F32))
        return z, hout

    y_p, (nk_p, nv_p), hout_p = group(x_prompt[0], rope_p, attn_p, ssm_p, (mk_p, mv_p), None)
    hre_p = jnp.concatenate([hout_p[0], hout_p[2]], axis=0)
    him_p = jnp.concatenate([hout_p[1], hout_p[3]], axis=0)

    rope_s = _rope_tables(jnp.full((nseq * ts // tm,), past_len), jnp.arange(tm) % ts)

    def attn_s(q, k, v):
        to_t = lambda c: jnp.transpose(c[l], (0, 2, 3, 1))
        o, nkt, nvt = _attn_sample(sinks, q, k, v, to_t(cache_win_k), to_t(cache_win_v), nb=8, ts=ts)
        return o, (jnp.transpose(nkt, (0, 3, 1, 2)), jnp.transpose(nvt, (0, 3, 1, 2)))

    def ssm_s(u):
        z, hre, him = _ssm_sample(u, bb, cc, lre_row, lim_row, d_skip, state_ssm_re[l].reshape(nseq, SSM_CH),
                                  state_ssm_im[l].reshape(nseq, SSM_CH), nb=64, ts=ts)
        return z, (hre, him)

    y_s, (nk_s, nv_s), (hre_s, him_s) = group(
        x_sample.reshape(nseq * ts, D_MODEL), rope_s, attn_s, ssm_s, None,
        lambda qx: _xattn_sample(qx, cache_mem_k[l], cache_mem_v[l], nb=4, ts=ts))

    kvshape = (1, 1, WINDOW, N_KV_HEADS, HEAD_DIM)
    stshape = (1, 1, N_SSM_GROUPS, SSM_STATE)
    memshape = (1, 1, N_MEM, N_X_HEADS, X_HEAD_DIM)
    return (y_p.reshape(1, seq, D_MODEL), y_s.reshape(nseq, ts, D_MODEL),
            nk_p.reshape(kvshape), nv_p.reshape(kvshape), hre_p.reshape(stshape), him_p.reshape(stshape),
            mk_p.reshape(memshape), mv_p.reshape(memshape),
            nk_s.reshape(1, nseq, WINDOW, N_KV_HEADS, HEAD_DIM), nv_s.reshape(1, nseq, WINDOW, N_KV_HEADS, HEAD_DIM),
            hre_s.reshape(1, nseq, N_SSM_GROUPS, SSM_STATE), him_s.reshape(1, nseq, N_SSM_GROUPS, SSM_STATE))
```

```python
import functools
import math

import jax
import jax.numpy as jnp
from jax import lax
from jax.experimental import pallas as pl
from jax.experimental.pallas import tpu as pltpu

F32 = jnp.float32
BF16 = jnp.bfloat16

D_MODEL = 1024
N_Q_HEADS = 8
N_KV_HEADS = 2
HEAD_DIM = 64
Q_PER_KV = N_Q_HEADS // N_KV_HEADS
ATTN_WIDTH = N_Q_HEADS * HEAD_DIM
KV_WIDTH = N_KV_HEADS * HEAD_DIM
WINDOW = 128
ROPE_THETA = 10000.0
SSM_WIDTH = D_MODEL // 2
SSM_GROUP = 16
N_SSM_GROUPS = SSM_WIDTH // SSM_GROUP
SSM_STATE = 64
N_MEM = 256
N_X_HEADS = 4
X_HEAD_DIM = D_MODEL // N_X_HEADS
D_FF = 2816
RMS_EPS = 1e-6
NEG_INF = -1e30
IN_SPLITS = (ATTN_WIDTH, KV_WIDTH, KV_WIDTH, SSM_WIDTH, D_MODEL, D_MODEL)
IN_WIDTH = sum(IN_SPLITS)
OFF_Q, OFF_K, OFF_V, OFF_U, OFF_GA, OFF_GB = (0, 512, 640, 768, 1280, 2304)

LANES = 128
SUBLANES = 8
VMEM_LIMIT = 56 * 1024 * 1024

SSM_CH = N_SSM_GROUPS * SSM_STATE
HALF_CH = SSM_CH // 2
HALF_U = SSM_WIDTH // 2
BLK_PER_HALF = 2 * HALF_CH // LANES
TM_FFN = 1024
FFN_CHUNKS = ((0, 768), (768, 1536), (1536, 2304), (2304, D_FF))
TM_MIX = 1024
ATTN_QBLOCKS = 8
SSM_TB = 256
SSM_NB = 2
SSM_PITCH = SSM_TB + 4


def _cparams(sem):
    return pltpu.CompilerParams(dimension_semantics=sem, vmem_limit_bytes=VMEM_LIMIT)


def _const_spec(shape):
    nd = len(shape)
    return pl.BlockSpec(shape, lambda *_: (0,) * nd, pipeline_mode=pl.Buffered(1))


def _rms(x, g):
    return x * lax.rsqrt(jnp.mean(x * x, axis=-1, keepdims=True) + RMS_EPS) * g


def _dot(a, b):
    return jnp.dot(a, b, preferred_element_type=F32)


def _dot_nt(a, b):
    return lax.dot_general(a, b, (((1,), (1,)), ((), ())), preferred_element_type=F32)


def _ffn_kernel(*refs, has_pre, final):
    if has_pre:
        x_ref, ox_ref, wxo_ref, g_ref, wi_ref, wo_ref, gf_ref, o_ref = refs
        x = x_ref[...] + _dot(ox_ref[...], wxo_ref[...])
    else:
        x_ref, g_ref, wi_ref, wo_ref, gf_ref, o_ref = refs
        x = x_ref[...]
    h = _rms(x, g_ref[...]).astype(BF16)
    acc = None
    for lo, hi in FFN_CHUNKS:
        a = _dot(h, wi_ref[:, lo:hi])
        b = _dot(h, wi_ref[:, D_FF + lo:D_FF + hi])
        part = _dot((a * jax.nn.sigmoid(a) * b).astype(BF16), wo_ref[lo:hi, :])
        acc = part if acc is None else acc + part
    y = x + 0.5 * acc
    if final:
        y = _rms(y, gf_ref[...])
    o_ref[...] = y


def _ffn(x, g, wi, wo, gf, *, tm, pre=None, final=False):
    t = x.shape[0]
    row = lambda w: pl.BlockSpec((tm, w), lambda i: (i, 0))
    in_specs = [row(D_MODEL)]
    args = [x]
    if pre is not None:
        ox, wxo = pre
        in_specs += [row(D_MODEL), _const_spec((D_MODEL, D_MODEL))]
        args += [ox, wxo]
    in_specs += [_const_spec((1, D_MODEL)), _const_spec((D_MODEL, 2 * D_FF)), _const_spec((D_FF, D_MODEL)),
                 _const_spec((1, D_MODEL))]
    args += [g, wi, wo, gf]
    return pl.pallas_call(
        functools.partial(_ffn_kernel, has_pre=pre is not None, final=final),
        grid=(t // tm,),
        in_specs=in_specs,
        out_specs=row(D_MODEL),
        out_shape=jax.ShapeDtypeStruct((t, D_MODEL), F32),
        compiler_params=_cparams(("parallel",)),
        name="tail" if pre is not None else "ffn",
    )(*args)


def _inproj_kernel(x_ref, g_ref, w_ref, ca_ref, sa_ref, cb_ref, sb_ref, q_ref, k_ref, v_ref, u_ref, ga_ref, gb_ref):
    h = _rms(x_ref[...], g_ref[...]).astype(BF16)
    ca, sa, cb, sb = ca_ref[0], sa_ref[0], cb_ref[...], sb_ref[...]
    cos = ca * cb - sa * sb
    sin = sa * cb + ca * sb
    lane = lax.broadcasted_iota(jnp.int32, cos.shape, 1)
    first_half = (lane & (HEAD_DIM - 1)) < (HEAD_DIM // 2)

    def rope(xc):
        rot = jnp.where(first_half, pltpu.roll(xc, LANES - HEAD_DIM // 2, 1), pltpu.roll(xc, HEAD_DIM // 2, 1))
        return xc * cos + rot * sin

    yq = _dot(h, w_ref[:, OFF_Q:OFF_K])
    for c in range(ATTN_WIDTH // LANES):
        q_ref[:, c * LANES:(c + 1) * LANES] = (rope(yq[:, c * LANES:(c + 1) * LANES]) * (HEAD_DIM ** -0.5)).astype(BF16)
    ykvu = _dot(h, w_ref[:, OFF_K:OFF_GA])
    k_ref[...] = rope(ykvu[:, :KV_WIDTH])
    v_ref[...] = ykvu[:, KV_WIDTH:2 * KV_WIDTH]
    u_ref[...] = ykvu[:, 2 * KV_WIDTH:]
    ga_ref[...] = jax.nn.sigmoid(_dot(h, w_ref[:, OFF_GA:OFF_GB])).astype(BF16)
    gb_ref[...] = jax.nn.sigmoid(_dot(h, w_ref[:, OFF_GB:])).astype(BF16)


def _inproj(x, g, w, rope_tabs, *, tm):
    t = x.shape[0]
    row = lambda wd: pl.BlockSpec((tm, wd), lambda i: (i, 0))
    step_tab = pl.BlockSpec((1, 1, LANES), lambda i: (i, 0, 0))
    widths = (ATTN_WIDTH, KV_WIDTH, KV_WIDTH, SSM_WIDTH, D_MODEL, D_MODEL)
    dtypes = (BF16, F32, F32, F32, BF16, BF16)
    return pl.pallas_call(
        _inproj_kernel,
        grid=(t // tm,),
        in_specs=[row(D_MODEL), _const_spec((1, D_MODEL)), _const_spec((D_MODEL, IN_WIDTH)), step_tab, step_tab,
                  _const_spec((tm, LANES)), _const_spec((tm, LANES))],
        out_specs=[row(wd) for wd in widths],
        out_shape=[jax.ShapeDtypeStruct((t, wd), dt) for wd, dt in zip(widths, dtypes)],
        compiler_params=_cparams(("parallel",)),
        name="inproj",
    )(x, g, w, *rope_tabs)


def _rope_tables(step_pos, row_pos):
    half = HEAD_DIM // 2
    inv = ROPE_THETA ** (-jnp.arange(half, dtype=F32) / half)
    sign = jnp.concatenate([-jnp.ones((half,), F32), jnp.ones((half,), F32)])

    def tabs(pos):
        ang = pos.astype(F32)[:, None] * inv[None, :]
        cos = jnp.tile(jnp.cos(ang), (1, LANES // half))
        sin = jnp.tile(jnp.tile(jnp.sin(ang), (1, 2)) * sign[None, :], (1, LANES // HEAD_DIM))
        return cos, sin

    ca, sa = tabs(step_pos)
    cb, sb = tabs(row_pos)
    return ca[:, None, :], sa[:, None, :], cb, sb


def _attn_prompt_kernel(sinks_ref, q_ref, kp_ref, kc_ref, vp_ref, vc_ref, o_ref):
    i = pl.program_id(0)
    r = lax.broadcasted_iota(jnp.int32, (WINDOW, 2 * WINDOW), 0)
    c = lax.broadcasted_iota(jnp.int32, (WINDOW, 2 * WINDOW), 1)
    band = (c >= r) & (c <= WINDOW + r)
    first_mask = band & ((c >= WINDOW) | (i > 0))
    kcat = jnp.concatenate([kp_ref[...], kc_ref[...]], axis=0)
    vcat = jnp.concatenate([vp_ref[...], vc_ref[...]], axis=0)
    kswap = pltpu.roll(kcat, HEAD_DIM, 1)
    vswap = pltpu.roll(vcat, HEAD_DIM, 1)
    lo = lax.broadcasted_iota(jnp.int32, kcat.shape, 1) < HEAD_DIM
    lo_q = lax.broadcasted_iota(jnp.int32, (WINDOW, LANES), 1) < HEAD_DIM
    ones = jnp.ones(kcat.shape, F32)
    for kv in range(N_KV_HEADS):
        k_own, k_other = (kcat, kswap) if kv == 0 else (kswap, kcat)
        v_own, v_other = (vcat, vswap) if kv == 0 else (vswap, vcat)
        k_half = (jnp.where(lo, k_own, 0.0).astype(BF16), jnp.where(lo, 0.0, k_other).astype(BF16))
        rhs_half = (jnp.concatenate([jnp.where(lo, v_own, 0.0), jnp.where(lo, ones, 0.0)], axis=1).astype(BF16),
                    jnp.concatenate([jnp.where(lo, 0.0, v_other), jnp.where(lo, 0.0, ones)], axis=1).astype(BF16))
        for j in range(ATTN_QBLOCKS):
            qrows = slice(j * WINDOW, (j + 1) * WINDOW)
            krows = slice(j * WINDOW, (j + 2) * WINDOW)
            mask = first_mask if j == 0 else band
            for c in range(Q_PER_KV // 2):
                tile = kv * (Q_PER_KV // 2) + c
                qt = q_ref[qrows, tile * LANES:(tile + 1) * LANES]
                acc = None
                sink_terms = []
                for side in range(2):
                    s = jnp.where(mask, _dot_nt(qt, k_half[side][krows]), NEG_INF)
                    sink = sinks_ref[2 * tile + side]
                    m = jnp.maximum(jnp.max(s, axis=-1, keepdims=True), sink)
                    pv = _dot(jnp.exp(s - m).astype(BF16), rhs_half[side][krows])
                    acc = pv if acc is None else acc + pv
                    sink_terms.append(jnp.exp(sink - m))
                denom = acc[:, LANES:] + jnp.where(lo_q, sink_terms[0], sink_terms[1])
                o_ref[qrows, tile * LANES:(tile + 1) * LANES] = (acc[:, :LANES] / denom).astype(BF16)


def _attn_prompt(sinks, q, k, v):
    t = q.shape[0]
    rows = ATTN_QBLOCKS * WINDOW
    cur = lambda w: pl.BlockSpec((rows, w), lambda i: (i, 0))
    prev = lambda w: pl.BlockSpec((WINDOW, w), lambda i: (jnp.maximum(i * ATTN_QBLOCKS - 1, 0), 0))
    return pl.pallas_call(
        _attn_prompt_kernel,
        grid=(t // rows,),
        in_specs=[pl.BlockSpec(memory_space=pltpu.SMEM), cur(ATTN_WIDTH), prev(KV_WIDTH), cur(KV_WIDTH),
                  prev(KV_WIDTH), cur(KV_WIDTH)],
        out_specs=cur(ATTN_WIDTH),
        out_shape=jax.ShapeDtypeStruct((t, ATTN_WIDTH), BF16),
        compiler_params=_cparams(("parallel",)),
        name="attn_prompt",
    )(sinks, q, k, k, v, v)


def _attn_sample_kernel(sinks_ref, q_ref, k_ref, v_ref, ckt_ref, cvt_ref, o_ref, nkt_ref, nvt_ref, *, nb, ts):
    rows = nb * ts
    q = q_ref[...]
    pad = jnp.zeros((LANES - rows, KV_WIDTH), F32)
    knt = jnp.concatenate([k_ref[...], pad], axis=0).T
    vnt = jnp.concatenate([v_ref[...], pad], axis=0).T
    lane = lax.broadcasted_iota(jnp.int32, (HEAD_DIM, WINDOW), 1)
    is_new = lane >= WINDOW - ts
    for b in range(nb):
        knt_b = pltpu.roll(knt, (WINDOW - ts - b * ts) % LANES, 1)
        vnt_b = pltpu.roll(vnt, (WINDOW - ts - b * ts) % LANES, 1)
        for kv in range(N_KV_HEADS):
            ksl = slice(kv * HEAD_DIM, (kv + 1) * HEAD_DIM)
            nkt_ref[b, kv] = jnp.where(is_new, knt_b[ksl], pltpu.roll(ckt_ref[b, kv], WINDOW - ts, 1))
            nvt_ref[b, kv] = jnp.where(is_new, vnt_b[ksl], pltpu.roll(cvt_ref[b, kv], WINDOW - ts, 1))

    knt = knt[:, :rows].astype(BF16)
    vnt = vnt[:, :rows].astype(BF16)
    grows = Q_PER_KV * rows
    row1 = lax.broadcasted_iota(jnp.int32, (grows, nb * WINDOW), 0)
    col1 = lax.broadcasted_iota(jnp.int32, (grows, nb * WINDOW), 1)
    rseq1 = (row1 % rows) // ts
    mask_ctx = (rseq1 == col1 // WINDOW) & (col1 % WINDOW >= row1 % ts)
    row2 = lax.broadcasted_iota(jnp.int32, (grows, rows), 0)
    col2 = lax.broadcasted_iota(jnp.int32, (grows, rows), 1)
    mask_new = ((row2 % rows) // ts == col2 // ts) & (col2 % ts <= row2 % ts)
    ghead = lax.broadcasted_iota(jnp.int32, (grows, 1), 0) // rows
    for kv in range(N_KV_HEADS):
        ksl = slice(kv * HEAD_DIM, (kv + 1) * HEAD_DIM)
        qg = jnp.concatenate([q[:, (kv * Q_PER_KV + g) * HEAD_DIM:(kv * Q_PER_KV + g + 1) * HEAD_DIM]
                              for g in range(Q_PER_KV)], axis=0)
        kctx = jnp.concatenate([ckt_ref[b, kv] for b in range(nb)], axis=1).astype(BF16)
        vctx = jnp.concatenate([cvt_ref[b, kv] for b in range(nb)], axis=1).astype(BF16)
        s1 = jnp.where(mask_ctx, _dot(qg, kctx), NEG_INF)
        s2 = jnp.where(mask_new, _dot(qg, knt[ksl]), NEG_INF)
        sink = jnp.zeros((grows, 1), F32)
        for g in range(Q_PER_KV):
            sink = jnp.where(ghead == g, sinks_ref[kv * Q_PER_KV + g], sink)
        m = jnp.maximum(jnp.maximum(jnp.max(s1, axis=-1, keepdims=True), jnp.max(s2, axis=-1, keepdims=True)), sink)
        p1 = jnp.exp(s1 - m)
        p2 = jnp.exp(s2 - m)
        denom = jnp.sum(p1, axis=-1, keepdims=True) + jnp.sum(p2, axis=-1, keepdims=True) + jnp.exp(sink - m)
        o = (_dot_nt(p1.astype(BF16), vctx) + _dot_nt(p2.astype(BF16), vnt[ksl])) / denom
        for g in range(Q_PER_KV):
            h = kv * Q_PER_KV + g
            o_ref[:, h * HEAD_DIM:(h + 1) * HEAD_DIM] = o[g * rows:(g + 1) * rows].astype(BF16)


def _attn_sample(sinks, q, k, v, ckt, cvt, *, nb, ts):
    nseq = ckt.shape[0]
    rows = nb * ts
    assert rows <= LANES
    row = lambda w: pl.BlockSpec((rows, w), lambda i: (i, 0))
    win = pl.BlockSpec((nb, N_KV_HEADS, HEAD_DIM, WINDOW), lambda i: (i, 0, 0, 0))
    win_shape = jax.ShapeDtypeStruct((nseq, N_KV_HEADS, HEAD_DIM, WINDOW), F32)
    return pl.pallas_call(
        functools.partial(_attn_sample_kernel, nb=nb, ts=ts),
        grid=(nseq // nb,),
        in_specs=[pl.BlockSpec(memory_space=pltpu.SMEM), row(ATTN_WIDTH), row(KV_WIDTH), row(KV_WIDTH), win, win],
        out_specs=[row(ATTN_WIDTH), win, win],
        out_shape=[jax.ShapeDtypeStruct((nseq * ts, ATTN_WIDTH), BF16), win_shape, win_shape],
        compiler_params=_cparams(("parallel",)),
        name="attn_sample",
    )(sinks, q, k, v, ckt, cvt)


def _ssm_prep_kernel(are_ref, aim_ref, ldt_ref, bre_ref, bim_ref, lre_ref, lim_ref, bbre_ref, bbim_ref):
    a_re = are_ref[...]
    a_im = aim_ref[...]
    dt = jnp.exp(ldt_ref[...])
    mag = jnp.exp(a_re * dt)
    lb_re = mag * jnp.cos(a_im * dt)
    lb_im = mag * jnp.sin(a_im * dt)
    den = a_re * a_re + a_im * a_im
    nr = lb_re - 1.0
    ni = lb_im
    k_re = (nr * a_re + ni * a_im) / den
    k_im = (ni * a_re - nr * a_im) / den
    lre_ref[...] = lb_re
    lim_ref[...] = lb_im
    b_re = bre_ref[...]
    b_im = bim_ref[...]
    bbre_ref[...] = k_re[:, None, :] * b_re - k_im[:, None, :] * b_im
    bbim_ref[...] = k_re[:, None, :] * b_im + k_im[:, None, :] * b_re


def _ssm_prep(a_re, a_im, log_dt, b_re, b_im):
    g, n, gs = b_re.shape
    sds = jax.ShapeDtypeStruct
    return pl.pallas_call(
        _ssm_prep_kernel,
        out_shape=[sds((g, n), F32), sds((g, n), F32), sds((g, gs, n), F32), sds((g, gs, n), F32)],
        name="ssm_prep",
    )(a_re, a_im, log_dt.reshape(g, 1), jnp.swapaxes(b_re, 1, 2), jnp.swapaxes(b_im, 1, 2))


def _block_diag_halves(m):
    g, a, b = m.shape
    gh = g // 2
    tiled = jnp.tile(m.reshape(2, gh * a, b), (1, 1, gh))
    row_group = lax.broadcasted_iota(jnp.int32, tiled.shape, 1) // a
    col_group = lax.broadcasted_iota(jnp.int32, tiled.shape, 2) // b
    return jnp.where(row_group == col_group, tiled, 0.0)


def _ssm_y(u, scan_block, cc_ref, d_ref):
    ys = []
    for half in range(2):
        hc = jnp.concatenate([scan_block(half, jj) for jj in range(BLK_PER_HALF)], axis=1).astype(BF16)
        ys.append(_dot(hc, cc_ref[half]))
    y = jnp.concatenate(ys, axis=1) + d_ref[...] * u
    return jax.nn.gelu(y).astype(BF16)


def _ssm_prompt_kernel(u_ref, bb_ref, cc_ref, lam_ref, d_ref, h0_ref, z_ref, hout_ref,
                       x0_ref, x1_ref, s0_ref, s1_ref, hst_ref):
    tb, pitch = SSM_TB, SSM_PITCH
    x_bufs = (x0_ref, x1_ref)
    s_bufs = (s0_ref, s1_ref)

    @pl.when(pl.program_id(0) == 0)
    def _():
        hst_ref[...] = h0_ref[...]

    def b_proj(k):
        ub = u_ref[k * tb:(k + 1) * tb, :].astype(BF16)
        for half in range(2):
            x = _dot(ub[:, half * HALF_U:(half + 1) * HALF_U], bb_ref[half])
            for jj in range(BLK_PER_HALF):
                x_bufs[k % 2][pl.ds((half * BLK_PER_HALF + jj) * pitch, tb), :] = x[:, jj * LANES:(jj + 1) * LANES]

    lam = [lam_ref[g] for g in range(4)]
    carry = [hst_ref[g] for g in range(4)]

    def scan(k):
        for t in range(tb):
            for half in range(2):
                hr, hi = carry[2 * half], carry[2 * half + 1]
                lr, li = lam[2 * half], lam[2 * half + 1]
                rows_re = pl.ds((half * BLK_PER_HALF) * pitch + t, SUBLANES, stride=pitch)
                rows_im = pl.ds((half * BLK_PER_HALF + SUBLANES) * pitch + t, SUBLANES, stride=pitch)
                nr = lr * hr - li * hi + x_bufs[k % 2][rows_re, :]
                ni = lr * hi + li * hr + x_bufs[k % 2][rows_im, :]
                s_bufs[k % 2][rows_re, :] = nr
                s_bufs[k % 2][rows_im, :] = ni
                carry[2 * half], carry[2 * half + 1] = nr, ni

    def c_proj(k):
        z_ref[k * tb:(k + 1) * tb, :] = _ssm_y(
            u_ref[k * tb:(k + 1) * tb, :],
            lambda half, jj: s_bufs[k % 2][pl.ds((half * BLK_PER_HALF + jj) * pitch, tb), :], cc_ref, d_ref)

    b_proj(0)
    for k in range(SSM_NB):
        if k + 1 < SSM_NB:
            b_proj(k + 1)
        scan(k)
        c_proj(k)
    for g in range(4):
        hst_ref[g] = carry[g]
        hout_ref[g] = carry[g]


def _ssm_prompt(u, bb, cc, lam, d, h0):
    t = u.shape[0]
    rows = SSM_NB * SSM_TB
    return pl.pallas_call(
        _ssm_prompt_kernel,
        grid=(t // rows,),
        in_specs=[pl.BlockSpec((rows, SSM_WIDTH), lambda i: (i, 0)), _const_spec(bb.shape), _const_spec(cc.shape),
                  _const_spec(lam.shape), _const_spec(d.shape), _const_spec(h0.shape)],
        out_specs=[pl.BlockSpec((rows, SSM_WIDTH), lambda i: (i, 0)),
                   pl.BlockSpec((4, SUBLANES, LANES), lambda i: (0, 0, 0))],
        out_shape=[jax.ShapeDtypeStruct((t, SSM_WIDTH), BF16), jax.ShapeDtypeStruct((4, SUBLANES, LANES), F32)],
        scratch_shapes=[pltpu.VMEM((2 * BLK_PER_HALF * SSM_PITCH, LANES), F32)] * 4
                       + [pltpu.VMEM((4, SUBLANES, LANES), F32)],
        compiler_params=_cparams(("arbitrary",)),
        name="ssm_prompt",
    )(u, bb, cc, lam, d, h0)


def _ssm_sample_kernel(u_ref, bb_ref, cc_ref, lre_ref, lim_ref, d_ref, h0re_ref, h0im_ref,
                       z_ref, hre_ref, him_ref, xs_ref, *, nb, ts):
    u = u_ref[...]
    ub = u.astype(BF16)
    for half in range(2):
        x = _dot(ub[:, half * HALF_U:(half + 1) * HALF_U], bb_ref[half])
        for jj in range(BLK_PER_HALF):
            xs_ref[half * BLK_PER_HALF + jj] = x[:, jj * LANES:(jj + 1) * LANES]
    for cb in range(SSM_CH // LANES):
        half, jj = divmod(cb, SUBLANES)
        j_re = half * BLK_PER_HALF + jj
        j_im = j_re + SUBLANES
        csl = slice(cb * LANES, (cb + 1) * LANES)
        lr = lre_ref[:, csl]
        li = lim_ref[:, csl]
        hr = h0re_ref[:, csl]
        hi = h0im_ref[:, csl]
        for t in range(ts):
            rows = pl.ds(t, nb, stride=ts)
            nr = lr * hr - li * hi + xs_ref[j_re, rows, :]
            ni = lr * hi + li * hr + xs_ref[j_im, rows, :]
            xs_ref[j_re, rows, :] = nr
            xs_ref[j_im, rows, :] = ni
            hr, hi = nr, ni
        hre_ref[:, csl] = hr
        him_ref[:, csl] = hi
    z_ref[...] = _ssm_y(u, lambda half, jj: xs_ref[half * BLK_PER_HALF + jj], cc_ref, d_ref)


def _ssm_sample(u, bb, cc, lre, lim, d, h0re, h0im, *, nb, ts):
    nseq = h0re.shape[0]
    rows = nb * ts
    st = pl.BlockSpec((nb, SSM_CH), lambda i: (i, 0))
    return pl.pallas_call(
        functools.partial(_ssm_sample_kernel, nb=nb, ts=ts),
        grid=(nseq // nb,),
        in_specs=[pl.BlockSpec((rows, SSM_WIDTH), lambda i: (i, 0)), _const_spec(bb.shape), _const_spec(cc.shape),
                  _const_spec(lre.shape), _const_spec(lim.shape), _const_spec(d.shape), st, st],
        out_specs=[pl.BlockSpec((rows, SSM_WIDTH), lambda i: (i, 0)), st, st],
        out_shape=[jax.ShapeDtypeStruct((nseq * ts, SSM_WIDTH), BF16),
                   jax.ShapeDtypeStruct((nseq, SSM_CH), F32), jax.ShapeDtypeStruct((nseq, SSM_CH), F32)],
        scratch_shapes=[pltpu.VMEM((2 * BLK_PER_HALF, rows, LANES), F32)],
        compiler_params=_cparams(("parallel",)),
        name="ssm_sample",
    )(u, bb, cc, lre, lim, d, h0re, h0im)


def _xattn_heads(q, mk, mv):
    outs = []
    for h in range(N_X_HEADS):
        sl = slice(h * X_HEAD_DIM, (h + 1) * X_HEAD_DIM)
        s = _dot_nt(q[:, sl], mk[:, sl])
        p = jnp.exp(s - jnp.max(s, axis=-1, keepdims=True))
        outs.append(_dot(p.astype(BF16), mv[:, sl]) / jnp.sum(p, axis=-1, keepdims=True))
    return jnp.concatenate(outs, axis=1)


def _merge_kernel(*refs, shared_mem):
    if shared_mem:
        (x1_ref, o_ref, z_ref, ga_ref, gb_ref, wup_ref, wglu_ref, wout_ref, gx_ref, wxq_ref, mk_ref, mv_ref,
         x2_ref, out_ref) = refs
    else:
        x1_ref, o_ref, z_ref, ga_ref, gb_ref, wup_ref, wglu_ref, wout_ref, gx_ref, wxq_ref, x2_ref, out_ref = refs
    ya = _dot(o_ref[...], wup_ref[...])
    glu = _dot(z_ref[...], wglu_ref[...])
    yb = glu[:, :D_MODEL] * jax.nn.sigmoid(glu[:, D_MODEL:])
    merged = ga_ref[...].astype(F32) * ya + gb_ref[...].astype(F32) * yb
    x2 = x1_ref[...] + _dot(merged.astype(BF16), wout_ref[...])
    x2_ref[...] = x2
    hx = _rms(x2, gx_ref[...]).astype(BF16)
    qx = (_dot(hx, wxq_ref[...]) * (X_HEAD_DIM ** -0.5)).astype(BF16)
    if shared_mem:
        out_ref[...] = _xattn_heads(qx, mk_ref[...].astype(BF16), mv_ref[...].astype(BF16)).astype(BF16)
    else:
        out_ref[...] = qx


def _merge(x1, o, z, ga, gb, wup, wglu, wout, gx, wxq, *, tm, mem=None):
    t = x1.shape[0]
    row = lambda w: pl.BlockSpec((tm, w), lambda i: (i, 0))
    consts = [wup, wglu, wout, gx, wxq] + (list(mem) if mem is not None else [])
    return pl.pallas_call(
        functools.partial(_merge_kernel, shared_mem=mem is not None),
        grid=(t // tm,),
        in_specs=[row(D_MODEL), row(ATTN_WIDTH), row(SSM_WIDTH), row(D_MODEL), row(D_MODEL)]
                 + [_const_spec(c.shape) for c in consts],
        out_specs=[row(D_MODEL), row(D_MODEL)],
        out_shape=[jax.ShapeDtypeStruct((t, D_MODEL), F32), jax.ShapeDtypeStruct((t, D_MODEL), BF16)],
        compiler_params=_cparams(("parallel",)),
        name="merge",
    )(x1, o, z, ga, gb, *consts)


def _memkv_kernel(mem_ref, g_ref, wk_ref, wv_ref, mk_ref, mv_ref):
    mn = _rms(mem_ref[...], g_ref[...]).astype(BF16)
    mk_ref[...] = _dot(mn, wk_ref[...])
    mv_ref[...] = _dot(mn, wv_ref[...])


def _memkv(mem, g, wk, wv):
    sds = jax.ShapeDtypeStruct((mem.shape[0], D_MODEL), F32)
    return pl.pallas_call(_memkv_kernel, out_shape=[sds, sds], name="memkv",
                          compiler_params=pltpu.CompilerParams(vmem_limit_bytes=VMEM_LIMIT))(mem, g, wk, wv)


def _xattn_sample_kernel(q_ref, mk_ref, mv_ref, o_ref, *, nb, ts):
    qrows = ts * N_X_HEADS
    krows = N_MEM * N_X_HEADS
    row = lax.broadcasted_iota(jnp.int32, (qrows, krows), 0)
    col = lax.broadcasted_iota(jnp.int32, (qrows, krows), 1)
    same_head = (row // ts) == (col & (N_X_HEADS - 1))
    for b in range(nb):
        qb = q_ref[b * ts:(b + 1) * ts, :]
        qs = jnp.concatenate([qb[:, h * X_HEAD_DIM:(h + 1) * X_HEAD_DIM] for h in range(N_X_HEADS)], axis=0)
        kf = mk_ref[b].reshape(krows, X_HEAD_DIM).astype(BF16)
        vf = mv_ref[b].reshape(krows, X_HEAD_DIM).astype(BF16)
        s = jnp.where(same_head, _dot_nt(qs, kf), NEG_INF)
        p = jnp.exp(s - jnp.max(s, axis=-1, keepdims=True))
        o = (_dot(p.astype(BF16), vf) / jnp.sum(p, axis=-1, keepdims=True)).astype(BF16)
        for h in range(N_X_HEADS):
            o_ref[b * ts:(b + 1) * ts, h * X_HEAD_DIM:(h + 1) * X_HEAD_DIM] = o[h * ts:(h + 1) * ts]


def _xattn_sample(qx, mk, mv, *, nb, ts):
    nseq = mk.shape[0]
    row = pl.BlockSpec((nb * ts, D_MODEL), lambda i: (i, 0))
    mem = pl.BlockSpec((nb, N_MEM, N_X_HEADS, X_HEAD_DIM), lambda i: (i, 0, 0, 0))
    return pl.pallas_call(
        functools.partial(_xattn_sample_kernel, nb=nb, ts=ts),
        grid=(nseq // nb,),
        in_specs=[row, mem, mem],
        out_specs=row,
        out_shape=jax.ShapeDtypeStruct((nseq * ts, D_MODEL), BF16),
        compiler_params=_cparams(("parallel",)),
        name="xattn_sample",
    )(qx, mk, mv)


def kernel(x_prompt, x_sample, cache_win_k, cache_win_v, state_ssm_re, state_ssm_im, cache_mem_k, cache_mem_v, mem_prompt, g_ffn1, w_ffn1_in, w_ffn1_out, g_mix, w_in, attn_sinks, ssm_a_re, ssm_a_im, ssm_log_dt, ssm_b_re, ssm_b_im, ssm_c_re, ssm_c_im, ssm_d, w_attn_up, w_ssm_glu, w_out, g_xattn, g_mem, w_xq, w_xk, w_xv, w_xo, g_ffn2, w_ffn2_in, w_ffn2_out, g_final):
    assert x_prompt.shape[0] == 1 and g_ffn1.shape[0] == 1
    seq = x_prompt.shape[1]
    nseq, ts = x_sample.shape[0], x_sample.shape[1]
    past_len = seq
    l = 0
    bf = lambda w: w[l].astype(BF16)
    vec = lambda g: g[l].reshape(1, -1)
    w1i, w1o, w2i, w2o = bf(w_ffn1_in), bf(w_ffn1_out), bf(w_ffn2_in), bf(w_ffn2_out)
    win, wup, wglu, wout = bf(w_in), bf(w_attn_up), bf(w_ssm_glu), bf(w_out)
    wxq, wxk, wxv, wxo = bf(w_xq), bf(w_xk), bf(w_xv), bf(w_xo)
    gf = g_final.reshape(1, -1)
    sinks = attn_sinks[l]

    lam_re, lam_im, bbt_re, bbt_im = _ssm_prep(ssm_a_re[l], ssm_a_im[l], ssm_log_dt[l], ssm_b_re[l], ssm_b_im[l])
    bb = jnp.concatenate([_block_diag_halves(bbt_re), _block_diag_halves(bbt_im)], axis=-1).astype(BF16)
    ct_re = jnp.swapaxes(ssm_c_re[l], 1, 2)
    ct_im = jnp.swapaxes(ssm_c_im[l], 1, 2)
    cc = jnp.concatenate([_block_diag_halves(ct_re), -_block_diag_halves(ct_im)], axis=1).astype(BF16)
    d_skip = ssm_d[l].reshape(1, -1)
    lre16 = lam_re.reshape(2, SUBLANES, LANES)
    lim16 = lam_im.reshape(2, SUBLANES, LANES)
    lam_tm = jnp.stack([lre16[0], lim16[0], lre16[1], lim16[1]])
    lre_row = lam_re.reshape(1, SSM_CH)
    lim_row = lam_im.reshape(1, SSM_CH)

    mk_p, mv_p = _memkv(mem_prompt[0], vec(g_mem), wxk, wxv)

    def group(x, rope_tabs, attn_fn, ssm_fn, shared_mem, xattn_fn):
        x1 = _ffn(x, vec(g_ffn1), w1i, w1o, gf, tm=TM_FFN)
        q, k, v, u, ga, gb = _inproj(x1, vec(g_mix), win, rope_tabs, tm=TM_MIX)
        o, extra_attn = attn_fn(q, k, v)
        z, extra_ssm = ssm_fn(u)
        x2, ox = _merge(x1, o, z, ga, gb, wup, wglu, wout, vec(g_xattn), wxq, tm=TM_MIX, mem=shared_mem)
        if shared_mem is None:
            ox = xattn_fn(ox)
        y = _ffn(x2, vec(g_ffn2), w2i, w2o, gf, tm=TM_FFN, pre=(ox, wxo), final=True)
        return y, extra_attn, extra_ssm

    tm = TM_MIX
    assert tm % ts == 0
    rope_p = _rope_tables(jnp.arange(seq // tm) * tm, jnp.arange(tm))

    def attn_p(q, k, v):
        return _attn_prompt(sinks, q, k, v), (k[-WINDOW:], v[-WINDOW:])

    def ssm_p(u):
        z, hout = _ssm_prompt(u, bb, cc, lam_tm, d_skip, jnp.zeros((4, SUBLANES, LANES), F32))
        return z, hout

    y_p, (nk_p, nv_p), hout_p = group(x_prompt[0], rope_p, attn_p, ssm_p, (mk_p, mv_p), None)
    hre_p = jnp.concatenate([hout_p[0], hout_p[2]], axis=0)
    him_p = jnp.concatenate([hout_p[1], hout_p[3]], axis=0)

    rope_s = _rope_tables(jnp.full((nseq * ts // tm,), past_len), jnp.arange(tm) % ts)

    def attn_s(q, k, v):
        to_t = lambda c: jnp.transpose(c[l], (0, 2, 3, 1))
        o, nkt, nvt = _attn_sample(sinks, q, k, v, to_t(cache_win_k), to_t(cache_win_v), nb=8, ts=ts)
        return o, (jnp.transpose(nkt, (0, 3, 1, 2)), jnp.transpose(nvt, (0, 3, 1, 2)))

    def ssm_s(u):
        z, hre, him = _ssm_sample(u, bb, cc, lre_row, lim_row, d_skip, state_ssm_re[l].reshape(nseq, SSM_CH),
                                  state_ssm_im[l].reshape(nseq, SSM_CH), nb=64, ts=ts)
        return z, (hre, him)

    y_s, (nk_s, nv_s), (hre_s, him_s) = group(
        x_sample.reshape(nseq * ts, D_MODEL), rope_s, attn_s, ssm_s, None,
        lambda qx: _xattn_sample(qx, cache_mem_k[l], cache_mem_v[l], nb=4, ts=ts))

    kvshape = (1, 1, WINDOW, N_KV_HEADS, HEAD_DIM)
    stshape = (1, 1, N_SSM_GROUPS, SSM_STATE)
    memshape = (1, 1, N_MEM, N_X_HEADS, X_HEAD_DIM)
    return (y_p.reshape(1, seq, D_MODEL), y_s.reshape(nseq, ts, D_MODEL),
            nk_p.reshape(kvshape), nv_p.reshape(kvshape), hre_p.reshape(stshape), him_p.reshape(stshape),
            mk_p.reshape(memshape), mv_p.reshape(memshape),
            nk_s.reshape(1, nseq, WINDOW, N_KV_HEADS, HEAD_DIM), nv_s.reshape(1, nseq, WINDOW, N_KV_HEADS, HEAD_DIM),
            hre_s.reshape(1, nseq, N_SSM_GROUPS, SSM_STATE), him_s.reshape(1, nseq, N_SSM_GROUPS, SSM_STATE))
```

```python
import functools
import math

import jax
import jax.numpy as jnp
from jax import lax
from jax.experimental import pallas as pl
from jax.experimental.pallas import tpu as pltpu

F32 = jnp.float32
BF16 = jnp.bfloat16

D_MODEL = 1024
N_Q_HEADS = 8
N_KV_HEADS = 2
HEAD_DIM = 64
Q_PER_KV = N_Q_HEADS // N_KV_HEADS
ATTN_WIDTH = N_Q_HEADS * HEAD_DIM
KV_WIDTH = N_KV_HEADS * HEAD_DIM
WINDOW = 128
ROPE_THETA = 10000.0
SSM_WIDTH = D_MODEL // 2
SSM_GROUP = 16
N_SSM_GROUPS = SSM_WIDTH // SSM_GROUP
SSM_STATE = 64
N_MEM = 256
N_X_HEADS = 4
X_HEAD_DIM = D_MODEL // N_X_HEADS
D_FF = 2816
RMS_EPS = 1e-6
NEG_INF = -1e30
IN_SPLITS = (ATTN_WIDTH, KV_WIDTH, KV_WIDTH, SSM_WIDTH, D_MODEL, D_MODEL)
IN_WIDTH = sum(IN_SPLITS)
OFF_Q, OFF_K, OFF_V, OFF_U, OFF_GA, OFF_GB = (0, 512, 640, 768, 1280, 2304)

LANES = 128
SUBLANES = 8
VMEM_LIMIT = 60 * 1024 * 1024

SSM_CH = N_SSM_GROUPS * SSM_STATE
HALF_CH = SSM_CH // 2
HALF_U = SSM_WIDTH // 2
BLK_PER_HALF = 2 * HALF_CH // LANES
TM_FFN = 1024
TM_TAIL = 512
FFN_CHUNKS = ((0, 768), (768, 1536), (1536, 2304), (2304, D_FF))
TM_MIX = 1024
ATTN_QBLOCKS = 8
SSM_TB = 256
SSM_NB = 2
SSM_PITCH = SSM_TB + 4


def _cparams(sem):
    return pltpu.CompilerParams(dimension_semantics=sem, vmem_limit_bytes=VMEM_LIMIT)


def _const_spec(shape):
    nd = len(shape)
    return pl.BlockSpec(shape, lambda *_: (0,) * nd, pipeline_mode=pl.Buffered(1))


def _rms(x, g):
    return x * lax.rsqrt(jnp.mean(x * x, axis=-1, keepdims=True) + RMS_EPS) * g


def _dot(a, b):
    return jnp.dot(a, b, preferred_element_type=F32)


def _dot_nt(a, b):
    return lax.dot_general(a, b, (((1,), (1,)), ((), ())), preferred_element_type=F32)


def _row_spec(tm, width):
    return pl.BlockSpec((tm, width), lambda i: (i, 0))


def _two_array_specs(n_first, tm, width):
    return [pl.BlockSpec((tm, width), lambda i: (jnp.minimum(i, n_first - 1), 0)),
            pl.BlockSpec((tm, width), lambda i: (jnp.maximum(i - n_first, 0), 0))]


def _store_by_step(is_first, first_ref, second_ref, value):
    @pl.when(is_first)
    def _():
        first_ref[...] = value

    @pl.when(jnp.logical_not(is_first))
    def _():
        second_ref[...] = value


def _ffn_residual(x, g_ref, wi_ref, wo_ref):
    h = _rms(x, g_ref[...]).astype(BF16)
    acc = None
    for lo, hi in FFN_CHUNKS:
        a = _dot(h, wi_ref[:, lo:hi])
        b = _dot(h, wi_ref[:, D_FF + lo:D_FF + hi])
        part = _dot((a * jax.nn.sigmoid(a) * b).astype(BF16), wo_ref[lo:hi, :])
        acc = part if acc is None else acc + part
    return x + 0.5 * acc


def _ffn_first_kernel(xp_ref, xs_ref, g_ref, wi_ref, wo_ref, o_ref, *, n_first):
    x = jnp.where(pl.program_id(0) < n_first, xp_ref[...], xs_ref[...])
    o_ref[...] = _ffn_residual(x, g_ref, wi_ref, wo_ref)


def _ffn_first(xp, xs, g, wi, wo, *, tm):
    n_first, n_second = xp.shape[0] // tm, xs.shape[0] // tm
    return pl.pallas_call(
        functools.partial(_ffn_first_kernel, n_first=n_first),
        grid=(n_first + n_second,),
        in_specs=_two_array_specs(n_first, tm, D_MODEL)
                 + [_const_spec(g.shape), _const_spec(wi.shape), _const_spec(wo.shape)],
        out_specs=_row_spec(tm, D_MODEL),
        out_shape=jax.ShapeDtypeStruct((xp.shape[0] + xs.shape[0], D_MODEL), F32),
        compiler_params=_cparams(("parallel",)),
        name="ffn",
    )(xp, xs, g, wi, wo)


def _tail_kernel(x_ref, oxp_ref, oxs_ref, wxo_ref, g_ref, wi_ref, wo_ref, gf_ref, yp_ref, ys_ref, *, n_first):
    is_first = pl.program_id(0) < n_first
    ox = jnp.where(is_first, oxp_ref[...], oxs_ref[...])
    x = x_ref[...] + _dot(ox, wxo_ref[...])
    y = _rms(_ffn_residual(x, g_ref, wi_ref, wo_ref), gf_ref[...])
    _store_by_step(is_first, yp_ref, ys_ref, y)


def _tail(x2, ox_p, ox_s, wxo, g, wi, wo, gf, *, tm):
    n_first, n_second = ox_p.shape[0] // tm, ox_s.shape[0] // tm
    two = _two_array_specs(n_first, tm, D_MODEL)
    return pl.pallas_call(
        functools.partial(_tail_kernel, n_first=n_first),
        grid=(n_first + n_second,),
        in_specs=[_row_spec(tm, D_MODEL)] + two
                 + [_const_spec(c.shape) for c in (wxo, g, wi, wo, gf)],
        out_specs=two,
        out_shape=[jax.ShapeDtypeStruct((ox_p.shape[0], D_MODEL), F32),
                   jax.ShapeDtypeStruct((ox_s.shape[0], D_MODEL), F32)],
        compiler_params=_cparams(("arbitrary",)),
        name="tail",
    )(x2, ox_p, ox_s, wxo, g, wi, wo, gf)


def _inproj_kernel(x_ref, g_ref, w_ref, ca_ref, sa_ref, cb_ref, sb_ref, q_ref, k_ref, v_ref, u_ref, ga_ref, gb_ref):
    h = _rms(x_ref[...], g_ref[...]).astype(BF16)
    ca, sa, cb, sb = ca_ref[0], sa_ref[0], cb_ref[0], sb_ref[0]
    cos = ca * cb - sa * sb
    sin = sa * cb + ca * sb
    lane = lax.broadcasted_iota(jnp.int32, cos.shape, 1)
    first_half = (lane & (HEAD_DIM - 1)) < (HEAD_DIM // 2)

    def rope(xc):
        rot = jnp.where(first_half, pltpu.roll(xc, LANES - HEAD_DIM // 2, 1), pltpu.roll(xc, HEAD_DIM // 2, 1))
        return xc * cos + rot * sin

    yq = _dot(h, w_ref[:, OFF_Q:OFF_K])
    for c in range(ATTN_WIDTH // LANES):
        q_ref[:, c * LANES:(c + 1) * LANES] = (rope(yq[:, c * LANES:(c + 1) * LANES]) * (HEAD_DIM ** -0.5)).astype(BF16)
    ykvu = _dot(h, w_ref[:, OFF_K:OFF_GA])
    k_ref[...] = rope(ykvu[:, :KV_WIDTH])
    v_ref[...] = ykvu[:, KV_WIDTH:2 * KV_WIDTH]
    u_ref[...] = ykvu[:, 2 * KV_WIDTH:]
    ga_ref[...] = jax.nn.sigmoid(_dot(h, w_ref[:, OFF_GA:OFF_GB])).astype(BF16)
    gb_ref[...] = jax.nn.sigmoid(_dot(h, w_ref[:, OFF_GB:])).astype(BF16)


def _inproj(x, g, w, rope_tabs, *, tm, n_first):
    t = x.shape[0]
    row = lambda wd: _row_spec(tm, wd)
    step_tab = pl.BlockSpec((1, 1, LANES), lambda i: (i, 0, 0))
    row_tab = pl.BlockSpec((1, tm, LANES), lambda i: (jnp.minimum(i // n_first, 1), 0, 0))
    widths = (ATTN_WIDTH, KV_WIDTH, KV_WIDTH, SSM_WIDTH, D_MODEL, D_MODEL)
    dtypes = (BF16, F32, F32, F32, BF16, BF16)
    return pl.pallas_call(
        _inproj_kernel,
        grid=(t // tm,),
        in_specs=[row(D_MODEL), _const_spec((1, D_MODEL)), _const_spec((D_MODEL, IN_WIDTH)), step_tab, step_tab,
                  row_tab, row_tab],
        out_specs=[row(wd) for wd in widths],
        out_shape=[jax.ShapeDtypeStruct((t, wd), dt) for wd, dt in zip(widths, dtypes)],
        compiler_params=_cparams(("parallel",)),
        name="inproj",
    )(x, g, w, *rope_tabs)


def _rope_tables(step_pos, row_pos):
    half = HEAD_DIM // 2
    inv = ROPE_THETA ** (-jnp.arange(half, dtype=F32) / half)
    sign = jnp.concatenate([-jnp.ones((half,), F32), jnp.ones((half,), F32)])

    def tabs(pos):
        ang = pos.astype(F32)[..., None] * inv
        cos = jnp.tile(jnp.cos(ang), LANES // half)
        sin = jnp.tile(jnp.tile(jnp.sin(ang), 2) * sign, LANES // HEAD_DIM)
        return cos, sin

    ca, sa = tabs(step_pos[:, None])
    cb, sb = tabs(row_pos)
    return ca, sa, cb, sb


def _attn_prompt_kernel(sinks_ref, q_ref, kp_ref, kc_ref, vp_ref, vc_ref, o_ref):
    i = pl.program_id(0)
    r = lax.broadcasted_iota(jnp.int32, (WINDOW, 2 * WINDOW), 0)
    c = lax.broadcasted_iota(jnp.int32, (WINDOW, 2 * WINDOW), 1)
    band = (c >= r) & (c <= WINDOW + r)
    first_mask = band & ((c >= WINDOW) | (i > 0))
    kcat = jnp.concatenate([kp_ref[...], kc_ref[...]], axis=0)
    vcat = jnp.concatenate([vp_ref[...], vc_ref[...]], axis=0)
    kswap = pltpu.roll(kcat, HEAD_DIM, 1)
    vswap = pltpu.roll(vcat, HEAD_DIM, 1)
    lo = lax.broadcasted_iota(jnp.int32, kcat.shape, 1) < HEAD_DIM
    lo_q = lax.broadcasted_iota(jnp.int32, (WINDOW, LANES), 1) < HEAD_DIM
    ones = jnp.ones(kcat.shape, F32)
    for kv in range(N_KV_HEADS):
        k_own, k_other = (kcat, kswap) if kv == 0 else (kswap, kcat)
        v_own, v_other = (vcat, vswap) if kv == 0 else (vswap, vcat)
        k_half = (jnp.where(lo, k_own, 0.0).astype(BF16), jnp.where(lo, 0.0, k_other).astype(BF16))
        rhs_half = (jnp.concatenate([jnp.where(lo, v_own, 0.0), jnp.where(lo, ones, 0.0)], axis=1).astype(BF16),
                    jnp.concatenate([jnp.where(lo, 0.0, v_other), jnp.where(lo, 0.0, ones)], axis=1).astype(BF16))
        for j in range(ATTN_QBLOCKS):
            qrows = slice(j * WINDOW, (j + 1) * WINDOW)
            krows = slice(j * WINDOW, (j + 2) * WINDOW)
            mask = first_mask if j == 0 else band
            for c in range(Q_PER_KV // 2):
                tile = kv * (Q_PER_KV // 2) + c
                qt = q_ref[qrows, tile * LANES:(tile + 1) * LANES]
                acc = None
                sink_terms = []
                for side in range(2):
                    s = jnp.where(mask, _dot_nt(qt, k_half[side][krows]), NEG_INF)
                    sink = sinks_ref[2 * tile + side]
                    m = jnp.maximum(jnp.max(s, axis=-1, keepdims=True), sink)
                    pv = _dot(jnp.exp(s - m).astype(BF16), rhs_half[side][krows])
                    acc = pv if acc is None else acc + pv
                    sink_terms.append(jnp.exp(sink - m))
                denom = acc[:, LANES:] + jnp.where(lo_q, sink_terms[0], sink_terms[1])
                o_ref[qrows, tile * LANES:(tile + 1) * LANES] = (acc[:, :LANES] / denom).astype(BF16)


def _attn_prompt(sinks, q, k, v, *, t):
    rows = ATTN_QBLOCKS * WINDOW
    cur = lambda w: pl.BlockSpec((rows, w), lambda i: (i, 0))
    prev = lambda w: pl.BlockSpec((WINDOW, w), lambda i: (jnp.maximum(i * ATTN_QBLOCKS - 1, 0), 0))
    return pl.pallas_call(
        _attn_prompt_kernel,
        grid=(t // rows,),
        in_specs=[pl.BlockSpec(memory_space=pltpu.SMEM), cur(ATTN_WIDTH), prev(KV_WIDTH), cur(KV_WIDTH),
                  prev(KV_WIDTH), cur(KV_WIDTH)],
        out_specs=cur(ATTN_WIDTH),
        out_shape=jax.ShapeDtypeStruct((t, ATTN_WIDTH), BF16),
        compiler_params=_cparams(("parallel",)),
        name="attn_prompt",
    )(sinks, q, k, k, v, v)


def _attn_sample_kernel(sinks_ref, q_ref, k_ref, v_ref, ckt_ref, cvt_ref, o_ref, nkt_ref, nvt_ref, *, nb, ts):
    rows = nb * ts
    q = q_ref[...]
    pad = jnp.zeros((LANES - rows, KV_WIDTH), F32)
    knt = jnp.concatenate([k_ref[...], pad], axis=0).T
    vnt = jnp.concatenate([v_ref[...], pad], axis=0).T
    lane = lax.broadcasted_iota(jnp.int32, (HEAD_DIM, WINDOW), 1)
    is_new = lane >= WINDOW - ts
    for b in range(nb):
        knt_b = pltpu.roll(knt, (WINDOW - ts - b * ts) % LANES, 1)
        vnt_b = pltpu.roll(vnt, (WINDOW - ts - b * ts) % LANES, 1)
        for kv in range(N_KV_HEADS):
            ksl = slice(kv * HEAD_DIM, (kv + 1) * HEAD_DIM)
            nkt_ref[b, kv] = jnp.where(is_new, knt_b[ksl], pltpu.roll(ckt_ref[b, kv], WINDOW - ts, 1))
            nvt_ref[b, kv] = jnp.where(is_new, vnt_b[ksl], pltpu.roll(cvt_ref[b, kv], WINDOW - ts, 1))

    knt = knt[:, :rows].astype(BF16)
    vnt = vnt[:, :rows].astype(BF16)
    grows = Q_PER_KV * rows
    row1 = lax.broadcasted_iota(jnp.int32, (grows, nb * WINDOW), 0)
    col1 = lax.broadcasted_iota(jnp.int32, (grows, nb * WINDOW), 1)
    rseq1 = (row1 % rows) // ts
    mask_ctx = (rseq1 == col1 // WINDOW) & (col1 % WINDOW >= row1 % ts)
    row2 = lax.broadcasted_iota(jnp.int32, (grows, rows), 0)
    col2 = lax.broadcasted_iota(jnp.int32, (grows, rows), 1)
    mask_new = ((row2 % rows) // ts == col2 // ts) & (col2 % ts <= row2 % ts)
    ghead = lax.broadcasted_iota(jnp.int32, (grows, 1), 0) // rows
    for kv in range(N_KV_HEADS):
        ksl = slice(kv * HEAD_DIM, (kv + 1) * HEAD_DIM)
        qg = jnp.concatenate([q[:, (kv * Q_PER_KV + g) * HEAD_DIM:(kv * Q_PER_KV + g + 1) * HEAD_DIM]
                              for g in range(Q_PER_KV)], axis=0)
        kctx = jnp.concatenate([ckt_ref[b, kv] for b in range(nb)], axis=1).astype(BF16)
        vctx = jnp.concatenate([cvt_ref[b, kv] for b in range(nb)], axis=1).astype(BF16)
        s1 = jnp.where(mask_ctx, _dot(qg, kctx), NEG_INF)
        s2 = jnp.where(mask_new, _dot(qg, knt[ksl]), NEG_INF)
        sink = jnp.zeros((grows, 1), F32)
        for g in range(Q_PER_KV):
            sink = jnp.where(ghead == g, sinks_ref[kv * Q_PER_KV + g], sink)
        m = jnp.maximum(jnp.maximum(jnp.max(s1, axis=-1, keepdims=True), jnp.max(s2, axis=-1, keepdims=True)), sink)
        p1 = jnp.exp(s1 - m)
        p2 = jnp.exp(s2 - m)
        denom = jnp.sum(p1, axis=-1, keepdims=True) + jnp.sum(p2, axis=-1, keepdims=True) + jnp.exp(sink - m)
        o = (_dot_nt(p1.astype(BF16), vctx) + _dot_nt(p2.astype(BF16), vnt[ksl])) / denom
        for g in range(Q_PER_KV):
            h = kv * Q_PER_KV + g
            o_ref[:, h * HEAD_DIM:(h + 1) * HEAD_DIM] = o[g * rows:(g + 1) * rows].astype(BF16)


def _attn_sample(sinks, q, k, v, ckt, cvt, *, nb, ts, row0):
    nseq = ckt.shape[0]
    rows = nb * ts
    assert rows <= LANES and row0 % rows == 0
    row_in = lambda w: pl.BlockSpec((rows, w), lambda i: (i + row0 // rows, 0))
    win = pl.BlockSpec((nb, N_KV_HEADS, HEAD_DIM, WINDOW), lambda i: (i, 0, 0, 0))
    win_shape = jax.ShapeDtypeStruct((nseq, N_KV_HEADS, HEAD_DIM, WINDOW), F32)
    return pl.pallas_call(
        functools.partial(_attn_sample_kernel, nb=nb, ts=ts),
        grid=(nseq // nb,),
        in_specs=[pl.BlockSpec(memory_space=pltpu.SMEM), row_in(ATTN_WIDTH), row_in(KV_WIDTH), row_in(KV_WIDTH),
                  win, win],
        out_specs=[_row_spec(rows, ATTN_WIDTH), win, win],
        out_shape=[jax.ShapeDtypeStruct((nseq * ts, ATTN_WIDTH), BF16), win_shape, win_shape],
        compiler_params=_cparams(("parallel",)),
        name="attn_sample",
    )(sinks, q, k, v, ckt, cvt)


def _ssm_prep_kernel(are_ref, aim_ref, ldt_ref, bre_ref, bim_ref, lre_ref, lim_ref, bbre_ref, bbim_ref):
    a_re = are_ref[...]
    a_im = aim_ref[...]
    dt = jnp.exp(ldt_ref[...])
    mag = jnp.exp(a_re * dt)
    lb_re = mag * jnp.cos(a_im * dt)
    lb_im = mag * jnp.sin(a_im * dt)
    den = a_re * a_re + a_im * a_im
    nr = lb_re - 1.0
    ni = lb_im
    k_re = (nr * a_re + ni * a_im) / den
    k_im = (ni * a_re - nr * a_im) / den
    lre_ref[...] = lb_re
    lim_ref[...] = lb_im
    b_re = bre_ref[...]
    b_im = bim_ref[...]
    bbre_ref[...] = k_re[:, None, :] * b_re - k_im[:, None, :] * b_im
    bbim_ref[...] = k_re[:, None, :] * b_im + k_im[:, None, :] * b_re


def _ssm_prep(a_re, a_im, log_dt, b_re, b_im):
    g, n, gs = b_re.shape
    sds = jax.ShapeDtypeStruct
    return pl.pallas_call(
        _ssm_prep_kernel,
        out_shape=[sds((g, n), F32), sds((g, n), F32), sds((g, gs, n), F32), sds((g, gs, n), F32)],
        name="ssm_prep",
    )(a_re, a_im, log_dt.reshape(g, 1), jnp.swapaxes(b_re, 1, 2), jnp.swapaxes(b_im, 1, 2))


def _block_diag_halves(m):
    g, a, b = m.shape
    gh = g // 2
    tiled = jnp.tile(m.reshape(2, gh * a, b), (1, 1, gh))
    row_group = lax.broadcasted_iota(jnp.int32, tiled.shape, 1) // a
    col_group = lax.broadcasted_iota(jnp.int32, tiled.shape, 2) // b
    return jnp.where(row_group == col_group, tiled, 0.0)


def _ssm_y(u, scan_block, cc_ref, d_ref):
    ys = []
    for half in range(2):
        hc = jnp.concatenate([scan_block(half, jj) for jj in range(BLK_PER_HALF)], axis=1).astype(BF16)
        ys.append(_dot(hc, cc_ref[half]))
    y = jnp.concatenate(ys, axis=1) + d_ref[...] * u
    return jax.nn.gelu(y).astype(BF16)


def _ssm_prompt_kernel(u_ref, bb_ref, cc_ref, lam_ref, d_ref, h0_ref, z_ref, hout_ref,
                       x0_ref, x1_ref, s0_ref, s1_ref, hst_ref):
    tb, pitch = SSM_TB, SSM_PITCH
    x_bufs = (x0_ref, x1_ref)
    s_bufs = (s0_ref, s1_ref)

    @pl.when(pl.program_id(0) == 0)
    def _():
        hst_ref[...] = h0_ref[...]

    def b_proj(k):
        ub = u_ref[k * tb:(k + 1) * tb, :].astype(BF16)
        for half in range(2):
            x = _dot(ub[:, half * HALF_U:(half + 1) * HALF_U], bb_ref[half])
            for jj in range(BLK_PER_HALF):
                x_bufs[k % 2][pl.ds((half * BLK_PER_HALF + jj) * pitch, tb), :] = x[:, jj * LANES:(jj + 1) * LANES]

    lam = [lam_ref[g] for g in range(4)]
    carry = [hst_ref[g] for g in range(4)]

    def scan(k):
        for t in range(tb):
            for half in range(2):
                hr, hi = carry[2 * half], carry[2 * half + 1]
                lr, li = lam[2 * half], lam[2 * half + 1]
                rows_re = pl.ds((half * BLK_PER_HALF) * pitch + t, SUBLANES, stride=pitch)
                rows_im = pl.ds((half * BLK_PER_HALF + SUBLANES) * pitch + t, SUBLANES, stride=pitch)
                nr = lr * hr - li * hi + x_bufs[k % 2][rows_re, :]
                ni = lr * hi + li * hr + x_bufs[k % 2][rows_im, :]
                s_bufs[k % 2][rows_re, :] = nr
                s_bufs[k % 2][rows_im, :] = ni
                carry[2 * half], carry[2 * half + 1] = nr, ni

    def c_proj(k):
        z_ref[k * tb:(k + 1) * tb, :] = _ssm_y(
            u_ref[k * tb:(k + 1) * tb, :],
            lambda half, jj: s_bufs[k % 2][pl.ds((half * BLK_PER_HALF + jj) * pitch, tb), :], cc_ref, d_ref)

    b_proj(0)
    for k in range(SSM_NB):
        if k + 1 < SSM_NB:
            b_proj(k + 1)
        scan(k)
        c_proj(k)
    for g in range(4):
        hst_ref[g] = carry[g]
        hout_ref[g] = carry[g]


def _ssm_prompt(u, bb, cc, lam, d, h0, *, t):
    rows = SSM_NB * SSM_TB
    return pl.pallas_call(
        _ssm_prompt_kernel,
        grid=(t // rows,),
        in_specs=[pl.BlockSpec((rows, SSM_WIDTH), lambda i: (i, 0)), _const_spec(bb.shape), _const_spec(cc.shape),
                  _const_spec(lam.shape), _const_spec(d.shape), _const_spec(h0.shape)],
        out_specs=[pl.BlockSpec((rows, SSM_WIDTH), lambda i: (i, 0)),
                   pl.BlockSpec((4, SUBLANES, LANES), lambda i: (0, 0, 0))],
        out_shape=[jax.ShapeDtypeStruct((t, SSM_WIDTH), BF16), jax.ShapeDtypeStruct((4, SUBLANES, LANES), F32)],
        scratch_shapes=[pltpu.VMEM((2 * BLK_PER_HALF * SSM_PITCH, LANES), F32)] * 4
                       + [pltpu.VMEM((4, SUBLANES, LANES), F32)],
        compiler_params=_cparams(("arbitrary",)),
        name="ssm_prompt",
    )(u, bb, cc, lam, d, h0)


def _ssm_sample_kernel(u_ref, bb_ref, cc_ref, lre_ref, lim_ref, d_ref, h0re_ref, h0im_ref,
                       z_ref, hre_ref, him_ref, xs_ref, *, nb, ts):
    u = u_ref[...]
    ub = u.astype(BF16)
    for half in range(2):
        x = _dot(ub[:, half * HALF_U:(half + 1) * HALF_U], bb_ref[half])
        for jj in range(BLK_PER_HALF):
            xs_ref[half * BLK_PER_HALF + jj] = x[:, jj * LANES:(jj + 1) * LANES]
    for cb in range(SSM_CH // LANES):
        half, jj = divmod(cb, SUBLANES)
        j_re = half * BLK_PER_HALF + jj
        j_im = j_re + SUBLANES
        csl = slice(cb * LANES, (cb + 1) * LANES)
        lr = lre_ref[:, csl]
        li = lim_ref[:, csl]
        hr = h0re_ref[:, csl]
        hi = h0im_ref[:, csl]
        for t in range(ts):
            rows = pl.ds(t, nb, stride=ts)
            nr = lr * hr - li * hi + xs_ref[j_re, rows, :]
            ni = lr * hi + li * hr + xs_ref[j_im, rows, :]
            xs_ref[j_re, rows, :] = nr
            xs_ref[j_im, rows, :] = ni
            hr, hi = nr, ni
        hre_ref[:, csl] = hr
        him_ref[:, csl] = hi
    z_ref[...] = _ssm_y(u, lambda half, jj: xs_ref[half * BLK_PER_HALF + jj], cc_ref, d_ref)


def _ssm_sample(u, bb, cc, lre, lim, d, h0re, h0im, *, nb, ts, row0):
    nseq = h0re.shape[0]
    rows = nb * ts
    assert row0 % rows == 0
    st = pl.BlockSpec((nb, SSM_CH), lambda i: (i, 0))
    return pl.pallas_call(
        functools.partial(_ssm_sample_kernel, nb=nb, ts=ts),
        grid=(nseq // nb,),
        in_specs=[pl.BlockSpec((rows, SSM_WIDTH), lambda i: (i + row0 // rows, 0)), _const_spec(bb.shape),
                  _const_spec(cc.shape),
                  _const_spec(lre.shape), _const_spec(lim.shape), _const_spec(d.shape), st, st],
        out_specs=[pl.BlockSpec((rows, SSM_WIDTH), lambda i: (i, 0)), st, st],
        out_shape=[jax.ShapeDtypeStruct((nseq * ts, SSM_WIDTH), BF16),
                   jax.ShapeDtypeStruct((nseq, SSM_CH), F32), jax.ShapeDtypeStruct((nseq, SSM_CH), F32)],
        scratch_shapes=[pltpu.VMEM((2 * BLK_PER_HALF, rows, LANES), F32)],
        compiler_params=_cparams(("parallel",)),
        name="ssm_sample",
    )(u, bb, cc, lre, lim, d, h0re, h0im)


def _xattn_heads(q, mk, mv):
    outs = []
    for h in range(N_X_HEADS):
        sl = slice(h * X_HEAD_DIM, (h + 1) * X_HEAD_DIM)
        s = _dot_nt(q[:, sl], mk[:, sl])
        p = jnp.exp(s - jnp.max(s, axis=-1, keepdims=True))
        outs.append(_dot(p.astype(BF16), mv[:, sl]) / jnp.sum(p, axis=-1, keepdims=True))
    return jnp.concatenate(outs, axis=1)


def _merge_kernel(x1_ref, op_ref, os_ref, zp_ref, zs_ref, ga_ref, gb_ref, wup_ref, wglu_ref, wout_ref, gx_ref,
                  wxq_ref, mk_ref, mv_ref, x2_ref, oxp_ref, qxs_ref, *, n_first):
    is_first = pl.program_id(0) < n_first
    o = jnp.where(is_first, op_ref[...], os_ref[...])
    z = jnp.where(is_first, zp_ref[...], zs_ref[...])
    ya = _dot(o, wup_ref[...])
    glu = _dot(z, wglu_ref[...])
    yb = glu[:, :D_MODEL] * jax.nn.sigmoid(glu[:, D_MODEL:])
    merged = ga_ref[...].astype(F32) * ya + gb_ref[...].astype(F32) * yb
    x2 = x1_ref[...] + _dot(merged.astype(BF16), wout_ref[...])
    x2_ref[...] = x2
    hx = _rms(x2, gx_ref[...]).astype(BF16)
    qx = (_dot(hx, wxq_ref[...]) * (X_HEAD_DIM ** -0.5)).astype(BF16)
    ox = _xattn_heads(qx, mk_ref[...].astype(BF16), mv_ref[...].astype(BF16)).astype(BF16)

    @pl.when(is_first)
    def _():
        oxp_ref[...] = ox

    @pl.when(jnp.logical_not(is_first))
    def _():
        qxs_ref[...] = qx


def _merge(x1, o_p, o_s, z_p, z_s, ga, gb, wup, wglu, wout, gx, wxq, mk, mv, *, tm):
    n_first, n_second = o_p.shape[0] // tm, o_s.shape[0] // tm
    row = lambda w: _row_spec(tm, w)
    consts = [wup, wglu, wout, gx, wxq, mk, mv]
    return pl.pallas_call(
        functools.partial(_merge_kernel, n_first=n_first),
        grid=(n_first + n_second,),
        in_specs=[row(D_MODEL)] + _two_array_specs(n_first, tm, ATTN_WIDTH) + _two_array_specs(n_first, tm, SSM_WIDTH)
                 + [row(D_MODEL), row(D_MODEL)] + [_const_spec(c.shape) for c in consts],
        out_specs=[row(D_MODEL)] + _two_array_specs(n_first, tm, D_MODEL),
        out_shape=[jax.ShapeDtypeStruct(x1.shape, F32), jax.ShapeDtypeStruct((o_p.shape[0], D_MODEL), BF16),
                   jax.ShapeDtypeStruct((o_s.shape[0], D_MODEL), BF16)],
        compiler_params=_cparams(("arbitrary",)),
        name="merge",
    )(x1, o_p, o_s, z_p, z_s, ga, gb, *consts)


def _memkv_kernel(mem_ref, g_ref, wk_ref, wv_ref, mk_ref, mv_ref):
    mn = _rms(mem_ref[...], g_ref[...]).astype(BF16)
    mk_ref[...] = _dot(mn, wk_ref[...])
    mv_ref[...] = _dot(mn, wv_ref[...])


def _memkv(mem, g, wk, wv):
    sds = jax.ShapeDtypeStruct((mem.shape[0], D_MODEL), F32)
    return pl.pallas_call(_memkv_kernel, out_shape=[sds, sds], name="memkv",
                          compiler_params=pltpu.CompilerParams(vmem_limit_bytes=VMEM_LIMIT))(mem, g, wk, wv)


def _xattn_sample_kernel(q_ref, mk_ref, mv_ref, o_ref, *, nb, ts):
    qrows = ts * N_X_HEADS
    krows = N_MEM * N_X_HEADS
    row = lax.broadcasted_iota(jnp.int32, (qrows, krows), 0)
    col = lax.broadcasted_iota(jnp.int32, (qrows, krows), 1)
    same_head = (row // ts) == (col & (N_X_HEADS - 1))
    for b in range(nb):
        qb = q_ref[b * ts:(b + 1) * ts, :]
        qs = jnp.concatenate([qb[:, h * X_HEAD_DIM:(h + 1) * X_HEAD_DIM] for h in range(N_X_HEADS)], axis=0)
        kf = mk_ref[b].reshape(krows, X_HEAD_DIM).astype(BF16)
        vf = mv_ref[b].reshape(krows, X_HEAD_DIM).astype(BF16)
        s = jnp.where(same_head, _dot_nt(qs, kf), NEG_INF)
        p = jnp.exp(s - jnp.max(s, axis=-1, keepdims=True))
        o = (_dot(p.astype(BF16), vf) / jnp.sum(p, axis=-1, keepdims=True)).astype(BF16)
        for h in range(N_X_HEADS):
            o_ref[b * ts:(b + 1) * ts, h * X_HEAD_DIM:(h + 1) * X_HEAD_DIM] = o[h * ts:(h + 1) * ts]


def _xattn_sample(qx, mk, mv, *, nb, ts):
    nseq = mk.shape[0]
    row = pl.BlockSpec((nb * ts, D_MODEL), lambda i: (i, 0))
    mem = pl.BlockSpec((nb, N_MEM, N_X_HEADS, X_HEAD_DIM), lambda i: (i, 0, 0, 0))
    return pl.pallas_call(
        functools.partial(_xattn_sample_kernel, nb=nb, ts=ts),
        grid=(nseq // nb,),
        in_specs=[row, mem, mem],
        out_specs=row,
        out_shape=jax.ShapeDtypeStruct((nseq * ts, D_MODEL), BF16),
        compiler_params=_cparams(("parallel",)),
        name="xattn_sample",
    )(qx, mk, mv)


def kernel(x_prompt, x_sample, cache_win_k, cache_win_v, state_ssm_re, state_ssm_im, cache_mem_k, cache_mem_v, mem_prompt, g_ffn1, w_ffn1_in, w_ffn1_out, g_mix, w_in, attn_sinks, ssm_a_re, ssm_a_im, ssm_log_dt, ssm_b_re, ssm_b_im, ssm_c_re, ssm_c_im, ssm_d, w_attn_up, w_ssm_glu, w_out, g_xattn, g_mem, w_xq, w_xk, w_xv, w_xo, g_ffn2, w_ffn2_in, w_ffn2_out, g_final):
    assert x_prompt.shape[0] == 1 and g_ffn1.shape[0] == 1
    seq = x_prompt.shape[1]
    nseq, ts = x_sample.shape[0], x_sample.shape[1]
    past_len = seq
    l = 0
    bf = lambda w: w[l].astype(BF16)
    vec = lambda g: g[l].reshape(1, -1)
    w1i, w1o, w2i, w2o = bf(w_ffn1_in), bf(w_ffn1_out), bf(w_ffn2_in), bf(w_ffn2_out)
    win, wup, wglu, wout = bf(w_in), bf(w_attn_up), bf(w_ssm_glu), bf(w_out)
    wxq, wxk, wxv, wxo = bf(w_xq), bf(w_xk), bf(w_xv), bf(w_xo)
    gf = g_final.reshape(1, -1)
    sinks = attn_sinks[l]

    lam_re, lam_im, bbt_re, bbt_im = _ssm_prep(ssm_a_re[l], ssm_a_im[l], ssm_log_dt[l], ssm_b_re[l], ssm_b_im[l])
    bb = jnp.concatenate([_block_diag_halves(bbt_re), _block_diag_halves(bbt_im)], axis=-1).astype(BF16)
    ct_re = jnp.swapaxes(ssm_c_re[l], 1, 2)
    ct_im = jnp.swapaxes(ssm_c_im[l], 1, 2)
    cc = jnp.concatenate([_block_diag_halves(ct_re), -_block_diag_halves(ct_im)], axis=1).astype(BF16)
    d_skip = ssm_d[l].reshape(1, -1)
    lre16 = lam_re.reshape(2, SUBLANES, LANES)
    lim16 = lam_im.reshape(2, SUBLANES, LANES)
    lam_tm = jnp.stack([lre16[0], lim16[0], lre16[1], lim16[1]])
    lre_row = lam_re.reshape(1, SSM_CH)
    lim_row = lam_im.reshape(1, SSM_CH)

    mk_p, mv_p = _memkv(mem_prompt[0], vec(g_mem), wxk, wxv)

    tm = TM_MIX
    t_s = nseq * ts
    assert seq % tm == 0 and t_s % tm == 0 and tm % ts == 0
    n_p, n_s = seq // tm, t_s // tm
    rope_tabs = _rope_tables(jnp.concatenate([jnp.arange(n_p) * tm, jnp.full((n_s,), past_len)]),
                             jnp.stack([jnp.arange(tm), jnp.arange(tm) % ts]))

    x1 = _ffn_first(x_prompt[0], x_sample.reshape(t_s, D_MODEL), vec(g_ffn1), w1i, w1o, tm=TM_FFN)
    q, k, v, u, ga, gb = _inproj(x1, vec(g_mix), win, rope_tabs, tm=tm, n_first=n_p)

    o_p = _attn_prompt(sinks, q, k, v, t=seq)
    nk_p, nv_p = k[seq - WINDOW:seq], v[seq - WINDOW:seq]
    to_t = lambda c: jnp.transpose(c[l], (0, 2, 3, 1))
    o_s, nkt, nvt = _attn_sample(sinks, q, k, v, to_t(cache_win_k), to_t(cache_win_v), nb=8, ts=ts, row0=seq)
    nk_s, nv_s = jnp.transpose(nkt, (0, 3, 1, 2)), jnp.transpose(nvt, (0, 3, 1, 2))

    z_p, hout_p = _ssm_prompt(u, bb, cc, lam_tm, d_skip, jnp.zeros((4, SUBLANES, LANES), F32), t=seq)
    hre_p = jnp.concatenate([hout_p[0], hout_p[2]], axis=0)
    him_p = jnp.concatenate([hout_p[1], hout_p[3]], axis=0)
    z_s, hre_s, him_s = _ssm_sample(u, bb, cc, lre_row, lim_row, d_skip, state_ssm_re[l].reshape(nseq, SSM_CH),
                                    state_ssm_im[l].reshape(nseq, SSM_CH), nb=64, ts=ts, row0=seq)

    x2, ox_p, qx_s = _merge(x1, o_p, o_s, z_p, z_s, ga, gb, wup, wglu, wout, vec(g_xattn), wxq, mk_p, mv_p, tm=tm)
    ox_s = _xattn_sample(qx_s, cache_mem_k[l], cache_mem_v[l], nb=4, ts=ts)
    y_p, y_s = _tail(x2, ox_p, ox_s, wxo, vec(g_ffn2), w2i, w2o, gf, tm=TM_TAIL)

    kvshape = (1, 1, WINDOW, N_KV_HEADS, HEAD_DIM)
    stshape = (1, 1, N_SSM_GROUPS, SSM_STATE)
    memshape = (1, 1, N_MEM, N_X_HEADS, X_HEAD_DIM)
    return (y_p.reshape(1, seq, D_MODEL), y_s.reshape(nseq, ts, D_MODEL),
            nk_p.reshape(kvshape), nv_p.reshape(kvshape), hre_p.reshape(stshape), him_p.reshape(stshape),
            mk_p.reshape(memshape), mv_p.reshape(memshape),
            nk_s.reshape(1, nseq, WINDOW, N_KV_HEADS, HEAD_DIM), nv_s.reshape(1, nseq, WINDOW, N_KV_HEADS, HEAD_DIM),
            hre_s.reshape(1, nseq, N_SSM_GROUPS, SSM_STATE), him_s.reshape(1, nseq, N_SSM_GROUPS, SSM_STATE))
```

```python
import functools
import math

import jax
import jax.numpy as jnp
from jax import lax
from jax.experimental import pallas as pl
from jax.experimental.pallas import tpu as pltpu

F32 = jnp.float32
BF16 = jnp.bfloat16

D_MODEL = 1024
N_Q_HEADS = 8
N_KV_HEADS = 2
HEAD_DIM = 64
Q_PER_KV = N_Q_HEADS // N_KV_HEADS
ATTN_WIDTH = N_Q_HEADS * HEAD_DIM
KV_WIDTH = N_KV_HEADS * HEAD_DIM
WINDOW = 128
ROPE_THETA = 10000.0
SSM_WIDTH = D_MODEL // 2
SSM_GROUP = 16
N_SSM_GROUPS = SSM_WIDTH // SSM_GROUP
SSM_STATE = 64
N_MEM = 256
N_X_HEADS = 4
X_HEAD_DIM = D_MODEL // N_X_HEADS
D_FF = 2816
RMS_EPS = 1e-6
NEG_INF = -1e30
IN_SPLITS = (ATTN_WIDTH, KV_WIDTH, KV_WIDTH, SSM_WIDTH, D_MODEL, D_MODEL)
IN_WIDTH = sum(IN_SPLITS)
OFF_Q, OFF_K, OFF_V, OFF_U, OFF_GA, OFF_GB = (0, 512, 640, 768, 1280, 2304)

LANES = 128
SUBLANES = 8
BF16_SUBLANES = 16
VMEM_LIMIT = 60 * 1024 * 1024

SSM_CH = N_SSM_GROUPS * SSM_STATE
HALF_CH = SSM_CH // 2
HALF_U = SSM_WIDTH // 2
BLK_PER_HALF = 2 * HALF_CH // LANES
TM_FFN = 1024
TM_TAIL = 512
FFN_CHUNKS = ((0, 768), (768, 1536), (1536, 2304), (2304, D_FF))
TM_MIX = 1024
ATTN_QBLOCKS = 8
SSM_TB = 256
SSM_NB = 2
SSM_PITCH = SSM_TB + 4


def _cparams(sem):
    return pltpu.CompilerParams(dimension_semantics=sem, vmem_limit_bytes=VMEM_LIMIT)


def _const_spec(shape):
    nd = len(shape)
    return pl.BlockSpec(shape, lambda *_: (0,) * nd, pipeline_mode=pl.Buffered(1))


def _rms(x, g):
    return x * lax.rsqrt(jnp.mean(x * x, axis=-1, keepdims=True) + RMS_EPS) * g


def _dot(a, b):
    return jnp.dot(a, b, preferred_element_type=F32)


def _dot_nt(a, b):
    return lax.dot_general(a, b, (((1,), (1,)), ((), ())), preferred_element_type=F32)


def _row_spec(tm, width):
    return pl.BlockSpec((tm, width), lambda i: (i, 0))


def _two_array_specs(n_first, tm, width):
    return [pl.BlockSpec((tm, width), lambda i: (jnp.minimum(i, n_first - 1), 0)),
            pl.BlockSpec((tm, width), lambda i: (jnp.maximum(i - n_first, 0), 0))]


def _cast_stream_specs(weights, nsteps):
    in_specs, out_specs, out_shapes = [], [], []
    for w in weights:
        rows, cols = w.shape
        chunk = rows // nsteps
        assert chunk * nsteps == rows and chunk % BF16_SUBLANES == 0
        spec = pl.BlockSpec((chunk, cols), lambda i: (jnp.minimum(i, nsteps - 1), 0))
        in_specs.append(spec)
        out_specs.append(spec)
        out_shapes.append(jax.ShapeDtypeStruct(w.shape, BF16))
    return in_specs, out_specs, out_shapes


def _cast_chunks(src_refs, dst_refs):
    for src, dst in zip(src_refs, dst_refs, strict=True):
        dst[...] = src[...].astype(BF16)


def _store_by_step(is_first, first_ref, second_ref, value):
    @pl.when(is_first)
    def _():
        first_ref[...] = value

    @pl.when(jnp.logical_not(is_first))
    def _():
        second_ref[...] = value


def _ffn_residual(x, g_ref, wi_ref, wo_ref):
    h = _rms(x, g_ref[...]).astype(BF16)
    acc = None
    for lo, hi in FFN_CHUNKS:
        a = _dot(h, wi_ref[:, lo:hi])
        b = _dot(h, wi_ref[:, D_FF + lo:D_FF + hi])
        part = _dot((a * jax.nn.sigmoid(a) * b).astype(BF16), wo_ref[lo:hi, :])
        acc = part if acc is None else acc + part
    return x + 0.5 * acc


def _ffn_first_kernel(xp_ref, xs_ref, g_ref, wi_ref, wo_ref, *rest, n_first, n_cast):
    cast_src, o_ref, cast_dst = rest[:n_cast], rest[n_cast], rest[n_cast + 1:]
    _cast_chunks(cast_src, cast_dst)
    x = jnp.where(pl.program_id(0) < n_first, xp_ref[...], xs_ref[...])
    o_ref[...] = _ffn_residual(x, g_ref, wi_ref, wo_ref)


def _ffn_first(xp, xs, g, wi, wo, cast_weights, *, tm):
    n_first, n_second = xp.shape[0] // tm, xs.shape[0] // tm
    c_in, c_out, c_shapes = _cast_stream_specs(cast_weights, n_first)
    return pl.pallas_call(
        functools.partial(_ffn_first_kernel, n_first=n_first, n_cast=len(cast_weights)),
        grid=(n_first + n_second,),
        in_specs=_two_array_specs(n_first, tm, D_MODEL)
                 + [_const_spec(g.shape), _const_spec(wi.shape), _const_spec(wo.shape)] + c_in,
        out_specs=[_row_spec(tm, D_MODEL)] + c_out,
        out_shape=[jax.ShapeDtypeStruct((xp.shape[0] + xs.shape[0], D_MODEL), F32)] + c_shapes,
        compiler_params=_cparams(("arbitrary",)),
        name="ffn",
    )(xp, xs, g, wi, wo, *cast_weights)


def _tail_kernel(x_ref, oxp_ref, oxs_ref, wxo_ref, g_ref, wi_ref, wo_ref, gf_ref, yp_ref, ys_ref, *, n_first):
    is_first = pl.program_id(0) < n_first
    ox = jnp.where(is_first, oxp_ref[...], oxs_ref[...])
    x = x_ref[...] + _dot(ox, wxo_ref[...])
    y = _rms(_ffn_residual(x, g_ref, wi_ref, wo_ref), gf_ref[...])
    _store_by_step(is_first, yp_ref, ys_ref, y)


def _tail(x2, ox_p, ox_s, wxo, g, wi, wo, gf, *, tm):
    n_first, n_second = ox_p.shape[0] // tm, ox_s.shape[0] // tm
    two = _two_array_specs(n_first, tm, D_MODEL)
    return pl.pallas_call(
        functools.partial(_tail_kernel, n_first=n_first),
        grid=(n_first + n_second,),
        in_specs=[_row_spec(tm, D_MODEL)] + two
                 + [_const_spec(c.shape) for c in (wxo, g, wi, wo, gf)],
        out_specs=two,
        out_shape=[jax.ShapeDtypeStruct((ox_p.shape[0], D_MODEL), F32),
                   jax.ShapeDtypeStruct((ox_s.shape[0], D_MODEL), F32)],
        compiler_params=_cparams(("arbitrary",)),
        name="tail",
    )(x2, ox_p, ox_s, wxo, g, wi, wo, gf)


def _inproj_kernel(x_ref, g_ref, w_ref, ca_ref, sa_ref, cb_ref, sb_ref, q_ref, k_ref, v_ref, u_ref, ga_ref, gb_ref):
    h = _rms(x_ref[...], g_ref[...]).astype(BF16)
    ca, sa, cb, sb = ca_ref[0], sa_ref[0], cb_ref[0], sb_ref[0]
    cos = ca * cb - sa * sb
    sin = sa * cb + ca * sb
    lane = lax.broadcasted_iota(jnp.int32, cos.shape, 1)
    first_half = (lane & (HEAD_DIM - 1)) < (HEAD_DIM // 2)

    def rope(xc):
        rot = jnp.where(first_half, pltpu.roll(xc, LANES - HEAD_DIM // 2, 1), pltpu.roll(xc, HEAD_DIM // 2, 1))
        return xc * cos + rot * sin

    yq = _dot(h, w_ref[:, OFF_Q:OFF_K])
    for c in range(ATTN_WIDTH // LANES):
        q_ref[:, c * LANES:(c + 1) * LANES] = (rope(yq[:, c * LANES:(c + 1) * LANES]) * (HEAD_DIM ** -0.5)).astype(BF16)
    ykvu = _dot(h, w_ref[:, OFF_K:OFF_GA])
    k_ref[...] = rope(ykvu[:, :KV_WIDTH])
    v_ref[...] = ykvu[:, KV_WIDTH:2 * KV_WIDTH]
    u_ref[...] = ykvu[:, 2 * KV_WIDTH:]
    ga_ref[...] = jax.nn.sigmoid(_dot(h, w_ref[:, OFF_GA:OFF_GB])).astype(BF16)
    gb_ref[...] = jax.nn.sigmoid(_dot(h, w_ref[:, OFF_GB:])).astype(BF16)


def _inproj(x, g, w, rope_tabs, *, tm, n_first):
    t = x.shape[0]
    row = lambda wd: _row_spec(tm, wd)
    step_tab = pl.BlockSpec((1, 1, LANES), lambda i: (i, 0, 0))
    row_tab = pl.BlockSpec((1, tm, LANES), lambda i: (jnp.minimum(i // n_first, 1), 0, 0))
    widths = (ATTN_WIDTH, KV_WIDTH, KV_WIDTH, SSM_WIDTH, D_MODEL, D_MODEL)
    dtypes = (BF16, F32, F32, F32, BF16, BF16)
    return pl.pallas_call(
        _inproj_kernel,
        grid=(t // tm,),
        in_specs=[row(D_MODEL), _const_spec((1, D_MODEL)), _const_spec((D_MODEL, IN_WIDTH)), step_tab, step_tab,
                  row_tab, row_tab],
        out_specs=[row(wd) for wd in widths],
        out_shape=[jax.ShapeDtypeStruct((t, wd), dt) for wd, dt in zip(widths, dtypes)],
        compiler_params=_cparams(("parallel",)),
        name="inproj",
    )(x, g, w, *rope_tabs)


def _rope_tables(step_pos, row_pos):
    half = HEAD_DIM // 2
    inv = ROPE_THETA ** (-jnp.arange(half, dtype=F32) / half)
    sign = jnp.concatenate([-jnp.ones((half,), F32), jnp.ones((half,), F32)])

    def tabs(pos):
        ang = pos.astype(F32)[..., None] * inv
        cos = jnp.tile(jnp.cos(ang), LANES // half)
        sin = jnp.tile(jnp.tile(jnp.sin(ang), 2) * sign, LANES // HEAD_DIM)
        return cos, sin

    ca, sa = tabs(step_pos[:, None])
    cb, sb = tabs(row_pos)
    return ca, sa, cb, sb


def _attn_prompt_kernel(sinks_ref, q_ref, kp_ref, kc_ref, vp_ref, vc_ref, *rest, n_cast):
    cast_src, o_ref, cast_dst = rest[:n_cast], rest[n_cast], rest[n_cast + 1:]
    _cast_chunks(cast_src, cast_dst)
    i = pl.program_id(0)
    r = lax.broadcasted_iota(jnp.int32, (WINDOW, 2 * WINDOW), 0)
    c = lax.broadcasted_iota(jnp.int32, (WINDOW, 2 * WINDOW), 1)
    band = (c >= r) & (c <= WINDOW + r)
    first_mask = band & ((c >= WINDOW) | (i > 0))
    kcat = jnp.concatenate([kp_ref[...], kc_ref[...]], axis=0)
    vcat = jnp.concatenate([vp_ref[...], vc_ref[...]], axis=0)
    kswap = pltpu.roll(kcat, HEAD_DIM, 1)
    vswap = pltpu.roll(vcat, HEAD_DIM, 1)
    lo = lax.broadcasted_iota(jnp.int32, kcat.shape, 1) < HEAD_DIM
    lo_q = lax.broadcasted_iota(jnp.int32, (WINDOW, LANES), 1) < HEAD_DIM
    ones = jnp.ones(kcat.shape, F32)
    for kv in range(N_KV_HEADS):
        k_own, k_other = (kcat, kswap) if kv == 0 else (kswap, kcat)
        v_own, v_other = (vcat, vswap) if kv == 0 else (vswap, vcat)
        k_half = (jnp.where(lo, k_own, 0.0).astype(BF16), jnp.where(lo, 0.0, k_other).astype(BF16))
        rhs_half = (jnp.concatenate([jnp.where(lo, v_own, 0.0), jnp.where(lo, ones, 0.0)], axis=1).astype(BF16),
                    jnp.concatenate([jnp.where(lo, 0.0, v_other), jnp.where(lo, 0.0, ones)], axis=1).astype(BF16))
        for j in range(ATTN_QBLOCKS):
            qrows = slice(j * WINDOW, (j + 1) * WINDOW)
            krows = slice(j * WINDOW, (j + 2) * WINDOW)
            mask = first_mask if j == 0 else band
            for c in range(Q_PER_KV // 2):
                tile = kv * (Q_PER_KV // 2) + c
                qt = q_ref[qrows, tile * LANES:(tile + 1) * LANES]
                acc = None
                sink_terms = []
                for side in range(2):
                    s = jnp.where(mask, _dot_nt(qt, k_half[side][krows]), NEG_INF)
                    sink = sinks_ref[2 * tile + side]
                    m = jnp.maximum(jnp.max(s, axis=-1, keepdims=True), sink)
                    pv = _dot(jnp.exp(s - m).astype(BF16), rhs_half[side][krows])
                    acc = pv if acc is None else acc + pv
                    sink_terms.append(jnp.exp(sink - m))
                denom = acc[:, LANES:] + jnp.where(lo_q, sink_terms[0], sink_terms[1])
                o_ref[qrows, tile * LANES:(tile + 1) * LANES] = (acc[:, :LANES] / denom).astype(BF16)


def _attn_prompt(sinks, q, k, v, cast_weights, *, t):
    rows = ATTN_QBLOCKS * WINDOW
    cur = lambda w: pl.BlockSpec((rows, w), lambda i: (i, 0))
    prev = lambda w: pl.BlockSpec((WINDOW, w), lambda i: (jnp.maximum(i * ATTN_QBLOCKS - 1, 0), 0))
    c_in, c_out, c_shapes = _cast_stream_specs(cast_weights, t // rows)
    return pl.pallas_call(
        functools.partial(_attn_prompt_kernel, n_cast=len(cast_weights)),
        grid=(t // rows,),
        in_specs=[pl.BlockSpec(memory_space=pltpu.SMEM), cur(ATTN_WIDTH), prev(KV_WIDTH), cur(KV_WIDTH),
                  prev(KV_WIDTH), cur(KV_WIDTH)] + c_in,
        out_specs=[cur(ATTN_WIDTH)] + c_out,
        out_shape=[jax.ShapeDtypeStruct((t, ATTN_WIDTH), BF16)] + c_shapes,
        compiler_params=_cparams(("parallel",)),
        name="attn_prompt",
    )(sinks, q, k, k, v, v, *cast_weights)


def _attn_sample_kernel(sinks_ref, q_ref, k_ref, v_ref, ckt_ref, cvt_ref, o_ref, nkt_ref, nvt_ref, *, nb, ts):
    rows = nb * ts
    q = q_ref[...]
    pad = jnp.zeros((LANES - rows, KV_WIDTH), F32)
    knt = jnp.concatenate([k_ref[...], pad], axis=0).T
    vnt = jnp.concatenate([v_ref[...], pad], axis=0).T
    lane = lax.broadcasted_iota(jnp.int32, (HEAD_DIM, WINDOW), 1)
    is_new = lane >= WINDOW - ts
    for b in range(nb):
        knt_b = pltpu.roll(knt, (WINDOW - ts - b * ts) % LANES, 1)
        vnt_b = pltpu.roll(vnt, (WINDOW - ts - b * ts) % LANES, 1)
        for kv in range(N_KV_HEADS):
            ksl = slice(kv * HEAD_DIM, (kv + 1) * HEAD_DIM)
            nkt_ref[b, kv] = jnp.where(is_new, knt_b[ksl], pltpu.roll(ckt_ref[b, kv], WINDOW - ts, 1))
            nvt_ref[b, kv] = jnp.where(is_new, vnt_b[ksl], pltpu.roll(cvt_ref[b, kv], WINDOW - ts, 1))

    knt = knt[:, :rows].astype(BF16)
    vnt = vnt[:, :rows].astype(BF16)
    grows = Q_PER_KV * rows
    row1 = lax.broadcasted_iota(jnp.int32, (grows, nb * WINDOW), 0)
    col1 = lax.broadcasted_iota(jnp.int32, (grows, nb * WINDOW), 1)
    rseq1 = (row1 % rows) // ts
    mask_ctx = (rseq1 == col1 // WINDOW) & (col1 % WINDOW >= row1 % ts)
    row2 = lax.broadcasted_iota(jnp.int32, (grows, rows), 0)
    col2 = lax.broadcasted_iota(jnp.int32, (grows, rows), 1)
    mask_new = ((row2 % rows) // ts == col2 // ts) & (col2 % ts <= row2 % ts)
    ghead = lax.broadcasted_iota(jnp.int32, (grows, 1), 0) // rows
    for kv in range(N_KV_HEADS):
        ksl = slice(kv * HEAD_DIM, (kv + 1) * HEAD_DIM)
        qg = jnp.concatenate([q[:, (kv * Q_PER_KV + g) * HEAD_DIM:(kv * Q_PER_KV + g + 1) * HEAD_DIM]
                              for g in range(Q_PER_KV)], axis=0)
        kctx = jnp.concatenate([ckt_ref[b, kv] for b in range(nb)], axis=1).astype(BF16)
        vctx = jnp.concatenate([cvt_ref[b, kv] for b in range(nb)], axis=1).astype(BF16)
        s1 = jnp.where(mask_ctx, _dot(qg, kctx), NEG_INF)
        s2 = jnp.where(mask_new, _dot(qg, knt[ksl]), NEG_INF)
        sink = jnp.zeros((grows, 1), F32)
        for g in range(Q_PER_KV):
            sink = jnp.where(ghead == g, sinks_ref[kv * Q_PER_KV + g], sink)
        m = jnp.maximum(jnp.maximum(jnp.max(s1, axis=-1, keepdims=True), jnp.max(s2, axis=-1, keepdims=True)), sink)
        p1 = jnp.exp(s1 - m)
        p2 = jnp.exp(s2 - m)
        denom = jnp.sum(p1, axis=-1, keepdims=True) + jnp.sum(p2, axis=-1, keepdims=True) + jnp.exp(sink - m)
        o = (_dot_nt(p1.astype(BF16), vctx) + _dot_nt(p2.astype(BF16), vnt[ksl])) / denom
        for g in range(Q_PER_KV):
            h = kv * Q_PER_KV + g
            o_ref[:, h * HEAD_DIM:(h + 1) * HEAD_DIM] = o[g * rows:(g + 1) * rows].astype(BF16)


def _attn_sample(sinks, q, k, v, ckt, cvt, *, nb, ts, row0):
    nseq = ckt.shape[0]
    rows = nb * ts
    assert rows <= LANES and row0 % rows == 0
    row_in = lambda w: pl.BlockSpec((rows, w), lambda i: (i + row0 // rows, 0))
    win = pl.BlockSpec((nb, N_KV_HEADS, HEAD_DIM, WINDOW), lambda i: (i, 0, 0, 0))
    win_shape = jax.ShapeDtypeStruct((nseq, N_KV_HEADS, HEAD_DIM, WINDOW), F32)
    return pl.pallas_call(
        functools.partial(_attn_sample_kernel, nb=nb, ts=ts),
        grid=(nseq // nb,),
        in_specs=[pl.BlockSpec(memory_space=pltpu.SMEM), row_in(ATTN_WIDTH), row_in(KV_WIDTH), row_in(KV_WIDTH),
                  win, win],
        out_specs=[_row_spec(rows, ATTN_WIDTH), win, win],
        out_shape=[jax.ShapeDtypeStruct((nseq * ts, ATTN_WIDTH), BF16), win_shape, win_shape],
        compiler_params=_cparams(("parallel",)),
        name="attn_sample",
    )(sinks, q, k, v, ckt, cvt)


def _ssm_prep_kernel(are_ref, aim_ref, ldt_ref, bre_ref, bim_ref, lre_ref, lim_ref, bbre_ref, bbim_ref):
    a_re = are_ref[...]
    a_im = aim_ref[...]
    dt = jnp.exp(ldt_ref[...])
    mag = jnp.exp(a_re * dt)
    lb_re = mag * jnp.cos(a_im * dt)
    lb_im = mag * jnp.sin(a_im * dt)
    den = a_re * a_re + a_im * a_im
    nr = lb_re - 1.0
    ni = lb_im
    k_re = (nr * a_re + ni * a_im) / den
    k_im = (ni * a_re - nr * a_im) / den
    lre_ref[...] = lb_re
    lim_ref[...] = lb_im
    b_re = bre_ref[...]
    b_im = bim_ref[...]
    bbre_ref[...] = k_re[:, None, :] * b_re - k_im[:, None, :] * b_im
    bbim_ref[...] = k_re[:, None, :] * b_im + k_im[:, None, :] * b_re


def _ssm_prep(a_re, a_im, log_dt, b_re, b_im):
    g, n, gs = b_re.shape
    sds = jax.ShapeDtypeStruct
    return pl.pallas_call(
        _ssm_prep_kernel,
        out_shape=[sds((g, n), F32), sds((g, n), F32), sds((g, gs, n), F32), sds((g, gs, n), F32)],
        name="ssm_prep",
    )(a_re, a_im, log_dt.reshape(g, 1), jnp.swapaxes(b_re, 1, 2), jnp.swapaxes(b_im, 1, 2))


def _block_diag_halves(m):
    g, a, b = m.shape
    gh = g // 2
    tiled = jnp.tile(m.reshape(2, gh * a, b), (1, 1, gh))
    row_group = lax.broadcasted_iota(jnp.int32, tiled.shape, 1) // a
    col_group = lax.broadcasted_iota(jnp.int32, tiled.shape, 2) // b
    return jnp.where(row_group == col_group, tiled, 0.0)


def _ssm_y(u, scan_block, cc_ref, d_ref):
    ys = []
    for half in range(2):
        hc = jnp.concatenate([scan_block(half, jj) for jj in range(BLK_PER_HALF)], axis=1).astype(BF16)
        ys.append(_dot(hc, cc_ref[half]))
    y = jnp.concatenate(ys, axis=1) + d_ref[...] * u
    return jax.nn.gelu(y).astype(BF16)


def _ssm_prompt_kernel(u_ref, bb_ref, cc_ref, lam_ref, d_ref, h0_ref, z_ref, hout_ref,
                       x0_ref, x1_ref, s0_ref, s1_ref, hst_ref):
    tb, pitch = SSM_TB, SSM_PITCH
    x_bufs = (x0_ref, x1_ref)
    s_bufs = (s0_ref, s1_ref)

    @pl.when(pl.program_id(0) == 0)
    def _():
        hst_ref[...] = h0_ref[...]

    def b_proj(k):
        ub = u_ref[k * tb:(k + 1) * tb, :].astype(BF16)
        for half in range(2):
            x = _dot(ub[:, half * HALF_U:(half + 1) * HALF_U], bb_ref[half])
            for jj in range(BLK_PER_HALF):
                x_bufs[k % 2][pl.ds((half * BLK_PER_HALF + jj) * pitch, tb), :] = x[:, jj * LANES:(jj + 1) * LANES]

    lam = [lam_ref[g] for g in range(4)]
    carry = [hst_ref[g] for g in range(4)]

    def scan(k):
        for t in range(tb):
            for half in range(2):
                hr, hi = carry[2 * half], carry[2 * half + 1]
                lr, li = lam[2 * half], lam[2 * half + 1]
                rows_re = pl.ds((half * BLK_PER_HALF) * pitch + t, SUBLANES, stride=pitch)
                rows_im = pl.ds((half * BLK_PER_HALF + SUBLANES) * pitch + t, SUBLANES, stride=pitch)
                nr = lr * hr - li * hi + x_bufs[k % 2][rows_re, :]
                ni = lr * hi + li * hr + x_bufs[k % 2][rows_im, :]
                s_bufs[k % 2][rows_re, :] = nr
                s_bufs[k % 2][rows_im, :] = ni
                carry[2 * half], carry[2 * half + 1] = nr, ni

    def c_proj(k):
        z_ref[k * tb:(k + 1) * tb, :] = _ssm_y(
            u_ref[k * tb:(k + 1) * tb, :],
            lambda half, jj: s_bufs[k % 2][pl.ds((half * BLK_PER_HALF + jj) * pitch, tb), :], cc_ref, d_ref)

    b_proj(0)
    for k in range(SSM_NB):
        if k + 1 < SSM_NB:
            b_proj(k + 1)
        scan(k)
        c_proj(k)
    for g in range(4):
        hst_ref[g] = carry[g]
        hout_ref[g] = carry[g]


def _ssm_prompt(u, bb, cc, lam, d, h0, *, t):
    rows = SSM_NB * SSM_TB
    return pl.pallas_call(
        _ssm_prompt_kernel,
        grid=(t // rows,),
        in_specs=[pl.BlockSpec((rows, SSM_WIDTH), lambda i: (i, 0)), _const_spec(bb.shape), _const_spec(cc.shape),
                  _const_spec(lam.shape), _const_spec(d.shape), _const_spec(h0.shape)],
        out_specs=[pl.BlockSpec((rows, SSM_WIDTH), lambda i: (i, 0)),
                   pl.BlockSpec((4, SUBLANES, LANES), lambda i: (0, 0, 0))],
        out_shape=[jax.ShapeDtypeStruct((t, SSM_WIDTH), BF16), jax.ShapeDtypeStruct((4, SUBLANES, LANES), F32)],
        scratch_shapes=[pltpu.VMEM((2 * BLK_PER_HALF * SSM_PITCH, LANES), F32)] * 4
                       + [pltpu.VMEM((4, SUBLANES, LANES), F32)],
        compiler_params=_cparams(("arbitrary",)),
        name="ssm_prompt",
    )(u, bb, cc, lam, d, h0)


def _ssm_sample_kernel(u_ref, bb_ref, cc_ref, lre_ref, lim_ref, d_ref, h0re_ref, h0im_ref,
                       z_ref, hre_ref, him_ref, xs_ref, *, nb, ts):
    u = u_ref[...]
    ub = u.astype(BF16)
    for half in range(2):
        x = _dot(ub[:, half * HALF_U:(half + 1) * HALF_U], bb_ref[half])
        for jj in range(BLK_PER_HALF):
            xs_ref[half * BLK_PER_HALF + jj] = x[:, jj * LANES:(jj + 1) * LANES]
    for cb in range(SSM_CH // LANES):
        half, jj = divmod(cb, SUBLANES)
        j_re = half * BLK_PER_HALF + jj
        j_im = j_re + SUBLANES
        csl = slice(cb * LANES, (cb + 1) * LANES)
        lr = lre_ref[:, csl]
        li = lim_ref[:, csl]
        hr = h0re_ref[:, csl]
        hi = h0im_ref[:, csl]
        for t in range(ts):
            rows = pl.ds(t, nb, stride=ts)
            nr = lr * hr - li * hi + xs_ref[j_re, rows, :]
            ni = lr * hi + li * hr + xs_ref[j_im, rows, :]
            xs_ref[j_re, rows, :] = nr
            xs_ref[j_im, rows, :] = ni
            hr, hi = nr, ni
        hre_ref[:, csl] = hr
        him_ref[:, csl] = hi
    z_ref[...] = _ssm_y(u, lambda half, jj: xs_ref[half * BLK_PER_HALF + jj], cc_ref, d_ref)


def _ssm_sample(u, bb, cc, lre, lim, d, h0re, h0im, *, nb, ts, row0):
    nseq = h0re.shape[0]
    rows = nb * ts
    assert row0 % rows == 0
    st = pl.BlockSpec((nb, SSM_CH), lambda i: (i, 0))
    return pl.pallas_call(
        functools.partial(_ssm_sample_kernel, nb=nb, ts=ts),
        grid=(nseq // nb,),
        in_specs=[pl.BlockSpec((rows, SSM_WIDTH), lambda i: (i + row0 // rows, 0)), _const_spec(bb.shape),
                  _const_spec(cc.shape),
                  _const_spec(lre.shape), _const_spec(lim.shape), _const_spec(d.shape), st, st],
        out_specs=[pl.BlockSpec((rows, SSM_WIDTH), lambda i: (i, 0)), st, st],
        out_shape=[jax.ShapeDtypeStruct((nseq * ts, SSM_WIDTH), BF16),
                   jax.ShapeDtypeStruct((nseq, SSM_CH), F32), jax.ShapeDtypeStruct((nseq, SSM_CH), F32)],
        scratch_shapes=[pltpu.VMEM((2 * BLK_PER_HALF, rows, LANES), F32)],
        compiler_params=_cparams(("parallel",)),
        name="ssm_sample",
    )(u, bb, cc, lre, lim, d, h0re, h0im)


def _xattn_heads(q, mk, mv):
    outs = []
    for h in range(N_X_HEADS):
        sl = slice(h * X_HEAD_DIM, (h + 1) * X_HEAD_DIM)
        s = _dot_nt(q[:, sl], mk[:, sl])
        p = jnp.exp(s - jnp.max(s, axis=-1, keepdims=True))
        outs.append(_dot(p.astype(BF16), mv[:, sl]) / jnp.sum(p, axis=-1, keepdims=True))
    return jnp.concatenate(outs, axis=1)


def _merge_kernel(x1_ref, op_ref, os_ref, zp_ref, zs_ref, ga_ref, gb_ref, wup_ref, wglu_ref, wout_ref, gx_ref,
                  wxq_ref, mk_ref, mv_ref, x2_ref, oxp_ref, qxs_ref, *, n_first):
    is_first = pl.program_id(0) < n_first
    o = jnp.where(is_first, op_ref[...], os_ref[...])
    z = jnp.where(is_first, zp_ref[...], zs_ref[...])
    ya = _dot(o, wup_ref[...])
    glu = _dot(z, wglu_ref[...])
    yb = glu[:, :D_MODEL] * jax.nn.sigmoid(glu[:, D_MODEL:])
    merged = ga_ref[...].astype(F32) * ya + gb_ref[...].astype(F32) * yb
    x2 = x1_ref[...] + _dot(merged.astype(BF16), wout_ref[...])
    x2_ref[...] = x2
    hx = _rms(x2, gx_ref[...]).astype(BF16)
    qx = (_dot(hx, wxq_ref[...]) * (X_HEAD_DIM ** -0.5)).astype(BF16)
    ox = _xattn_heads(qx, mk_ref[...].astype(BF16), mv_ref[...].astype(BF16)).astype(BF16)

    @pl.when(is_first)
    def _():
        oxp_ref[...] = ox

    @pl.when(jnp.logical_not(is_first))
    def _():
        qxs_ref[...] = qx


def _merge(x1, o_p, o_s, z_p, z_s, ga, gb, wup, wglu, wout, gx, wxq, mk, mv, *, tm):
    n_first, n_second = o_p.shape[0] // tm, o_s.shape[0] // tm
    row = lambda w: _row_spec(tm, w)
    consts = [wup, wglu, wout, gx, wxq, mk, mv]
    return pl.pallas_call(
        functools.partial(_merge_kernel, n_first=n_first),
        grid=(n_first + n_second,),
        in_specs=[row(D_MODEL)] + _two_array_specs(n_first, tm, ATTN_WIDTH) + _two_array_specs(n_first, tm, SSM_WIDTH)
                 + [row(D_MODEL), row(D_MODEL)] + [_const_spec(c.shape) for c in consts],
        out_specs=[row(D_MODEL)] + _two_array_specs(n_first, tm, D_MODEL),
        out_shape=[jax.ShapeDtypeStruct(x1.shape, F32), jax.ShapeDtypeStruct((o_p.shape[0], D_MODEL), BF16),
                   jax.ShapeDtypeStruct((o_s.shape[0], D_MODEL), BF16)],
        compiler_params=_cparams(("arbitrary",)),
        name="merge",
    )(x1, o_p, o_s, z_p, z_s, ga, gb, *consts)


def _memkv_kernel(mem_ref, g_ref, wk_ref, wv_ref, mk_ref, mv_ref):
    mn = _rms(mem_ref[...], g_ref[...]).astype(BF16)
    mk_ref[...] = _dot(mn, wk_ref[...].astype(BF16))
    mv_ref[...] = _dot(mn, wv_ref[...].astype(BF16))


def _memkv(mem, g, wk, wv):
    sds = jax.ShapeDtypeStruct((mem.shape[0], D_MODEL), F32)
    return pl.pallas_call(_memkv_kernel, out_shape=[sds, sds], name="memkv",
                          compiler_params=pltpu.CompilerParams(vmem_limit_bytes=VMEM_LIMIT))(mem, g, wk, wv)


def _xattn_sample_kernel(q_ref, mk_ref, mv_ref, o_ref, *, nb, ts):
    qrows = ts * N_X_HEADS
    krows = N_MEM * N_X_HEADS
    row = lax.broadcasted_iota(jnp.int32, (qrows, krows), 0)
    col = lax.broadcasted_iota(jnp.int32, (qrows, krows), 1)
    same_head = (row // ts) == (col & (N_X_HEADS - 1))
    for b in range(nb):
        qb = q_ref[b * ts:(b + 1) * ts, :]
        qs = jnp.concatenate([qb[:, h * X_HEAD_DIM:(h + 1) * X_HEAD_DIM] for h in range(N_X_HEADS)], axis=0)
        kf = mk_ref[b].reshape(krows, X_HEAD_DIM).astype(BF16)
        vf = mv_ref[b].reshape(krows, X_HEAD_DIM).astype(BF16)
        s = jnp.where(same_head, _dot_nt(qs, kf), NEG_INF)
        p = jnp.exp(s - jnp.max(s, axis=-1, keepdims=True))
        o = (_dot(p.astype(BF16), vf) / jnp.sum(p, axis=-1, keepdims=True)).astype(BF16)
        for h in range(N_X_HEADS):
            o_ref[b * ts:(b + 1) * ts, h * X_HEAD_DIM:(h + 1) * X_HEAD_DIM] = o[h * ts:(h + 1) * ts]


def _xattn_sample(qx, mk, mv, *, nb, ts):
    nseq = mk.shape[0]
    row = pl.BlockSpec((nb * ts, D_MODEL), lambda i: (i, 0))
    mem = pl.BlockSpec((nb, N_MEM, N_X_HEADS, X_HEAD_DIM), lambda i: (i, 0, 0, 0))
    return pl.pallas_call(
        functools.partial(_xattn_sample_kernel, nb=nb, ts=ts),
        grid=(nseq // nb,),
        in_specs=[row, mem, mem],
        out_specs=row,
        out_shape=jax.ShapeDtypeStruct((nseq * ts, D_MODEL), BF16),
        compiler_params=_cparams(("parallel",)),
        name="xattn_sample",
    )(qx, mk, mv)


def kernel(x_prompt, x_sample, cache_win_k, cache_win_v, state_ssm_re, state_ssm_im, cache_mem_k, cache_mem_v, mem_prompt, g_ffn1, w_ffn1_in, w_ffn1_out, g_mix, w_in, attn_sinks, ssm_a_re, ssm_a_im, ssm_log_dt, ssm_b_re, ssm_b_im, ssm_c_re, ssm_c_im, ssm_d, w_attn_up, w_ssm_glu, w_out, g_xattn, g_mem, w_xq, w_xk, w_xv, w_xo, g_ffn2, w_ffn2_in, w_ffn2_out, g_final):
    assert x_prompt.shape[0] == 1 and g_ffn1.shape[0] == 1
    seq = x_prompt.shape[1]
    nseq, ts = x_sample.shape[0], x_sample.shape[1]
    past_len = seq
    l = 0
    vec = lambda g: g[l].reshape(1, -1)
    w1i, w1o = w_ffn1_in[l].astype(BF16), w_ffn1_out[l].astype(BF16)
    gf = g_final.reshape(1, -1)
    sinks = attn_sinks[l]

    lam_re, lam_im, bbt_re, bbt_im = _ssm_prep(ssm_a_re[l], ssm_a_im[l], ssm_log_dt[l], ssm_b_re[l], ssm_b_im[l])
    bb = jnp.concatenate([_block_diag_halves(bbt_re), _block_diag_halves(bbt_im)], axis=-1).astype(BF16)
    ct_re = jnp.swapaxes(ssm_c_re[l], 1, 2)
    ct_im = jnp.swapaxes(ssm_c_im[l], 1, 2)
    cc = jnp.concatenate([_block_diag_halves(ct_re), -_block_diag_halves(ct_im)], axis=1).astype(BF16)
    d_skip = ssm_d[l].reshape(1, -1)
    lre16 = lam_re.reshape(2, SUBLANES, LANES)
    lim16 = lam_im.reshape(2, SUBLANES, LANES)
    lam_tm = jnp.stack([lre16[0], lim16[0], lre16[1], lim16[1]])
    lre_row = lam_re.reshape(1, SSM_CH)
    lim_row = lam_im.reshape(1, SSM_CH)

    mk_p, mv_p = _memkv(mem_prompt[0], vec(g_mem), w_xk[l], w_xv[l])

    tm = TM_MIX
    t_s = nseq * ts
    assert seq % tm == 0 and t_s % tm == 0 and tm % ts == 0
    n_p, n_s = seq // tm, t_s // tm
    rope_tabs = _rope_tables(jnp.concatenate([jnp.arange(n_p) * tm, jnp.full((n_s,), past_len)]),
                             jnp.stack([jnp.arange(tm), jnp.arange(tm) % ts]))

    x1, win = _ffn_first(x_prompt[0], x_sample.reshape(t_s, D_MODEL), vec(g_ffn1), w1i, w1o, [w_in[l]], tm=TM_FFN)
    q, k, v, u, ga, gb = _inproj(x1, vec(g_mix), win, rope_tabs, tm=tm, n_first=n_p)

    o_p, wup, wglu, wout, wxq, wxo, w2i, w2o = _attn_prompt(
        sinks, q, k, v, [w_attn_up[l], w_ssm_glu[l], w_out[l], w_xq[l], w_xo[l], w_ffn2_in[l], w_ffn2_out[l]], t=seq)
    nk_p, nv_p = k[seq - WINDOW:seq], v[seq - WINDOW:seq]
    to_t = lambda c: jnp.transpose(c[l], (0, 2, 3, 1))
    o_s, nkt, nvt = _attn_sample(sinks, q, k, v, to_t(cache_win_k), to_t(cache_win_v), nb=8, ts=ts, row0=seq)
    nk_s, nv_s = jnp.transpose(nkt, (0, 3, 1, 2)), jnp.transpose(nvt, (0, 3, 1, 2))

    z_p, hout_p = _ssm_prompt(u, bb, cc, lam_tm, d_skip, jnp.zeros((4, SUBLANES, LANES), F32), t=seq)
    hre_p = jnp.concatenate([hout_p[0], hout_p[2]], axis=0)
    him_p = jnp.concatenate([hout_p[1], hout_p[3]], axis=0)
    z_s, hre_s, him_s = _ssm_sample(u, bb, cc, lre_row, lim_row, d_skip, state_ssm_re[l].reshape(nseq, SSM_CH),
                                    state_ssm_im[l].reshape(nseq, SSM_CH), nb=64, ts=ts, row0=seq)

    x2, ox_p, qx_s = _merge(x1, o_p, o_s, z_p, z_s, ga, gb, wup, wglu, wout, vec(g_xattn), wxq, mk_p, mv_p, tm=tm)
    ox_s = _xattn_sample(qx_s, cache_mem_k[l], cache_mem_v[l], nb=4, ts=ts)
    y_p, y_s = _tail(x2, ox_p, ox_s, wxo, vec(g_ffn2), w2i, w2o, gf, tm=TM_TAIL)

    kvshape = (1, 1, WINDOW, N_KV_HEADS, HEAD_DIM)
    stshape = (1, 1, N_SSM_GROUPS, SSM_STATE)
    memshape = (1, 1, N_MEM, N_X_HEADS, X_HEAD_DIM)
    return (y_p.reshape(1, seq, D_MODEL), y_s.reshape(nseq, ts, D_MODEL),
            nk_p.reshape(kvshape), nv_p.reshape(kvshape), hre_p.reshape(stshape), him_p.reshape(stshape),
            mk_p.reshape(memshape), mv_p.reshape(memshape),
            nk_s.reshape(1, nseq, WINDOW, N_KV_HEADS, HEAD_DIM), nv_s.reshape(1, nseq, WINDOW, N_KV_HEADS, HEAD_DIM),
            hre_s.reshape(1, nseq, N_SSM_GROUPS, SSM_STATE), him_s.reshape(1, nseq, N_SSM_GROUPS, SSM_STATE))
```

```python
import functools
import math

import jax
import jax.numpy as jnp
from jax import lax
from jax.experimental import pallas as pl
from jax.experimental.pallas import tpu as pltpu

F32 = jnp.float32
BF16 = jnp.bfloat16

D_MODEL = 1024
N_Q_HEADS = 8
N_KV_HEADS = 2
HEAD_DIM = 64
Q_PER_KV = N_Q_HEADS // N_KV_HEADS
ATTN_WIDTH = N_Q_HEADS * HEAD_DIM
KV_WIDTH = N_KV_HEADS * HEAD_DIM
WINDOW = 128
ROPE_THETA = 10000.0
SSM_WIDTH = D_MODEL // 2
SSM_GROUP = 16
N_SSM_GROUPS = SSM_WIDTH // SSM_GROUP
SSM_STATE = 64
N_MEM = 256
N_X_HEADS = 4
X_HEAD_DIM = D_MODEL // N_X_HEADS
D_FF = 2816
RMS_EPS = 1e-6
NEG_INF = -1e30
IN_SPLITS = (ATTN_WIDTH, KV_WIDTH, KV_WIDTH, SSM_WIDTH, D_MODEL, D_MODEL)
IN_WIDTH = sum(IN_SPLITS)
OFF_Q, OFF_K, OFF_V, OFF_U, OFF_GA, OFF_GB = (0, 512, 640, 768, 1280, 2304)

LANES = 128
SUBLANES = 8
BF16_SUBLANES = 16
VMEM_LIMIT = 60 * 1024 * 1024

SSM_CH = N_SSM_GROUPS * SSM_STATE
HALF_CH = SSM_CH // 2
HALF_U = SSM_WIDTH // 2
BLK_PER_HALF = 2 * HALF_CH // LANES
TM_FFN = 1024
TM_TAIL = 512
FFN_CHUNKS = ((0, 768), (768, 1536), (1536, 2304), (2304, D_FF))
TM_MIX = 1024
ATTN_QBLOCKS = 8
SSM_TB = 256
SSM_NB = 2
SSM_PITCH = SSM_TB + 4


def _cparams(sem):
    return pltpu.CompilerParams(dimension_semantics=sem, vmem_limit_bytes=VMEM_LIMIT)


def _const_spec(shape):
    nd = len(shape)
    return pl.BlockSpec(shape, lambda *_: (0,) * nd, pipeline_mode=pl.Buffered(1))


def _rms(x, g):
    return x * lax.rsqrt(jnp.mean(x * x, axis=-1, keepdims=True) + RMS_EPS) * g


def _dot(a, b):
    return jnp.dot(a, b, preferred_element_type=F32)


def _dot_nt(a, b):
    return lax.dot_general(a, b, (((1,), (1,)), ((), ())), preferred_element_type=F32)


def _row_spec(tm, width):
    return pl.BlockSpec((tm, width), lambda i: (i, 0))


def _two_array_specs(n_first, tm, width):
    return [pl.BlockSpec((tm, width), lambda i: (jnp.minimum(i, n_first - 1), 0)),
            pl.BlockSpec((tm, width), lambda i: (jnp.maximum(i - n_first, 0), 0))]


def _cast_stream_specs(weights, nsteps):
    in_specs, out_specs, out_shapes = [], [], []
    for w in weights:
        rows, cols = w.shape
        chunk = rows // nsteps
        assert chunk * nsteps == rows and chunk % BF16_SUBLANES == 0
        spec = pl.BlockSpec((chunk, cols), lambda i: (jnp.minimum(i, nsteps - 1), 0))
        in_specs.append(spec)
        out_specs.append(spec)
        out_shapes.append(jax.ShapeDtypeStruct(w.shape, BF16))
    return in_specs, out_specs, out_shapes


def _cast_chunks(src_refs, dst_refs):
    for src, dst in zip(src_refs, dst_refs, strict=True):
        dst[...] = src[...].astype(BF16)


def _store_by_step(is_first, first_ref, second_ref, value):
    @pl.when(is_first)
    def _():
        first_ref[...] = value

    @pl.when(jnp.logical_not(is_first))
    def _():
        second_ref[...] = value


def _ffn_residual(x, g_ref, wi_ref, wo_ref):
    h = _rms(x, g_ref[...]).astype(BF16)
    acc = None
    for lo, hi in FFN_CHUNKS:
        a = _dot(h, wi_ref[:, lo:hi])
        b = _dot(h, wi_ref[:, D_FF + lo:D_FF + hi])
        part = _dot((a * jax.nn.sigmoid(a) * b).astype(BF16), wo_ref[lo:hi, :])
        acc = part if acc is None else acc + part
    return x + 0.5 * acc


def _ffn_first_kernel(xp_ref, xs_ref, g_ref, wi_ref, wo_ref, *rest, n_first, n_cast):
    cast_src, o_ref, cast_dst = rest[:n_cast], rest[n_cast], rest[n_cast + 1:]
    _cast_chunks(cast_src, cast_dst)
    x = jnp.where(pl.program_id(0) < n_first, xp_ref[...], xs_ref[...])
    o_ref[...] = _ffn_residual(x, g_ref, wi_ref, wo_ref)


def _ffn_first(xp, xs, g, wi, wo, cast_weights, *, tm):
    n_first, n_second = xp.shape[0] // tm, xs.shape[0] // tm
    c_in, c_out, c_shapes = _cast_stream_specs(cast_weights, n_first)
    return pl.pallas_call(
        functools.partial(_ffn_first_kernel, n_first=n_first, n_cast=len(cast_weights)),
        grid=(n_first + n_second,),
        in_specs=_two_array_specs(n_first, tm, D_MODEL)
                 + [_const_spec(g.shape), _const_spec(wi.shape), _const_spec(wo.shape)] + c_in,
        out_specs=[_row_spec(tm, D_MODEL)] + c_out,
        out_shape=[jax.ShapeDtypeStruct((xp.shape[0] + xs.shape[0], D_MODEL), F32)] + c_shapes,
        compiler_params=_cparams(("arbitrary",)),
        name="ffn",
    )(xp, xs, g, wi, wo, *cast_weights)


def _xattn_one_seq(qb, mk, mv):
    ts = qb.shape[0]
    krows = N_MEM * N_X_HEADS
    row = lax.broadcasted_iota(jnp.int32, (ts * N_X_HEADS, krows), 0)
    col = lax.broadcasted_iota(jnp.int32, (ts * N_X_HEADS, krows), 1)
    same_head = (row // ts) == (col & (N_X_HEADS - 1))
    qs = jnp.concatenate([qb[:, h * X_HEAD_DIM:(h + 1) * X_HEAD_DIM] for h in range(N_X_HEADS)], axis=0)
    kf = mk.reshape(krows, X_HEAD_DIM).astype(BF16)
    vf = mv.reshape(krows, X_HEAD_DIM).astype(BF16)
    s = jnp.where(same_head, _dot_nt(qs, kf), NEG_INF)
    p = jnp.exp(s - jnp.max(s, axis=-1, keepdims=True))
    o = (_dot(p.astype(BF16), vf) / jnp.sum(p, axis=-1, keepdims=True)).astype(BF16)
    return jnp.concatenate([o[h * ts:(h + 1) * ts] for h in range(N_X_HEADS)], axis=1)


def _tail_kernel(x_ref, oxp_ref, qxs_ref, mk_ref, mv_ref, wxo_ref, g_ref, wi_ref, wo_ref, gf_ref, yp_ref, ys_ref,
                 oxs_ref, *, n_first, nb, ts, tm):
    i = pl.program_id(0)
    is_first = i < n_first
    t_s = qxs_ref.shape[0]
    pair = 2 * ts

    @pl.when(i == 0)
    def _():
        oxs_ref[...] = jnp.zeros(oxs_ref.shape, oxs_ref.dtype)

    row0 = pl.multiple_of(jnp.maximum(i - n_first, 0) * tm, tm)
    ox = jnp.where(is_first, oxp_ref[...], oxs_ref[pl.ds(row0, tm), :])
    x = x_ref[...] + _dot(ox, wxo_ref[...])
    y = _rms(_ffn_residual(x, g_ref, wi_ref, wo_ref), gf_ref[...])

    seq0 = jnp.minimum(i, n_first - 1) * nb
    for j in range(nb // 2):
        src = pl.multiple_of((seq0 + 2 * j) * ts, pair)
        q2 = qxs_ref[pl.ds(src, pair), :]
        o2 = jnp.concatenate([_xattn_one_seq(q2[b * ts:(b + 1) * ts], mk_ref[2 * j + b], mv_ref[2 * j + b])
                              for b in range(2)], axis=0)
        dst = pl.multiple_of(jnp.where(is_first, src, t_s), pair)
        oxs_ref[pl.ds(dst, pair), :] = o2

    _store_by_step(is_first, yp_ref, ys_ref, y)


def _tail(x2, ox_p, qx_s, mk, mv, wxo, g, wi, wo, gf, *, tm, ts):
    t_p, t_s, nseq = ox_p.shape[0], qx_s.shape[0], mk.shape[0]
    n_first, n_second = t_p // tm, t_s // tm
    nb = nseq // n_first
    assert nb * n_first == nseq and nb % 2 == 0 and nseq * ts == t_s
    mem = pl.BlockSpec((nb, N_MEM, N_X_HEADS, X_HEAD_DIM), lambda i: (jnp.minimum(i, n_first - 1), 0, 0, 0))
    return pl.pallas_call(
        functools.partial(_tail_kernel, n_first=n_first, nb=nb, ts=ts, tm=tm),
        grid=(n_first + n_second,),
        in_specs=[_row_spec(tm, D_MODEL), _two_array_specs(n_first, tm, D_MODEL)[0], _const_spec(qx_s.shape), mem, mem]
                 + [_const_spec(c.shape) for c in (wxo, g, wi, wo, gf)],
        out_specs=_two_array_specs(n_first, tm, D_MODEL),
        out_shape=[jax.ShapeDtypeStruct((t_p, D_MODEL), F32), jax.ShapeDtypeStruct((t_s, D_MODEL), F32)],
        scratch_shapes=[pltpu.VMEM((t_s + 2 * ts, D_MODEL), BF16)],
        compiler_params=_cparams(("arbitrary",)),
        name="tail",
    )(x2, ox_p, qx_s, mk, mv, wxo, g, wi, wo, gf)


def _inproj_kernel(x_ref, g_ref, w_ref, ca_ref, sa_ref, cb_ref, sb_ref, q_ref, k_ref, v_ref, u_ref, ga_ref, gb_ref):
    h = _rms(x_ref[...], g_ref[...]).astype(BF16)
    ca, sa, cb, sb = ca_ref[0], sa_ref[0], cb_ref[0], sb_ref[0]
    cos = ca * cb - sa * sb
    sin = sa * cb + ca * sb
    lane = lax.broadcasted_iota(jnp.int32, cos.shape, 1)
    first_half = (lane & (HEAD_DIM - 1)) < (HEAD_DIM // 2)

    def rope(xc):
        rot = jnp.where(first_half, pltpu.roll(xc, LANES - HEAD_DIM // 2, 1), pltpu.roll(xc, HEAD_DIM // 2, 1))
        return xc * cos + rot * sin

    yq = _dot(h, w_ref[:, OFF_Q:OFF_K])
    for c in range(ATTN_WIDTH // LANES):
        q_ref[:, c * LANES:(c + 1) * LANES] = (rope(yq[:, c * LANES:(c + 1) * LANES]) * (HEAD_DIM ** -0.5)).astype(BF16)
    ykvu = _dot(h, w_ref[:, OFF_K:OFF_GA])
    k_ref[...] = rope(ykvu[:, :KV_WIDTH])
    v_ref[...] = ykvu[:, KV_WIDTH:2 * KV_WIDTH]
    u_ref[...] = ykvu[:, 2 * KV_WIDTH:]
    ga_ref[...] = jax.nn.sigmoid(_dot(h, w_ref[:, OFF_GA:OFF_GB])).astype(BF16)
    gb_ref[...] = jax.nn.sigmoid(_dot(h, w_ref[:, OFF_GB:])).astype(BF16)


def _inproj(x, g, w, rope_tabs, *, tm, n_first):
    t = x.shape[0]
    row = lambda wd: _row_spec(tm, wd)
    step_tab = pl.BlockSpec((1, 1, LANES), lambda i: (i, 0, 0))
    row_tab = pl.BlockSpec((1, tm, LANES), lambda i: (jnp.minimum(i // n_first, 1), 0, 0))
    widths = (ATTN_WIDTH, KV_WIDTH, KV_WIDTH, SSM_WIDTH, D_MODEL, D_MODEL)
    dtypes = (BF16, F32, F32, F32, BF16, BF16)
    return pl.pallas_call(
        _inproj_kernel,
        grid=(t // tm,),
        in_specs=[row(D_MODEL), _const_spec((1, D_MODEL)), _const_spec((D_MODEL, IN_WIDTH)), step_tab, step_tab,
                  row_tab, row_tab],
        out_specs=[row(wd) for wd in widths],
        out_shape=[jax.ShapeDtypeStruct((t, wd), dt) for wd, dt in zip(widths, dtypes)],
        compiler_params=_cparams(("parallel",)),
        name="inproj",
    )(x, g, w, *rope_tabs)


def _rope_tables(step_pos, row_pos):
    half = HEAD_DIM // 2
    inv = ROPE_THETA ** (-jnp.arange(half, dtype=F32) / half)
    sign = jnp.concatenate([-jnp.ones((half,), F32), jnp.ones((half,), F32)])

    def tabs(pos):
        ang = pos.astype(F32)[..., None] * inv
        cos = jnp.tile(jnp.cos(ang), LANES // half)
        sin = jnp.tile(jnp.tile(jnp.sin(ang), 2) * sign, LANES // HEAD_DIM)
        return cos, sin

    ca, sa = tabs(step_pos[:, None])
    cb, sb = tabs(row_pos)
    return ca, sa, cb, sb


def _attn_prompt_kernel(sinks_ref, q_ref, kp_ref, kc_ref, vp_ref, vc_ref, *rest, n_cast):
    cast_src, o_ref, cast_dst = rest[:n_cast], rest[n_cast], rest[n_cast + 1:]
    _cast_chunks(cast_src, cast_dst)
    i = pl.program_id(0)
    r = lax.broadcasted_iota(jnp.int32, (WINDOW, 2 * WINDOW), 0)
    c = lax.broadcasted_iota(jnp.int32, (WINDOW, 2 * WINDOW), 1)
    band = (c >= r) & (c <= WINDOW + r)
    first_mask = band & ((c >= WINDOW) | (i > 0))
    kcat = jnp.concatenate([kp_ref[...], kc_ref[...]], axis=0)
    vcat = jnp.concatenate([vp_ref[...], vc_ref[...]], axis=0)
    kswap = pltpu.roll(kcat, HEAD_DIM, 1)
    vswap = pltpu.roll(vcat, HEAD_DIM, 1)
    lo = lax.broadcasted_iota(jnp.int32, kcat.shape, 1) < HEAD_DIM
    lo_q = lax.broadcasted_iota(jnp.int32, (WINDOW, LANES), 1) < HEAD_DIM
    ones = jnp.ones(kcat.shape, F32)
    for kv in range(N_KV_HEADS):
        k_own, k_other = (kcat, kswap) if kv == 0 else (kswap, kcat)
        v_own, v_other = (vcat, vswap) if kv == 0 else (vswap, vcat)
        k_half = (jnp.where(lo, k_own, 0.0).astype(BF16), jnp.where(lo, 0.0, k_other).astype(BF16))
        rhs_half = (jnp.concatenate([jnp.where(lo, v_own, 0.0), jnp.where(lo, ones, 0.0)], axis=1).astype(BF16),
                    jnp.concatenate([jnp.where(lo, 0.0, v_other), jnp.where(lo, 0.0, ones)], axis=1).astype(BF16))
        for j in range(ATTN_QBLOCKS):
            qrows = slice(j * WINDOW, (j + 1) * WINDOW)
            krows = slice(j * WINDOW, (j + 2) * WINDOW)
            mask = first_mask if j == 0 else band
            for c in range(Q_PER_KV // 2):
                tile = kv * (Q_PER_KV // 2) + c
                qt = q_ref[qrows, tile * LANES:(tile + 1) * LANES]
                acc = None
                sink_terms = []
                for side in range(2):
                    s = jnp.where(mask, _dot_nt(qt, k_half[side][krows]), NEG_INF)
                    sink = sinks_ref[2 * tile + side]
                    m = jnp.maximum(jnp.max(s, axis=-1, keepdims=True), sink)
                    pv = _dot(jnp.exp(s - m).astype(BF16), rhs_half[side][krows])
                    acc = pv if acc is None else acc + pv
                    sink_terms.append(jnp.exp(sink - m))
                denom = acc[:, LANES:] + jnp.where(lo_q, sink_terms[0], sink_terms[1])
                o_ref[qrows, tile * LANES:(tile + 1) * LANES] = (acc[:, :LANES] / denom).astype(BF16)


def _attn_prompt(sinks, q, k, v, cast_weights, *, t):
    rows = ATTN_QBLOCKS * WINDOW
    cur = lambda w: pl.BlockSpec((rows, w), lambda i: (i, 0))
    prev = lambda w: pl.BlockSpec((WINDOW, w), lambda i: (jnp.maximum(i * ATTN_QBLOCKS - 1, 0), 0))
    c_in, c_out, c_shapes = _cast_stream_specs(cast_weights, t // rows)
    return pl.pallas_call(
        functools.partial(_attn_prompt_kernel, n_cast=len(cast_weights)),
        grid=(t // rows,),
        in_specs=[pl.BlockSpec(memory_space=pltpu.SMEM), cur(ATTN_WIDTH), prev(KV_WIDTH), cur(KV_WIDTH),
                  prev(KV_WIDTH), cur(KV_WIDTH)] + c_in,
        out_specs=[cur(ATTN_WIDTH)] + c_out,
        out_shape=[jax.ShapeDtypeStruct((t, ATTN_WIDTH), BF16)] + c_shapes,
        compiler_params=_cparams(("parallel",)),
        name="attn_prompt",
    )(sinks, q, k, k, v, v, *cast_weights)


def _attn_sample_kernel(sinks_ref, q_ref, k_ref, v_ref, ckt_ref, cvt_ref, o_ref, nkt_ref, nvt_ref, *, nb, ts):
    rows = nb * ts
    q = q_ref[...]
    pad = jnp.zeros((LANES - rows, KV_WIDTH), F32)
    knt = jnp.concatenate([k_ref[...], pad], axis=0).T
    vnt = jnp.concatenate([v_ref[...], pad], axis=0).T
    lane = lax.broadcasted_iota(jnp.int32, (HEAD_DIM, WINDOW), 1)
    is_new = lane >= WINDOW - ts
    for b in range(nb):
        knt_b = pltpu.roll(knt, (WINDOW - ts - b * ts) % LANES, 1)
        vnt_b = pltpu.roll(vnt, (WINDOW - ts - b * ts) % LANES, 1)
        for kv in range(N_KV_HEADS):
            ksl = slice(kv * HEAD_DIM, (kv + 1) * HEAD_DIM)
            nkt_ref[b, kv] = jnp.where(is_new, knt_b[ksl], pltpu.roll(ckt_ref[b, kv], WINDOW - ts, 1))
            nvt_ref[b, kv] = jnp.where(is_new, vnt_b[ksl], pltpu.roll(cvt_ref[b, kv], WINDOW - ts, 1))

    knt = knt[:, :rows].astype(BF16)
    vnt = vnt[:, :rows].astype(BF16)
    grows = Q_PER_KV * rows
    row1 = lax.broadcasted_iota(jnp.int32, (grows, nb * WINDOW), 0)
    col1 = lax.broadcasted_iota(jnp.int32, (grows, nb * WINDOW), 1)
    rseq1 = (row1 % rows) // ts
    mask_ctx = (rseq1 == col1 // WINDOW) & (col1 % WINDOW >= row1 % ts)
    row2 = lax.broadcasted_iota(jnp.int32, (grows, rows), 0)
    col2 = lax.broadcasted_iota(jnp.int32, (grows, rows), 1)
    mask_new = ((row2 % rows) // ts == col2 // ts) & (col2 % ts <= row2 % ts)
    ghead = lax.broadcasted_iota(jnp.int32, (grows, 1), 0) // rows
    for kv in range(N_KV_HEADS):
        ksl = slice(kv * HEAD_DIM, (kv + 1) * HEAD_DIM)
        qg = jnp.concatenate([q[:, (kv * Q_PER_KV + g) * HEAD_DIM:(kv * Q_PER_KV + g + 1) * HEAD_DIM]
                              for g in range(Q_PER_KV)], axis=0)
        kctx = jnp.concatenate([ckt_ref[b, kv] for b in range(nb)], axis=1).astype(BF16)
        vctx = jnp.concatenate([cvt_ref[b, kv] for b in range(nb)], axis=1).astype(BF16)
        s1 = jnp.where(mask_ctx, _dot(qg, kctx), NEG_INF)
        s2 = jnp.where(mask_new, _dot(qg, knt[ksl]), NEG_INF)
        sink = jnp.zeros((grows, 1), F32)
        for g in range(Q_PER_KV):
            sink = jnp.where(ghead == g, sinks_ref[kv * Q_PER_KV + g], sink)
        m = jnp.maximum(jnp.maximum(jnp.max(s1, axis=-1, keepdims=True), jnp.max(s2, axis=-1, keepdims=True)), sink)
        p1 = jnp.exp(s1 - m)
        p2 = jnp.exp(s2 - m)
        denom = jnp.sum(p1, axis=-1, keepdims=True) + jnp.sum(p2, axis=-1, keepdims=True) + jnp.exp(sink - m)
        o = (_dot_nt(p1.astype(BF16), vctx) + _dot_nt(p2.astype(BF16), vnt[ksl])) / denom
        for g in range(Q_PER_KV):
            h = kv * Q_PER_KV + g
            o_ref[:, h * HEAD_DIM:(h + 1) * HEAD_DIM] = o[g * rows:(g + 1) * rows].astype(BF16)


def _attn_sample(sinks, q, k, v, ckt, cvt, *, nb, ts, row0):
    nseq = ckt.shape[0]
    rows = nb * ts
    assert rows <= LANES and row0 % rows == 0
    row_in = lambda w: pl.BlockSpec((rows, w), lambda i: (i + row0 // rows, 0))
    win = pl.BlockSpec((nb, N_KV_HEADS, HEAD_DIM, WINDOW), lambda i: (i, 0, 0, 0))
    win_shape = jax.ShapeDtypeStruct((nseq, N_KV_HEADS, HEAD_DIM, WINDOW), F32)
    return pl.pallas_call(
        functools.partial(_attn_sample_kernel, nb=nb, ts=ts),
        grid=(nseq // nb,),
        in_specs=[pl.BlockSpec(memory_space=pltpu.SMEM), row_in(ATTN_WIDTH), row_in(KV_WIDTH), row_in(KV_WIDTH),
                  win, win],
        out_specs=[_row_spec(rows, ATTN_WIDTH), win, win],
        out_shape=[jax.ShapeDtypeStruct((nseq * ts, ATTN_WIDTH), BF16), win_shape, win_shape],
        compiler_params=_cparams(("parallel",)),
        name="attn_sample",
    )(sinks, q, k, v, ckt, cvt)


def _ssm_prep_kernel(are_ref, aim_ref, ldt_ref, bre_ref, bim_ref, lre_ref, lim_ref, bbre_ref, bbim_ref):
    a_re = are_ref[...]
    a_im = aim_ref[...]
    dt = jnp.exp(ldt_ref[...])
    mag = jnp.exp(a_re * dt)
    lb_re = mag * jnp.cos(a_im * dt)
    lb_im = mag * jnp.sin(a_im * dt)
    den = a_re * a_re + a_im * a_im
    nr = lb_re - 1.0
    ni = lb_im
    k_re = (nr * a_re + ni * a_im) / den
    k_im = (ni * a_re - nr * a_im) / den
    lre_ref[...] = lb_re
    lim_ref[...] = lb_im
    b_re = bre_ref[...]
    b_im = bim_ref[...]
    bbre_ref[...] = k_re[:, None, :] * b_re - k_im[:, None, :] * b_im
    bbim_ref[...] = k_re[:, None, :] * b_im + k_im[:, None, :] * b_re


def _ssm_prep(a_re, a_im, log_dt, b_re, b_im):
    g, n, gs = b_re.shape
    sds = jax.ShapeDtypeStruct
    return pl.pallas_call(
        _ssm_prep_kernel,
        out_shape=[sds((g, n), F32), sds((g, n), F32), sds((g, gs, n), F32), sds((g, gs, n), F32)],
        name="ssm_prep",
    )(a_re, a_im, log_dt.reshape(g, 1), jnp.swapaxes(b_re, 1, 2), jnp.swapaxes(b_im, 1, 2))


def _block_diag_halves(m):
    g, a, b = m.shape
    gh = g // 2
    tiled = jnp.tile(m.reshape(2, gh * a, b), (1, 1, gh))
    row_group = lax.broadcasted_iota(jnp.int32, tiled.shape, 1) // a
    col_group = lax.broadcasted_iota(jnp.int32, tiled.shape, 2) // b
    return jnp.where(row_group == col_group, tiled, 0.0)


def _ssm_y(u, scan_block, cc_ref, d_ref):
    ys = []
    for half in range(2):
        hc = jnp.concatenate([scan_block(half, jj) for jj in range(BLK_PER_HALF)], axis=1).astype(BF16)
        ys.append(_dot(hc, cc_ref[half]))
    y = jnp.concatenate(ys, axis=1) + d_ref[...] * u
    return jax.nn.gelu(y).astype(BF16)


def _ssm_prompt_kernel(u_ref, bb_ref, cc_ref, lam_ref, d_ref, h0_ref, z_ref, hout_ref,
                       x0_ref, x1_ref, s0_ref, s1_ref, hst_ref):
    tb, pitch = SSM_TB, SSM_PITCH
    x_bufs = (x0_ref, x1_ref)
    s_bufs = (s0_ref, s1_ref)

    @pl.when(pl.program_id(0) == 0)
    def _():
        hst_ref[...] = h0_ref[...]

    def b_proj(k):
        ub = u_ref[k * tb:(k + 1) * tb, :].astype(BF16)
        for half in range(2):
            x = _dot(ub[:, half * HALF_U:(half + 1) * HALF_U], bb_ref[half])
            for jj in range(BLK_PER_HALF):
                x_bufs[k % 2][pl.ds((half * BLK_PER_HALF + jj) * pitch, tb), :] = x[:, jj * LANES:(jj + 1) * LANES]

    lam = [lam_ref[g] for g in range(4)]
    carry = [hst_ref[g] for g in range(4)]

    def scan(k):
        for t in range(tb):
            for half in range(2):
                hr, hi = carry[2 * half], carry[2 * half + 1]
                lr, li = lam[2 * half], lam[2 * half + 1]
                rows_re = pl.ds((half * BLK_PER_HALF) * pitch + t, SUBLANES, stride=pitch)
                rows_im = pl.ds((half * BLK_PER_HALF + SUBLANES) * pitch + t, SUBLANES, stride=pitch)
                nr = lr * hr - li * hi + x_bufs[k % 2][rows_re, :]
                ni = lr * hi + li * hr + x_bufs[k % 2][rows_im, :]
                s_bufs[k % 2][rows_re, :] = nr
                s_bufs[k % 2][rows_im, :] = ni
                carry[2 * half], carry[2 * half + 1] = nr, ni

    def c_proj(k):
        z_ref[k * tb:(k + 1) * tb, :] = _ssm_y(
            u_ref[k * tb:(k + 1) * tb, :],
            lambda half, jj: s_bufs[k % 2][pl.ds((half * BLK_PER_HALF + jj) * pitch, tb), :], cc_ref, d_ref)

    b_proj(0)
    for k in range(SSM_NB):
        if k + 1 < SSM_NB:
            b_proj(k + 1)
        scan(k)
        c_proj(k)
    for g in range(4):
        hst_ref[g] = carry[g]
        hout_ref[g] = carry[g]


def _ssm_prompt(u, bb, cc, lam, d, h0, *, t):
    rows = SSM_NB * SSM_TB
    return pl.pallas_call(
        _ssm_prompt_kernel,
        grid=(t // rows,),
        in_specs=[pl.BlockSpec((rows, SSM_WIDTH), lambda i: (i, 0)), _const_spec(bb.shape), _const_spec(cc.shape),
                  _const_spec(lam.shape), _const_spec(d.shape), _const_spec(h0.shape)],
        out_specs=[pl.BlockSpec((rows, SSM_WIDTH), lambda i: (i, 0)),
                   pl.BlockSpec((4, SUBLANES, LANES), lambda i: (0, 0, 0))],
        out_shape=[jax.ShapeDtypeStruct((t, SSM_WIDTH), BF16), jax.ShapeDtypeStruct((4, SUBLANES, LANES), F32)],
        scratch_shapes=[pltpu.VMEM((2 * BLK_PER_HALF * SSM_PITCH, LANES), F32)] * 4
                       + [pltpu.VMEM((4, SUBLANES, LANES), F32)],
        compiler_params=_cparams(("arbitrary",)),
        name="ssm_prompt",
    )(u, bb, cc, lam, d, h0)


def _ssm_sample_kernel(u_ref, bb_ref, cc_ref, lre_ref, lim_ref, d_ref, h0re_ref, h0im_ref,
                       z_ref, hre_ref, him_ref, xs_ref, *, nb, ts):
    u = u_ref[...]
    ub = u.astype(BF16)
    for half in range(2):
        x = _dot(ub[:, half * HALF_U:(half + 1) * HALF_U], bb_ref[half])
        for jj in range(BLK_PER_HALF):
            xs_ref[half * BLK_PER_HALF + jj] = x[:, jj * LANES:(jj + 1) * LANES]
    for cb in range(SSM_CH // LANES):
        half, jj = divmod(cb, SUBLANES)
        j_re = half * BLK_PER_HALF + jj
        j_im = j_re + SUBLANES
        csl = slice(cb * LANES, (cb + 1) * LANES)
        lr = lre_ref[:, csl]
        li = lim_ref[:, csl]
        hr = h0re_ref[:, csl]
        hi = h0im_ref[:, csl]
        for t in range(ts):
            rows = pl.ds(t, nb, stride=ts)
            nr = lr * hr - li * hi + xs_ref[j_re, rows, :]
            ni = lr * hi + li * hr + xs_ref[j_im, rows, :]
            xs_ref[j_re, rows, :] = nr
            xs_ref[j_im, rows, :] = ni
            hr, hi = nr, ni
        hre_ref[:, csl] = hr
        him_ref[:, csl] = hi
    z_ref[...] = _ssm_y(u, lambda half, jj: xs_ref[half * BLK_PER_HALF + jj], cc_ref, d_ref)


def _ssm_sample(u, bb, cc, lre, lim, d, h0re, h0im, *, nb, ts, row0):
    nseq = h0re.shape[0]
    rows = nb * ts
    assert row0 % rows == 0
    st = pl.BlockSpec((nb, SSM_CH), lambda i: (i, 0))
    return pl.pallas_call(
        functools.partial(_ssm_sample_kernel, nb=nb, ts=ts),
        grid=(nseq // nb,),
        in_specs=[pl.BlockSpec((rows, SSM_WIDTH), lambda i: (i + row0 // rows, 0)), _const_spec(bb.shape),
                  _const_spec(cc.shape),
                  _const_spec(lre.shape), _const_spec(lim.shape), _const_spec(d.shape), st, st],
        out_specs=[pl.BlockSpec((rows, SSM_WIDTH), lambda i: (i, 0)), st, st],
        out_shape=[jax.ShapeDtypeStruct((nseq * ts, SSM_WIDTH), BF16),
                   jax.ShapeDtypeStruct((nseq, SSM_CH), F32), jax.ShapeDtypeStruct((nseq, SSM_CH), F32)],
        scratch_shapes=[pltpu.VMEM((2 * BLK_PER_HALF, rows, LANES), F32)],
        compiler_params=_cparams(("parallel",)),
        name="ssm_sample",
    )(u, bb, cc, lre, lim, d, h0re, h0im)


def _xattn_heads(q, mk, mv):
    outs = []
    for h in range(N_X_HEADS):
        sl = slice(h * X_HEAD_DIM, (h + 1) * X_HEAD_DIM)
        s = _dot_nt(q[:, sl], mk[:, sl])
        p = jnp.exp(s - jnp.max(s, axis=-1, keepdims=True))
        outs.append(_dot(p.astype(BF16), mv[:, sl]) / jnp.sum(p, axis=-1, keepdims=True))
    return jnp.concatenate(outs, axis=1)


def _merge_kernel(x1_ref, op_ref, os_ref, zp_ref, zs_ref, ga_ref, gb_ref, wup_ref, wglu_ref, wout_ref, gx_ref,
                  wxq_ref, mk_ref, mv_ref, x2_ref, oxp_ref, qxs_ref, *, n_first):
    is_first = pl.program_id(0) < n_first
    o = jnp.where(is_first, op_ref[...], os_ref[...])
    z = jnp.where(is_first, zp_ref[...], zs_ref[...])
    ya = _dot(o, wup_ref[...])
    glu = _dot(z, wglu_ref[...])
    yb = glu[:, :D_MODEL] * jax.nn.sigmoid(glu[:, D_MODEL:])
    merged = ga_ref[...].astype(F32) * ya + gb_ref[...].astype(F32) * yb
    x2 = x1_ref[...] + _dot(merged.astype(BF16), wout_ref[...])
    x2_ref[...] = x2
    hx = _rms(x2, gx_ref[...]).astype(BF16)
    qx = (_dot(hx, wxq_ref[...]) * (X_HEAD_DIM ** -0.5)).astype(BF16)
    ox = _xattn_heads(qx, mk_ref[...].astype(BF16), mv_ref[...].astype(BF16)).astype(BF16)

    @pl.when(is_first)
    def _():
        oxp_ref[...] = ox

    @pl.when(jnp.logical_not(is_first))
    def _():
        qxs_ref[...] = qx


def _merge(x1, o_p, o_s, z_p, z_s, ga, gb, wup, wglu, wout, gx, wxq, mk, mv, *, tm):
    n_first, n_second = o_p.shape[0] // tm, o_s.shape[0] // tm
    row = lambda w: _row_spec(tm, w)
    consts = [wup, wglu, wout, gx, wxq, mk, mv]
    return pl.pallas_call(
        functools.partial(_merge_kernel, n_first=n_first),
        grid=(n_first + n_second,),
        in_specs=[row(D_MODEL)] + _two_array_specs(n_first, tm, ATTN_WIDTH) + _two_array_specs(n_first, tm, SSM_WIDTH)
                 + [row(D_MODEL), row(D_MODEL)] + [_const_spec(c.shape) for c in consts],
        out_specs=[row(D_MODEL)] + _two_array_specs(n_first, tm, D_MODEL),
        out_shape=[jax.ShapeDtypeStruct(x1.shape, F32), jax.ShapeDtypeStruct((o_p.shape[0], D_MODEL), BF16),
                   jax.ShapeDtypeStruct((o_s.shape[0], D_MODEL), BF16)],
        compiler_params=_cparams(("arbitrary",)),
        name="merge",
    )(x1, o_p, o_s, z_p, z_s, ga, gb, *consts)


def _memkv_kernel(mem_ref, g_ref, wk_ref, wv_ref, mk_ref, mv_ref):
    mn = _rms(mem_ref[...], g_ref[...]).astype(BF16)
    mk_ref[...] = _dot(mn, wk_ref[...].astype(BF16))
    mv_ref[...] = _dot(mn, wv_ref[...].astype(BF16))


def _memkv(mem, g, wk, wv):
    sds = jax.ShapeDtypeStruct((mem.shape[0], D_MODEL), F32)
    return pl.pallas_call(_memkv_kernel, out_shape=[sds, sds], name="memkv",
                          compiler_params=pltpu.CompilerParams(vmem_limit_bytes=VMEM_LIMIT))(mem, g, wk, wv)


def kernel(x_prompt, x_sample, cache_win_k, cache_win_v, state_ssm_re, state_ssm_im, cache_mem_k, cache_mem_v, mem_prompt, g_ffn1, w_ffn1_in, w_ffn1_out, g_mix, w_in, attn_sinks, ssm_a_re, ssm_a_im, ssm_log_dt, ssm_b_re, ssm_b_im, ssm_c_re, ssm_c_im, ssm_d, w_attn_up, w_ssm_glu, w_out, g_xattn, g_mem, w_xq, w_xk, w_xv, w_xo, g_ffn2, w_ffn2_in, w_ffn2_out, g_final):
    assert x_prompt.shape[0] == 1 and g_ffn1.shape[0] == 1
    seq = x_prompt.shape[1]
    nseq, ts = x_sample.shape[0], x_sample.shape[1]
    past_len = seq
    l = 0
    vec = lambda g: g[l].reshape(1, -1)
    w1i, w1o = w_ffn1_in[l].astype(BF16), w_ffn1_out[l].astype(BF16)
    gf = g_final.reshape(1, -1)
    sinks = attn_sinks[l]

    lam_re, lam_im, bbt_re, bbt_im = _ssm_prep(ssm_a_re[l], ssm_a_im[l], ssm_log_dt[l], ssm_b_re[l], ssm_b_im[l])
    bb = jnp.concatenate([_block_diag_halves(bbt_re), _block_diag_halves(bbt_im)], axis=-1).astype(BF16)
    ct_re = jnp.swapaxes(ssm_c_re[l], 1, 2)
    ct_im = jnp.swapaxes(ssm_c_im[l], 1, 2)
    cc = jnp.concatenate([_block_diag_halves(ct_re), -_block_diag_halves(ct_im)], axis=1).astype(BF16)
    d_skip = ssm_d[l].reshape(1, -1)
    lre16 = lam_re.reshape(2, SUBLANES, LANES)
    lim16 = lam_im.reshape(2, SUBLANES, LANES)
    lam_tm = jnp.stack([lre16[0], lim16[0], lre16[1], lim16[1]])
    lre_row = lam_re.reshape(1, SSM_CH)
    lim_row = lam_im.reshape(1, SSM_CH)

    mk_p, mv_p = _memkv(mem_prompt[0], vec(g_mem), w_xk[l], w_xv[l])

    tm = TM_MIX
    t_s = nseq * ts
    assert seq % tm == 0 and t_s % tm == 0 and tm % ts == 0
    n_p, n_s = seq // tm, t_s // tm
    rope_tabs = _rope_tables(jnp.concatenate([jnp.arange(n_p) * tm, jnp.full((n_s,), past_len)]),
                             jnp.stack([jnp.arange(tm), jnp.arange(tm) % ts]))

    x1, win = _ffn_first(x_prompt[0], x_sample.reshape(t_s, D_MODEL), vec(g_ffn1), w1i, w1o, [w_in[l]], tm=TM_FFN)
    q, k, v, u, ga, gb = _inproj(x1, vec(g_mix), win, rope_tabs, tm=tm, n_first=n_p)

    o_p, wup, wglu, wout, wxq, wxo, w2i, w2o = _attn_prompt(
        sinks, q, k, v, [w_attn_up[l], w_ssm_glu[l], w_out[l], w_xq[l], w_xo[l], w_ffn2_in[l], w_ffn2_out[l]], t=seq)
    nk_p, nv_p = k[seq - WINDOW:seq], v[seq - WINDOW:seq]
    to_t = lambda c: jnp.transpose(c[l], (0, 2, 3, 1))
    o_s, nkt, nvt = _attn_sample(sinks, q, k, v, to_t(cache_win_k), to_t(cache_win_v), nb=8, ts=ts, row0=seq)
    nk_s, nv_s = jnp.transpose(nkt, (0, 3, 1, 2)), jnp.transpose(nvt, (0, 3, 1, 2))

    z_p, hout_p = _ssm_prompt(u, bb, cc, lam_tm, d_skip, jnp.zeros((4, SUBLANES, LANES), F32), t=seq)
    hre_p = jnp.concatenate([hout_p[0], hout_p[2]], axis=0)
    him_p = jnp.concatenate([hout_p[1], hout_p[3]], axis=0)
    z_s, hre_s, him_s = _ssm_sample(u, bb, cc, lre_row, lim_row, d_skip, state_ssm_re[l].reshape(nseq, SSM_CH),
                                    state_ssm_im[l].reshape(nseq, SSM_CH), nb=64, ts=ts, row0=seq)

    x2, ox_p, qx_s = _merge(x1, o_p, o_s, z_p, z_s, ga, gb, wup, wglu, wout, vec(g_xattn), wxq, mk_p, mv_p, tm=tm)
    y_p, y_s = _tail(x2, ox_p, qx_s, cache_mem_k[l], cache_mem_v[l], wxo, vec(g_ffn2), w2i, w2o, gf,
                     tm=TM_TAIL, ts=ts)

    kvshape = (1, 1, WINDOW, N_KV_HEADS, HEAD_DIM)
    stshape = (1, 1, N_SSM_GROUPS, SSM_STATE)
    memshape = (1, 1, N_MEM, N_X_HEADS, X_HEAD_DIM)
    return (y_p.reshape(1, seq, D_MODEL), y_s.reshape(nseq, ts, D_MODEL),
            nk_p.reshape(kvshape), nv_p.reshape(kvshape), hre_p.reshape(stshape), him_p.reshape(stshape),
            mk_p.reshape(memshape), mv_p.reshape(memshape),
            nk_s.reshape(1, nseq, WINDOW, N_KV_HEADS, HEAD_DIM), nv_s.reshape(1, nseq, WINDOW, N_KV_HEADS, HEAD_DIM),
            hre_s.reshape(1, nseq, N_SSM_GROUPS, SSM_STATE), him_s.reshape(1, nseq, N_SSM_GROUPS, SSM_STATE))
```

```python
import functools
import math

import jax
import jax.numpy as jnp
from jax import lax
from jax.experimental import pallas as pl
from jax.experimental.pallas import tpu as pltpu

F32 = jnp.float32
BF16 = jnp.bfloat16

D_MODEL = 1024
N_Q_HEADS = 8
N_KV_HEADS = 2
HEAD_DIM = 64
Q_PER_KV = N_Q_HEADS // N_KV_HEADS
ATTN_WIDTH = N_Q_HEADS * HEAD_DIM
KV_WIDTH = N_KV_HEADS * HEAD_DIM
WINDOW = 128
ROPE_THETA = 10000.0
SSM_WIDTH = D_MODEL // 2
SSM_GROUP = 16
N_SSM_GROUPS = SSM_WIDTH // SSM_GROUP
SSM_STATE = 64
N_MEM = 256
N_X_HEADS = 4
X_HEAD_DIM = D_MODEL // N_X_HEADS
D_FF = 2816
RMS_EPS = 1e-6
NEG_INF = -1e30
IN_SPLITS = (ATTN_WIDTH, KV_WIDTH, KV_WIDTH, SSM_WIDTH, D_MODEL, D_MODEL)
IN_WIDTH = sum(IN_SPLITS)
OFF_Q, OFF_K, OFF_V, OFF_U, OFF_GA, OFF_GB = (0, 512, 640, 768, 1280, 2304)

LANES = 128
SUBLANES = 8
BF16_SUBLANES = 16
VMEM_LIMIT = 60 * 1024 * 1024

SSM_CH = N_SSM_GROUPS * SSM_STATE
HALF_CH = SSM_CH // 2
HALF_U = SSM_WIDTH // 2
BLK_PER_HALF = 2 * HALF_CH // LANES
TM_FFN = 1024
TM_TAIL = 512
FFN_CHUNKS = ((0, 768), (768, 1536), (1536, 2304), (2304, D_FF))
TM_MIX = 1024
ATTN_QBLOCKS = 8
SSM_TB = 256
SSM_NB = 2
SSM_PITCH = SSM_TB + 4


def _cparams(sem):
    return pltpu.CompilerParams(dimension_semantics=sem, vmem_limit_bytes=VMEM_LIMIT)


def _const_spec(shape):
    nd = len(shape)
    return pl.BlockSpec(shape, lambda *_: (0,) * nd, pipeline_mode=pl.Buffered(1))


def _rms(x, g):
    return x * lax.rsqrt(jnp.mean(x * x, axis=-1, keepdims=True) + RMS_EPS) * g


def _dot(a, b):
    return jnp.dot(a, b, preferred_element_type=F32)


def _dot_nt(a, b):
    return lax.dot_general(a, b, (((1,), (1,)), ((), ())), preferred_element_type=F32)


def _row_spec(tm, width):
    return pl.BlockSpec((tm, width), lambda i: (i, 0))


def _two_array_specs(n_first, tm, width):
    return [pl.BlockSpec((tm, width), lambda i: (jnp.minimum(i, n_first - 1), 0)),
            pl.BlockSpec((tm, width), lambda i: (jnp.maximum(i - n_first, 0), 0))]


def _cast_stream_specs(weights, nsteps):
    in_specs, out_specs, out_shapes = [], [], []
    for w in weights:
        rows, cols = w.shape
        chunk = rows // nsteps
        assert chunk * nsteps == rows and chunk % BF16_SUBLANES == 0
        spec = pl.BlockSpec((chunk, cols), lambda i: (jnp.minimum(i, nsteps - 1), 0))
        in_specs.append(spec)
        out_specs.append(spec)
        out_shapes.append(jax.ShapeDtypeStruct(w.shape, BF16))
    return in_specs, out_specs, out_shapes


def _cast_chunks(src_refs, dst_refs):
    for src, dst in zip(src_refs, dst_refs, strict=True):
        dst[...] = src[...].astype(BF16)


def _store_by_step(is_first, first_ref, second_ref, value):
    @pl.when(is_first)
    def _():
        first_ref[...] = value

    @pl.when(jnp.logical_not(is_first))
    def _():
        second_ref[...] = value


def _ffn_residual(x, g_ref, wi_ref, wo_ref, between=()):
    h = _rms(x, g_ref[...]).astype(BF16)
    acc = None
    for c, (lo, hi) in enumerate(FFN_CHUNKS):
        a = _dot(h, wi_ref[:, lo:hi])
        b = _dot(h, wi_ref[:, D_FF + lo:D_FF + hi])
        part = _dot((a * jax.nn.sigmoid(a) * b).astype(BF16), wo_ref[lo:hi, :])
        acc = part if acc is None else acc + part
        if c < len(between):
            between[c]()
    return x + 0.5 * acc


def _ffn_first_kernel(xp_ref, xs_ref, g_ref, wi_ref, wo_ref, *rest, n_first, n_cast):
    cast_src, o_ref, cast_dst = rest[:n_cast], rest[n_cast], rest[n_cast + 1:]
    _cast_chunks(cast_src, cast_dst)
    x = jnp.where(pl.program_id(0) < n_first, xp_ref[...], xs_ref[...])
    o_ref[...] = _ffn_residual(x, g_ref, wi_ref, wo_ref)


def _ffn_first(xp, xs, g, wi, wo, cast_weights, *, tm):
    n_first, n_second = xp.shape[0] // tm, xs.shape[0] // tm
    c_in, c_out, c_shapes = _cast_stream_specs(cast_weights, n_first)
    return pl.pallas_call(
        functools.partial(_ffn_first_kernel, n_first=n_first, n_cast=len(cast_weights)),
        grid=(n_first + n_second,),
        in_specs=_two_array_specs(n_first, tm, D_MODEL)
                 + [_const_spec(g.shape), _const_spec(wi.shape), _const_spec(wo.shape)] + c_in,
        out_specs=[_row_spec(tm, D_MODEL)] + c_out,
        out_shape=[jax.ShapeDtypeStruct((xp.shape[0] + xs.shape[0], D_MODEL), F32)] + c_shapes,
        compiler_params=_cparams(("arbitrary",)),
        name="ffn",
    )(xp, xs, g, wi, wo, *cast_weights)


def _xattn_seq_probs(qb, mk):
    ts = qb.shape[0]
    krows = N_MEM * N_X_HEADS
    row = lax.broadcasted_iota(jnp.int32, (ts * N_X_HEADS, krows), 0)
    col = lax.broadcasted_iota(jnp.int32, (ts * N_X_HEADS, krows), 1)
    same_head = (row // ts) == (col & (N_X_HEADS - 1))
    qs = jnp.concatenate([qb[:, h * X_HEAD_DIM:(h + 1) * X_HEAD_DIM] for h in range(N_X_HEADS)], axis=0)
    s = jnp.where(same_head, _dot_nt(qs, mk.reshape(krows, X_HEAD_DIM).astype(BF16)), NEG_INF)
    p = jnp.exp(s - jnp.max(s, axis=-1, keepdims=True))
    return p.astype(BF16), jnp.sum(p, axis=-1, keepdims=True)


def _xattn_seq_values(p, denom, mv):
    ts = p.shape[0] // N_X_HEADS
    o = (_dot(p, mv.reshape(N_MEM * N_X_HEADS, X_HEAD_DIM).astype(BF16)) / denom).astype(BF16)
    return jnp.concatenate([o[h * ts:(h + 1) * ts] for h in range(N_X_HEADS)], axis=1)


def _tail_kernel(x_ref, oxp_ref, qxs_ref, mk_ref, mv_ref, wxo_ref, g_ref, wi_ref, wo_ref, gf_ref, yp_ref, ys_ref,
                 oxs_ref, *, n_first, nb, ts, tm):
    i = pl.program_id(0)
    is_first = i < n_first
    t_s = qxs_ref.shape[0]
    pair = 2 * ts

    @pl.when(i == 0)
    def _():
        oxs_ref[...] = jnp.zeros(oxs_ref.shape, oxs_ref.dtype)

    row0 = pl.multiple_of(jnp.maximum(i - n_first, 0) * tm, tm)
    ox = jnp.where(is_first, oxp_ref[...], oxs_ref[pl.ds(row0, tm), :])
    x = x_ref[...] + _dot(ox, wxo_ref[...])

    seq0 = jnp.minimum(i, n_first - 1) * nb
    pending = {}

    def probs_stage(j):
        src = pl.multiple_of((seq0 + 2 * j) * ts, pair)
        q2 = qxs_ref[pl.ds(src, pair), :]
        pending[j] = (src, [_xattn_seq_probs(q2[b * ts:(b + 1) * ts], mk_ref[2 * j + b]) for b in range(2)])

    def values_stage(j):
        src, probs = pending.pop(j)
        o2 = jnp.concatenate([_xattn_seq_values(p, denom, mv_ref[2 * j + b]) for b, (p, denom) in enumerate(probs)],
                             axis=0)
        dst = pl.multiple_of(jnp.where(is_first, src, t_s), pair)
        oxs_ref[pl.ds(dst, pair), :] = o2

    n_pairs = nb // 2
    assert 1 <= n_pairs < len(FFN_CHUNKS)

    def after_chunk(c):
        def run():
            values_stage(c)
            if c + 1 < n_pairs:
                probs_stage(c + 1)
        return run

    probs_stage(0)
    y = _rms(_ffn_residual(x, g_ref, wi_ref, wo_ref, between=[after_chunk(c) for c in range(n_pairs)]), gf_ref[...])
    _store_by_step(is_first, yp_ref, ys_ref, y)


def _tail(x2, ox_p, qx_s, mk, mv, wxo, g, wi, wo, gf, *, tm, ts):
    t_p, t_s, nseq = ox_p.shape[0], qx_s.shape[0], mk.shape[0]
    n_first, n_second = t_p // tm, t_s // tm
    nb = nseq // n_first
    assert nb * n_first == nseq and nb % 2 == 0 and nseq * ts == t_s
    mem = pl.BlockSpec((nb, N_MEM, N_X_HEADS, X_HEAD_DIM), lambda i: (jnp.minimum(i, n_first - 1), 0, 0, 0))
    return pl.pallas_call(
        functools.partial(_tail_kernel, n_first=n_first, nb=nb, ts=ts, tm=tm),
        grid=(n_first + n_second,),
        in_specs=[_row_spec(tm, D_MODEL), _two_array_specs(n_first, tm, D_MODEL)[0], _const_spec(qx_s.shape), mem, mem]
                 + [_const_spec(c.shape) for c in (wxo, g, wi, wo, gf)],
        out_specs=_two_array_specs(n_first, tm, D_MODEL),
        out_shape=[jax.ShapeDtypeStruct((t_p, D_MODEL), F32), jax.ShapeDtypeStruct((t_s, D_MODEL), F32)],
        scratch_shapes=[pltpu.VMEM((t_s + 2 * ts, D_MODEL), BF16)],
        compiler_params=_cparams(("arbitrary",)),
        name="tail",
    )(x2, ox_p, qx_s, mk, mv, wxo, g, wi, wo, gf)


def _inproj_kernel(x_ref, g_ref, w_ref, ca_ref, sa_ref, cb_ref, sb_ref, q_ref, k_ref, v_ref, u_ref, ga_ref, gb_ref):
    h = _rms(x_ref[...], g_ref[...]).astype(BF16)
    ca, sa, cb, sb = ca_ref[0], sa_ref[0], cb_ref[0], sb_ref[0]
    cos = ca * cb - sa * sb
    sin = sa * cb + ca * sb
    lane = lax.broadcasted_iota(jnp.int32, cos.shape, 1)
    first_half = (lane & (HEAD_DIM - 1)) < (HEAD_DIM // 2)

    def rope(xc):
        rot = jnp.where(first_half, pltpu.roll(xc, LANES - HEAD_DIM // 2, 1), pltpu.roll(xc, HEAD_DIM // 2, 1))
        return xc * cos + rot * sin

    ga_ref[...] = jax.nn.sigmoid(_dot(h, w_ref[:, OFF_GA:OFF_GB])).astype(BF16)
    gb_ref[...] = jax.nn.sigmoid(_dot(h, w_ref[:, OFF_GB:])).astype(BF16)
    yq = _dot(h, w_ref[:, OFF_Q:OFF_K])
    for c in range(ATTN_WIDTH // LANES):
        q_ref[:, c * LANES:(c + 1) * LANES] = (rope(yq[:, c * LANES:(c + 1) * LANES]) * (HEAD_DIM ** -0.5)).astype(BF16)
    ykvu = _dot(h, w_ref[:, OFF_K:OFF_GA])
    k_ref[...] = rope(ykvu[:, :KV_WIDTH])
    v_ref[...] = ykvu[:, KV_WIDTH:2 * KV_WIDTH]
    u_ref[...] = ykvu[:, 2 * KV_WIDTH:]


def _inproj(x, g, w, rope_tabs, *, tm, n_first):
    t = x.shape[0]
    row = lambda wd: _row_spec(tm, wd)
    step_tab = pl.BlockSpec((1, 1, LANES), lambda i: (i, 0, 0))
    row_tab = pl.BlockSpec((1, tm, LANES), lambda i: (jnp.minimum(i // n_first, 1), 0, 0))
    widths = (ATTN_WIDTH, KV_WIDTH, KV_WIDTH, SSM_WIDTH, D_MODEL, D_MODEL)
    dtypes = (BF16, F32, F32, F32, BF16, BF16)
    return pl.pallas_call(
        _inproj_kernel,
        grid=(t // tm,),
        in_specs=[row(D_MODEL), _const_spec((1, D_MODEL)), _const_spec((D_MODEL, IN_WIDTH)), step_tab, step_tab,
                  row_tab, row_tab],
        out_specs=[row(wd) for wd in widths],
        out_shape=[jax.ShapeDtypeStruct((t, wd), dt) for wd, dt in zip(widths, dtypes)],
        compiler_params=_cparams(("parallel",)),
        name="inproj",
    )(x, g, w, *rope_tabs)


def _rope_tables(step_pos, row_pos):
    half = HEAD_DIM // 2
    inv = ROPE_THETA ** (-jnp.arange(half, dtype=F32) / half)
    sign = jnp.concatenate([-jnp.ones((half,), F32), jnp.ones((half,), F32)])

    def tabs(pos):
        ang = pos.astype(F32)[..., None] * inv
        cos = jnp.tile(jnp.cos(ang), LANES // half)
        sin = jnp.tile(jnp.tile(jnp.sin(ang), 2) * sign, LANES // HEAD_DIM)
        return cos, sin

    ca, sa = tabs(step_pos[:, None])
    cb, sb = tabs(row_pos)
    return ca, sa, cb, sb


def _attn_prompt_kernel(sinks_ref, q_ref, kp_ref, kc_ref, vp_ref, vc_ref, *rest, n_cast):
    cast_src, o_ref, cast_dst = rest[:n_cast], rest[n_cast], rest[n_cast + 1:]
    _cast_chunks(cast_src, cast_dst)
    i = pl.program_id(0)
    r = lax.broadcasted_iota(jnp.int32, (WINDOW, 2 * WINDOW), 0)
    c = lax.broadcasted_iota(jnp.int32, (WINDOW, 2 * WINDOW), 1)
    band = (c >= r) & (c <= WINDOW + r)
    first_mask = band & ((c >= WINDOW) | (i > 0))
    kcat = jnp.concatenate([kp_ref[...], kc_ref[...]], axis=0)
    vcat = jnp.concatenate([vp_ref[...], vc_ref[...]], axis=0)
    kswap = pltpu.roll(kcat, HEAD_DIM, 1)
    vswap = pltpu.roll(vcat, HEAD_DIM, 1)
    lo = lax.broadcasted_iota(jnp.int32, kcat.shape, 1) < HEAD_DIM
    lo_q = lax.broadcasted_iota(jnp.int32, (WINDOW, LANES), 1) < HEAD_DIM
    ones = jnp.ones(kcat.shape, F32)
    for kv in range(N_KV_HEADS):
        k_own, k_other = (kcat, kswap) if kv == 0 else (kswap, kcat)
        v_own, v_other = (vcat, vswap) if kv == 0 else (vswap, vcat)
        k_half = (jnp.where(lo, k_own, 0.0).astype(BF16), jnp.where(lo, 0.0, k_other).astype(BF16))
        rhs_half = (jnp.concatenate([jnp.where(lo, v_own, 0.0), jnp.where(lo, ones, 0.0)], axis=1).astype(BF16),
                    jnp.concatenate([jnp.where(lo, 0.0, v_other), jnp.where(lo, 0.0, ones)], axis=1).astype(BF16))
        for j in range(ATTN_QBLOCKS):
            qrows = slice(j * WINDOW, (j + 1) * WINDOW)
            krows = slice(j * WINDOW, (j + 2) * WINDOW)
            mask = first_mask if j == 0 else band
            for c in range(Q_PER_KV // 2):
                tile = kv * (Q_PER_KV // 2) + c
                qt = q_ref[qrows, tile * LANES:(tile + 1) * LANES]
                acc = None
                sink_terms = []
                for side in range(2):
                    s = jnp.where(mask, _dot_nt(qt, k_half[side][krows]), NEG_INF)
                    sink = sinks_ref[2 * tile + side]
                    m = jnp.maximum(jnp.max(s, axis=-1, keepdims=True), sink)
                    pv = _dot(jnp.exp((s - m).astype(BF16)), rhs_half[side][krows])
                    acc = pv if acc is None else acc + pv
                    sink_terms.append(jnp.exp(sink - m))
                denom = acc[:, LANES:] + jnp.where(lo_q, sink_terms[0], sink_terms[1])
                o_ref[qrows, tile * LANES:(tile + 1) * LANES] = (acc[:, :LANES] / denom).astype(BF16)


def _attn_prompt(sinks, q, k, v, cast_weights, *, t):
    rows = ATTN_QBLOCKS * WINDOW
    cur = lambda w: pl.BlockSpec((rows, w), lambda i: (i, 0))
    prev = lambda w: pl.BlockSpec((WINDOW, w), lambda i: (jnp.maximum(i * ATTN_QBLOCKS - 1, 0), 0))
    c_in, c_out, c_shapes = _cast_stream_specs(cast_weights, t // rows)
    return pl.pallas_call(
        functools.partial(_attn_prompt_kernel, n_cast=len(cast_weights)),
        grid=(t // rows,),
        in_specs=[pl.BlockSpec(memory_space=pltpu.SMEM), cur(ATTN_WIDTH), prev(KV_WIDTH), cur(KV_WIDTH),
                  prev(KV_WIDTH), cur(KV_WIDTH)] + c_in,
        out_specs=[cur(ATTN_WIDTH)] + c_out,
        out_shape=[jax.ShapeDtypeStruct((t, ATTN_WIDTH), BF16)] + c_shapes,
        compiler_params=_cparams(("parallel",)),
        name="attn_prompt",
    )(sinks, q, k, k, v, v, *cast_weights)


def _attn_sample_kernel(sinks_ref, q_ref, k_ref, v_ref, ckt_ref, cvt_ref, o_ref, nkt_ref, nvt_ref, *, nb, ts):
    rows = nb * ts
    q = q_ref[...]
    pad = jnp.zeros((LANES - rows, KV_WIDTH), F32)
    knt = jnp.concatenate([k_ref[...], pad], axis=0).T
    vnt = jnp.concatenate([v_ref[...], pad], axis=0).T
    lane = lax.broadcasted_iota(jnp.int32, (HEAD_DIM, WINDOW), 1)
    is_new = lane >= WINDOW - ts
    for b in range(nb):
        knt_b = pltpu.roll(knt, (WINDOW - ts - b * ts) % LANES, 1)
        vnt_b = pltpu.roll(vnt, (WINDOW - ts - b * ts) % LANES, 1)
        for kv in range(N_KV_HEADS):
            ksl = slice(kv * HEAD_DIM, (kv + 1) * HEAD_DIM)
            nkt_ref[b, kv] = jnp.where(is_new, knt_b[ksl], pltpu.roll(ckt_ref[b, kv], WINDOW - ts, 1))
            nvt_ref[b, kv] = jnp.where(is_new, vnt_b[ksl], pltpu.roll(cvt_ref[b, kv], WINDOW - ts, 1))

    knt = knt[:, :rows].astype(BF16)
    vnt = vnt[:, :rows].astype(BF16)
    grows = Q_PER_KV * rows
    row1 = lax.broadcasted_iota(jnp.int32, (grows, nb * WINDOW), 0)
    col1 = lax.broadcasted_iota(jnp.int32, (grows, nb * WINDOW), 1)
    rseq1 = (row1 % rows) // ts
    mask_ctx = (rseq1 == col1 // WINDOW) & (col1 % WINDOW >= row1 % ts)
    row2 = lax.broadcasted_iota(jnp.int32, (grows, rows), 0)
    col2 = lax.broadcasted_iota(jnp.int32, (grows, rows), 1)
    mask_new = ((row2 % rows) // ts == col2 // ts) & (col2 % ts <= row2 % ts)
    ghead = lax.broadcasted_iota(jnp.int32, (grows, 1), 0) // rows
    for kv in range(N_KV_HEADS):
        ksl = slice(kv * HEAD_DIM, (kv + 1) * HEAD_DIM)
        qg = jnp.concatenate([q[:, (kv * Q_PER_KV + g) * HEAD_DIM:(kv * Q_PER_KV + g + 1) * HEAD_DIM]
                              for g in range(Q_PER_KV)], axis=0)
        kctx = jnp.concatenate([ckt_ref[b, kv] for b in range(nb)], axis=1).astype(BF16)
        vctx = jnp.concatenate([cvt_ref[b, kv] for b in range(nb)], axis=1).astype(BF16)
        s1 = jnp.where(mask_ctx, _dot(qg, kctx), NEG_INF)
        s2 = jnp.where(mask_new, _dot(qg, knt[ksl]), NEG_INF)
        sink = jnp.zeros((grows, 1), F32)
        for g in range(Q_PER_KV):
            sink = jnp.where(ghead == g, sinks_ref[kv * Q_PER_KV + g], sink)
        m = jnp.maximum(jnp.maximum(jnp.max(s1, axis=-1, keepdims=True), jnp.max(s2, axis=-1, keepdims=True)), sink)
        p1 = jnp.exp(s1 - m)
        p2 = jnp.exp(s2 - m)
        denom = jnp.sum(p1, axis=-1, keepdims=True) + jnp.sum(p2, axis=-1, keepdims=True) + jnp.exp(sink - m)
        o = (_dot_nt(p1.astype(BF16), vctx) + _dot_nt(p2.astype(BF16), vnt[ksl])) / denom
        for g in range(Q_PER_KV):
            h = kv * Q_PER_KV + g
            o_ref[:, h * HEAD_DIM:(h + 1) * HEAD_DIM] = o[g * rows:(g + 1) * rows].astype(BF16)


def _attn_sample(sinks, q, k, v, ckt, cvt, *, nb, ts, row0):
    nseq = ckt.shape[0]
    rows = nb * ts
    assert rows <= LANES and row0 % rows == 0
    row_in = lambda w: pl.BlockSpec((rows, w), lambda i: (i + row0 // rows, 0))
    win = pl.BlockSpec((nb, N_KV_HEADS, HEAD_DIM, WINDOW), lambda i: (i, 0, 0, 0))
    win_shape = jax.ShapeDtypeStruct((nseq, N_KV_HEADS, HEAD_DIM, WINDOW), F32)
    return pl.pallas_call(
        functools.partial(_attn_sample_kernel, nb=nb, ts=ts),
        grid=(nseq // nb,),
        in_specs=[pl.BlockSpec(memory_space=pltpu.SMEM), row_in(ATTN_WIDTH), row_in(KV_WIDTH), row_in(KV_WIDTH),
                  win, win],
        out_specs=[_row_spec(rows, ATTN_WIDTH), win, win],
        out_shape=[jax.ShapeDtypeStruct((nseq * ts, ATTN_WIDTH), BF16), win_shape, win_shape],
        compiler_params=_cparams(("parallel",)),
        name="attn_sample",
    )(sinks, q, k, v, ckt, cvt)


def _ssm_prep_kernel(are_ref, aim_ref, ldt_ref, bre_ref, bim_ref, lre_ref, lim_ref, bbre_ref, bbim_ref):
    a_re = are_ref[...]
    a_im = aim_ref[...]
    dt = jnp.exp(ldt_ref[...])
    mag = jnp.exp(a_re * dt)
    lb_re = mag * jnp.cos(a_im * dt)
    lb_im = mag * jnp.sin(a_im * dt)
    den = a_re * a_re + a_im * a_im
    nr = lb_re - 1.0
    ni = lb_im
    k_re = (nr * a_re + ni * a_im) / den
    k_im = (ni * a_re - nr * a_im) / den
    lre_ref[...] = lb_re
    lim_ref[...] = lb_im
    b_re = bre_ref[...]
    b_im = bim_ref[...]
    bbre_ref[...] = k_re[:, None, :] * b_re - k_im[:, None, :] * b_im
    bbim_ref[...] = k_re[:, None, :] * b_im + k_im[:, None, :] * b_re


def _ssm_prep(a_re, a_im, log_dt, b_re, b_im):
    g, n, gs = b_re.shape
    sds = jax.ShapeDtypeStruct
    return pl.pallas_call(
        _ssm_prep_kernel,
        out_shape=[sds((g, n), F32), sds((g, n), F32), sds((g, gs, n), F32), sds((g, gs, n), F32)],
        name="ssm_prep",
    )(a_re, a_im, log_dt.reshape(g, 1), jnp.swapaxes(b_re, 1, 2), jnp.swapaxes(b_im, 1, 2))


def _block_diag_halves(m):
    g, a, b = m.shape
    gh = g // 2
    tiled = jnp.tile(m.reshape(2, gh * a, b), (1, 1, gh))
    row_group = lax.broadcasted_iota(jnp.int32, tiled.shape, 1) // a
    col_group = lax.broadcasted_iota(jnp.int32, tiled.shape, 2) // b
    return jnp.where(row_group == col_group, tiled, 0.0)


def _ssm_y(u, scan_block, cc_ref, d_ref):
    ys = []
    for half in range(2):
        hc = jnp.concatenate([scan_block(half, jj) for jj in range(BLK_PER_HALF)], axis=1).astype(BF16)
        ys.append(_dot(hc, cc_ref[half]))
    y = jnp.concatenate(ys, axis=1) + d_ref[...] * u
    return jax.nn.gelu(y).astype(BF16)


def _ssm_prompt_kernel(u_ref, bb_ref, cc_ref, lam_ref, d_ref, h0_ref, z_ref, hout_ref,
                       x0_ref, x1_ref, s0_ref, s1_ref, hst_ref):
    tb, pitch = SSM_TB, SSM_PITCH
    x_bufs = (x0_ref, x1_ref)
    s_bufs = (s0_ref, s1_ref)

    @pl.when(pl.program_id(0) == 0)
    def _():
        hst_ref[...] = h0_ref[...]

    def b_proj(k):
        ub = u_ref[k * tb:(k + 1) * tb, :].astype(BF16)
        for half in range(2):
            x = _dot(ub[:, half * HALF_U:(half + 1) * HALF_U], bb_ref[half])
            for jj in range(BLK_PER_HALF):
                x_bufs[k % 2][pl.ds((half * BLK_PER_HALF + jj) * pitch, tb), :] = x[:, jj * LANES:(jj + 1) * LANES]

    lam = [lam_ref[g] for g in range(4)]
    carry = [hst_ref[g] for g in range(4)]

    def scan(k):
        for t in range(tb):
            for half in range(2):
                hr, hi = carry[2 * half], carry[2 * half + 1]
                lr, li = lam[2 * half], lam[2 * half + 1]
                rows_re = pl.ds((half * BLK_PER_HALF) * pitch + t, SUBLANES, stride=pitch)
                rows_im = pl.ds((half * BLK_PER_HALF + SUBLANES) * pitch + t, SUBLANES, stride=pitch)
                nr = lr * hr - li * hi + x_bufs[k % 2][rows_re, :]
                ni = lr * hi + li * hr + x_bufs[k % 2][rows_im, :]
                s_bufs[k % 2][rows_re, :] = nr
                s_bufs[k % 2][rows_im, :] = ni
                carry[2 * half], carry[2 * half + 1] = nr, ni

    def c_proj(k):
        z_ref[k * tb:(k + 1) * tb, :] = _ssm_y(
            u_ref[k * tb:(k + 1) * tb, :],
            lambda half, jj: s_bufs[k % 2][pl.ds((half * BLK_PER_HALF + jj) * pitch, tb), :], cc_ref, d_ref)

    b_proj(0)
    for k in range(SSM_NB):
        if k + 1 < SSM_NB:
            b_proj(k + 1)
        scan(k)
        c_proj(k)
    for g in range(4):
        hst_ref[g] = carry[g]
        hout_ref[g] = carry[g]


def _ssm_prompt(u, bb, cc, lam, d, h0, *, t):
    rows = SSM_NB * SSM_TB
    return pl.pallas_call(
        _ssm_prompt_kernel,
        grid=(t // rows,),
        in_specs=[pl.BlockSpec((rows, SSM_WIDTH), lambda i: (i, 0)), _const_spec(bb.shape), _const_spec(cc.shape),
                  _const_spec(lam.shape), _const_spec(d.shape), _const_spec(h0.shape)],
        out_specs=[pl.BlockSpec((rows, SSM_WIDTH), lambda i: (i, 0)),
                   pl.BlockSpec((4, SUBLANES, LANES), lambda i: (0, 0, 0))],
        out_shape=[jax.ShapeDtypeStruct((t, SSM_WIDTH), BF16), jax.ShapeDtypeStruct((4, SUBLANES, LANES), F32)],
        scratch_shapes=[pltpu.VMEM((2 * BLK_PER_HALF * SSM_PITCH, LANES), F32)] * 4
                       + [pltpu.VMEM((4, SUBLANES, LANES), F32)],
        compiler_params=_cparams(("arbitrary",)),
        name="ssm_prompt",
    )(u, bb, cc, lam, d, h0)


def _ssm_sample_kernel(u_ref, bb_ref, cc_ref, lre_ref, lim_ref, d_ref, h0re_ref, h0im_ref,
                       z_ref, hre_ref, him_ref, xs_ref, *, nb, ts):
    u = u_ref[...]
    ub = u.astype(BF16)
    for half in range(2):
        x = _dot(ub[:, half * HALF_U:(half + 1) * HALF_U], bb_ref[half])
        for jj in range(BLK_PER_HALF):
            xs_ref[half * BLK_PER_HALF + jj] = x[:, jj * LANES:(jj + 1) * LANES]
    for cb in range(SSM_CH // LANES):
        half, jj = divmod(cb, SUBLANES)
        j_re = half * BLK_PER_HALF + jj
        j_im = j_re + SUBLANES
        csl = slice(cb * LANES, (cb + 1) * LANES)
        lr = lre_ref[:, csl]
        li = lim_ref[:, csl]
        hr = h0re_ref[:, csl]
        hi = h0im_ref[:, csl]
        for t in range(ts):
            rows = pl.ds(t, nb, stride=ts)
            nr = lr * hr - li * hi + xs_ref[j_re, rows, :]
            ni = lr * hi + li * hr + xs_ref[j_im, rows, :]
            xs_ref[j_re, rows, :] = nr
            xs_ref[j_im, rows, :] = ni
            hr, hi = nr, ni
        hre_ref[:, csl] = hr
        him_ref[:, csl] = hi
    z_ref[...] = _ssm_y(u, lambda half, jj: xs_ref[half * BLK_PER_HALF + jj], cc_ref, d_ref)


def _ssm_sample(u, bb, cc, lre, lim, d, h0re, h0im, *, nb, ts, row0):
    nseq = h0re.shape[0]
    rows = nb * ts
    assert row0 % rows == 0
    st = pl.BlockSpec((nb, SSM_CH), lambda i: (i, 0))
    return pl.pallas_call(
        functools.partial(_ssm_sample_kernel, nb=nb, ts=ts),
        grid=(nseq // nb,),
        in_specs=[pl.BlockSpec((rows, SSM_WIDTH), lambda i: (i + row0 // rows, 0)), _const_spec(bb.shape),
                  _const_spec(cc.shape),
                  _const_spec(lre.shape), _const_spec(lim.shape), _const_spec(d.shape), st, st],
        out_specs=[pl.BlockSpec((rows, SSM_WIDTH), lambda i: (i, 0)), st, st],
        out_shape=[jax.ShapeDtypeStruct((nseq * ts, SSM_WIDTH), BF16),
                   jax.ShapeDtypeStruct((nseq, SSM_CH), F32), jax.ShapeDtypeStruct((nseq, SSM_CH), F32)],
        scratch_shapes=[pltpu.VMEM((2 * BLK_PER_HALF, rows, LANES), F32)],
        compiler_params=_cparams(("parallel",)),
        name="ssm_sample",
    )(u, bb, cc, lre, lim, d, h0re, h0im)


def _xattn_heads(q, mk, mv):
    outs = []
    for h in range(N_X_HEADS):
        sl = slice(h * X_HEAD_DIM, (h + 1) * X_HEAD_DIM)
        s = _dot_nt(q[:, sl], mk[:, sl])
        p = jnp.exp(s - jnp.max(s, axis=-1, keepdims=True))
        outs.append(_dot(p.astype(BF16), mv[:, sl]) / jnp.sum(p, axis=-1, keepdims=True))
    return jnp.concatenate(outs, axis=1)


def _merge_kernel(x1_ref, op_ref, os_ref, zp_ref, zs_ref, ga_ref, gb_ref, wup_ref, wglu_ref, wout_ref, gx_ref,
                  wxq_ref, mk_ref, mv_ref, x2_ref, oxp_ref, qxs_ref, *, n_first):
    is_first = pl.program_id(0) < n_first
    o = jnp.where(is_first, op_ref[...], os_ref[...])
    z = jnp.where(is_first, zp_ref[...], zs_ref[...])
    ya = _dot(o, wup_ref[...])
    glu = _dot(z, wglu_ref[...])
    yb = glu[:, :D_MODEL] * jax.nn.sigmoid(glu[:, D_MODEL:])
    merged = ga_ref[...].astype(F32) * ya + gb_ref[...].astype(F32) * yb
    x2 = x1_ref[...] + _dot(merged.astype(BF16), wout_ref[...])
    x2_ref[...] = x2
    hx = _rms(x2, gx_ref[...]).astype(BF16)
    qx = (_dot(hx, wxq_ref[...]) * (X_HEAD_DIM ** -0.5)).astype(BF16)
    ox = _xattn_heads(qx, mk_ref[...].astype(BF16), mv_ref[...].astype(BF16)).astype(BF16)

    @pl.when(is_first)
    def _():
        oxp_ref[...] = ox

    @pl.when(jnp.logical_not(is_first))
    def _():
        qxs_ref[...] = qx


def _merge(x1, o_p, o_s, z_p, z_s, ga, gb, wup, wglu, wout, gx, wxq, mk, mv, *, tm):
    n_first, n_second = o_p.shape[0] // tm, o_s.shape[0] // tm
    row = lambda w: _row_spec(tm, w)
    consts = [wup, wglu, wout, gx, wxq, mk, mv]
    return pl.pallas_call(
        functools.partial(_merge_kernel, n_first=n_first),
        grid=(n_first + n_second,),
        in_specs=[row(D_MODEL)] + _two_array_specs(n_first, tm, ATTN_WIDTH) + _two_array_specs(n_first, tm, SSM_WIDTH)
                 + [row(D_MODEL), row(D_MODEL)] + [_const_spec(c.shape) for c in consts],
        out_specs=[row(D_MODEL)] + _two_array_specs(n_first, tm, D_MODEL),
        out_shape=[jax.ShapeDtypeStruct(x1.shape, F32), jax.ShapeDtypeStruct((o_p.shape[0], D_MODEL), BF16),
                   jax.ShapeDtypeStruct((o_s.shape[0], D_MODEL), BF16)],
        compiler_params=_cparams(("arbitrary",)),
        name="merge",
    )(x1, o_p, o_s, z_p, z_s, ga, gb, *consts)


def _memkv_kernel(mem_ref, g_ref, wk_ref, wv_ref, mk_ref, mv_ref):
    mn = _rms(mem_ref[...], g_ref[...]).astype(BF16)
    mk_ref[...] = _dot(mn, wk_ref[...].astype(BF16))
    mv_ref[...] = _dot(mn, wv_ref[...].astype(BF16))


def _memkv(mem, g, wk, wv):
    sds = jax.ShapeDtypeStruct((mem.shape[0], D_MODEL), F32)
    return pl.pallas_call(_memkv_kernel, out_shape=[sds, sds], name="memkv",
                          compiler_params=pltpu.CompilerParams(vmem_limit_bytes=VMEM_LIMIT))(mem, g, wk, wv)


def kernel(x_prompt, x_sample, cache_win_k, cache_win_v, state_ssm_re, state_ssm_im, cache_mem_k, cache_mem_v, mem_prompt, g_ffn1, w_ffn1_in, w_ffn1_out, g_mix, w_in, attn_sinks, ssm_a_re, ssm_a_im, ssm_log_dt, ssm_b_re, ssm_b_im, ssm_c_re, ssm_c_im, ssm_d, w_attn_up, w_ssm_glu, w_out, g_xattn, g_mem, w_xq, w_xk, w_xv, w_xo, g_ffn2, w_ffn2_in, w_ffn2_out, g_final):
    assert x_prompt.shape[0] == 1 and g_ffn1.shape[0] == 1
    seq = x_prompt.shape[1]
    nseq, ts = x_sample.shape[0], x_sample.shape[1]
    past_len = seq
    l = 0
    vec = lambda g: g[l].reshape(1, -1)
    w1i, w1o = w_ffn1_in[l].astype(BF16), w_ffn1_out[l].astype(BF16)
    gf = g_final.reshape(1, -1)
    sinks = attn_sinks[l]

    lam_re, lam_im, bbt_re, bbt_im = _ssm_prep(ssm_a_re[l], ssm_a_im[l], ssm_log_dt[l], ssm_b_re[l], ssm_b_im[l])
    bb = jnp.concatenate([_block_diag_halves(bbt_re), _block_diag_halves(bbt_im)], axis=-1).astype(BF16)
    ct_re = jnp.swapaxes(ssm_c_re[l], 1, 2)
    ct_im = jnp.swapaxes(ssm_c_im[l], 1, 2)
    cc = jnp.concatenate([_block_diag_halves(ct_re), -_block_diag_halves(ct_im)], axis=1).astype(BF16)
    d_skip = ssm_d[l].reshape(1, -1)
    lre16 = lam_re.reshape(2, SUBLANES, LANES)
    lim16 = lam_im.reshape(2, SUBLANES, LANES)
    lam_tm = jnp.stack([lre16[0], lim16[0], lre16[1], lim16[1]])
    lre_row = lam_re.reshape(1, SSM_CH)
    lim_row = lam_im.reshape(1, SSM_CH)

    mk_p, mv_p = _memkv(mem_prompt[0], vec(g_mem), w_xk[l], w_xv[l])

    tm = TM_MIX
    t_s = nseq * ts
    assert seq % tm == 0 and t_s % tm == 0 and tm % ts == 0
    n_p, n_s = seq // tm, t_s // tm
    rope_tabs = _rope_tables(jnp.concatenate([jnp.arange(n_p) * tm, jnp.full((n_s,), past_len)]),
                             jnp.stack([jnp.arange(tm), jnp.arange(tm) % ts]))

    x1, win = _ffn_first(x_prompt[0], x_sample.reshape(t_s, D_MODEL), vec(g_ffn1), w1i, w1o, [w_in[l]], tm=TM_FFN)
    q, k, v, u, ga, gb = _inproj(x1, vec(g_mix), win, rope_tabs, tm=tm, n_first=n_p)

    o_p, wup, wglu, wout, wxq, wxo, w2i, w2o = _attn_prompt(
        sinks, q, k, v, [w_attn_up[l], w_ssm_glu[l], w_out[l], w_xq[l], w_xo[l], w_ffn2_in[l], w_ffn2_out[l]], t=seq)
    nk_p, nv_p = k[seq - WINDOW:seq], v[seq - WINDOW:seq]
    to_t = lambda c: jnp.transpose(c[l], (0, 2, 3, 1))
    o_s, nkt, nvt = _attn_sample(sinks, q, k, v, to_t(cache_win_k), to_t(cache_win_v), nb=8, ts=ts, row0=seq)
    nk_s, nv_s = jnp.transpose(nkt, (0, 3, 1, 2)), jnp.transpose(nvt, (0, 3, 1, 2))

    z_p, hout_p = _ssm_prompt(u, bb, cc, lam_tm, d_skip, jnp.zeros((4, SUBLANES, LANES), F32), t=seq)
    hre_p = jnp.concatenate([hout_p[0], hout_p[2]], axis=0)
    him_p = jnp.concatenate([hout_p[1], hout_p[3]], axis=0)
    z_s, hre_s, him_s = _ssm_sample(u, bb, cc, lre_row, lim_row, d_skip, state_ssm_re[l].reshape(nseq, SSM_CH),
                                    state_ssm_im[l].reshape(nseq, SSM_CH), nb=64, ts=ts, row0=seq)

    x2, ox_p, qx_s = _merge(x1, o_p, o_s, z_p, z_s, ga, gb, wup, wglu, wout, vec(g_xattn), wxq, mk_p, mv_p, tm=tm)
    y_p, y_s = _tail(x2, ox_p, qx_s, cache_mem_k[l], cache_mem_v[l], wxo, vec(g_ffn2), w2i, w2o, gf,
                     tm=TM_TAIL, ts=ts)

    kvshape = (1, 1, WINDOW, N_KV_HEADS, HEAD_DIM)
    stshape = (1, 1, N_SSM_GROUPS, SSM_STATE)
    memshape = (1, 1, N_MEM, N_X_HEADS, X_HEAD_DIM)
    return (y_p.reshape(1, seq, D_MODEL), y_s.reshape(nseq, ts, D_MODEL),
            nk_p.reshape(kvshape), nv_p.reshape(kvshape), hre_p.reshape(stshape), him_p.reshape(stshape),
            mk_p.reshape(memshape), mv_p.reshape(memshape),
            nk_s.reshape(1, nseq, WINDOW, N_KV_HEADS, HEAD_DIM), nv_s.reshape(1, nseq, WINDOW, N_KV_HEADS, HEAD_DIM),
            hre_s.reshape(1, nseq, N_SSM_GROUPS, SSM_STATE), him_s.reshape(1, nseq, N_SSM_GROUPS, SSM_STATE))
```

```python
import functools
import math

import jax
import jax.numpy as jnp
from jax import lax
from jax.experimental import pallas as pl
from jax.experimental.pallas import tpu as pltpu

F32 = jnp.float32
BF16 = jnp.bfloat16

D_MODEL = 1024
PAST_LEN = 16384
N_Q_HEADS = 8
N_KV_HEADS = 2
HEAD_DIM = 64
Q_PER_KV = N_Q_HEADS // N_KV_HEADS
ATTN_WIDTH = N_Q_HEADS * HEAD_DIM
KV_WIDTH = N_KV_HEADS * HEAD_DIM
WINDOW = 128
ROPE_THETA = 10000.0
SSM_WIDTH = D_MODEL // 2
SSM_GROUP = 16
N_SSM_GROUPS = SSM_WIDTH // SSM_GROUP
SSM_STATE = 64
N_MEM = 256
N_X_HEADS = 4
X_HEAD_DIM = D_MODEL // N_X_HEADS
D_FF = 2816
RMS_EPS = 1e-6
NEG_INF = -1e30
IN_SPLITS = (ATTN_WIDTH, KV_WIDTH, KV_WIDTH, SSM_WIDTH, D_MODEL, D_MODEL)
IN_WIDTH = sum(IN_SPLITS)
OFF_Q, OFF_K, OFF_V, OFF_U, OFF_GA, OFF_GB = (0, 512, 640, 768, 1280, 2304)

LANES = 128
SUBLANES = 8
BF16_SUBLANES = 16
VMEM_LIMIT = 60 * 1024 * 1024

SSM_CH = N_SSM_GROUPS * SSM_STATE
HALF_CH = SSM_CH // 2
HALF_U = SSM_WIDTH // 2
BLK_PER_HALF = 2 * HALF_CH // LANES
TM_FFN = 1024
TM_TAIL = 512
FFN_CHUNKS = ((0, 768), (768, 1536), (1536, 2304), (2304, D_FF))
TM_MIX = 1024
ATTN_QBLOCKS = 16
SSM_TB = 256
SSM_NB = 2
SSM_PITCH = SSM_TB + 4


def _cparams(sem):
    return pltpu.CompilerParams(dimension_semantics=sem, vmem_limit_bytes=VMEM_LIMIT)


def _const_spec(shape):
    nd = len(shape)
    return pl.BlockSpec(shape, lambda *_: (0,) * nd, pipeline_mode=pl.Buffered(1))


def _rms(x, g):
    return x * lax.rsqrt(jnp.mean(x * x, axis=-1, keepdims=True) + RMS_EPS) * g


def _dot(a, b):
    return jnp.dot(a, b, preferred_element_type=F32)


def _dot_nt(a, b):
    return lax.dot_general(a, b, (((1,), (1,)), ((), ())), preferred_element_type=F32)


def _row_spec(tm, width):
    return pl.BlockSpec((tm, width), lambda i: (i, 0))


def _two_array_specs(n_first, tm, width):
    return [pl.BlockSpec((tm, width), lambda i: (jnp.minimum(i, n_first - 1), 0)),
            pl.BlockSpec((tm, width), lambda i: (jnp.maximum(i - n_first, 0), 0))]


def _cast_stream_specs(weights, nsteps):
    in_specs, out_specs, out_shapes = [], [], []
    for w in weights:
        rows, cols = w.shape
        chunk = rows // nsteps
        assert chunk * nsteps == rows and chunk % BF16_SUBLANES == 0
        spec = pl.BlockSpec((chunk, cols), lambda i: (jnp.minimum(i, nsteps - 1), 0))
        in_specs.append(spec)
        out_specs.append(spec)
        out_shapes.append(jax.ShapeDtypeStruct(w.shape, BF16))
    return in_specs, out_specs, out_shapes


def _cast_chunks(src_refs, dst_refs):
    for src, dst in zip(src_refs, dst_refs, strict=True):
        dst[...] = src[...].astype(BF16)


def _store_by_step(is_first, first_ref, second_ref, value):
    @pl.when(is_first)
    def _():
        first_ref[...] = value

    @pl.when(jnp.logical_not(is_first))
    def _():
        second_ref[...] = value


def _ffn_residual(x, g_ref, wi_ref, wo_ref, between=()):
    h = _rms(x, g_ref[...]).astype(BF16)
    acc = None
    for c, (lo, hi) in enumerate(FFN_CHUNKS):
        a = _dot(h, wi_ref[:, lo:hi])
        b = _dot(h, wi_ref[:, D_FF + lo:D_FF + hi])
        part = _dot((a * jax.nn.sigmoid(a) * b).astype(BF16), wo_ref[lo:hi, :])
        acc = part if acc is None else acc + part
        if c < len(between):
            between[c]()
    return x + 0.5 * acc


def _ffn_first_kernel(xp_ref, xs_ref, g_ref, wi_ref, wo_ref, *rest, n_first, n_cast):
    cast_src, o_ref, cast_dst = rest[:n_cast], rest[n_cast], rest[n_cast + 1:]
    _cast_chunks(cast_src, cast_dst)
    x = jnp.where(pl.program_id(0) < n_first, xp_ref[...], xs_ref[...])
    o_ref[...] = _ffn_residual(x, g_ref, wi_ref, wo_ref)


def _ffn_first(xp, xs, g, wi, wo, cast_weights, *, tm):
    n_first, n_second = xp.shape[0] // tm, xs.shape[0] // tm
    c_in, c_out, c_shapes = _cast_stream_specs(cast_weights, n_first)
    return pl.pallas_call(
        functools.partial(_ffn_first_kernel, n_first=n_first, n_cast=len(cast_weights)),
        grid=(n_first + n_second,),
        in_specs=_two_array_specs(n_first, tm, D_MODEL)
                 + [_const_spec(g.shape), _const_spec(wi.shape), _const_spec(wo.shape)] + c_in,
        out_specs=[_row_spec(tm, D_MODEL)] + c_out,
        out_shape=[jax.ShapeDtypeStruct((xp.shape[0] + xs.shape[0], D_MODEL), F32)] + c_shapes,
        compiler_params=_cparams(("arbitrary",)),
        name="ffn",
    )(xp, xs, g, wi, wo, *cast_weights)


def _xattn_seq_probs(qb, mk):
    ts = qb.shape[0]
    krows = N_MEM * N_X_HEADS
    row = lax.broadcasted_iota(jnp.int32, (ts * N_X_HEADS, krows), 0)
    col = lax.broadcasted_iota(jnp.int32, (ts * N_X_HEADS, krows), 1)
    same_head = (row // ts) == (col & (N_X_HEADS - 1))
    qs = jnp.concatenate([qb[:, h * X_HEAD_DIM:(h + 1) * X_HEAD_DIM] for h in range(N_X_HEADS)], axis=0)
    s = jnp.where(same_head, _dot_nt(qs, mk.reshape(krows, X_HEAD_DIM).astype(BF16)), NEG_INF)
    p = jnp.exp(s - jnp.max(s, axis=-1, keepdims=True))
    return p.astype(BF16), jnp.sum(p, axis=-1, keepdims=True)


def _xattn_seq_values(p, denom, mv):
    ts = p.shape[0] // N_X_HEADS
    o = (_dot(p, mv.reshape(N_MEM * N_X_HEADS, X_HEAD_DIM).astype(BF16)) / denom).astype(BF16)
    return jnp.concatenate([o[h * ts:(h + 1) * ts] for h in range(N_X_HEADS)], axis=1)


def _tail_kernel(x_ref, oxp_ref, qxs_ref, mk_ref, mv_ref, wxo_ref, g_ref, wi_ref, wo_ref, gf_ref, yp_ref, ys_ref,
                 oxs_ref, *, n_first, nb, ts, tm):
    i = pl.program_id(0)
    is_first = i < n_first
    t_s = qxs_ref.shape[0]
    pair = 2 * ts

    @pl.when(i == 0)
    def _():
        oxs_ref[...] = jnp.zeros(oxs_ref.shape, oxs_ref.dtype)

    row0 = pl.multiple_of(jnp.maximum(i - n_first, 0) * tm, tm)
    ox = jnp.where(is_first, oxp_ref[...], oxs_ref[pl.ds(row0, tm), :])
    x = x_ref[...] + _dot(ox, wxo_ref[...])

    seq0 = jnp.minimum(i, n_first - 1) * nb
    pending = {}

    def probs_stage(j):
        src = pl.multiple_of((seq0 + 2 * j) * ts, pair)
        q2 = qxs_ref[pl.ds(src, pair), :]
        pending[j] = (src, [_xattn_seq_probs(q2[b * ts:(b + 1) * ts], mk_ref[2 * j + b]) for b in range(2)])

    def values_stage(j):
        src, probs = pending.pop(j)
        o2 = jnp.concatenate([_xattn_seq_values(p, denom, mv_ref[2 * j + b]) for b, (p, denom) in enumerate(probs)],
                             axis=0)
        dst = pl.multiple_of(jnp.where(is_first, src, t_s), pair)
        oxs_ref[pl.ds(dst, pair), :] = o2

    n_pairs = nb // 2
    assert 1 <= n_pairs < len(FFN_CHUNKS)

    def after_chunk(c):
        def run():
            values_stage(c)
            if c + 1 < n_pairs:
                probs_stage(c + 1)
        return run

    probs_stage(0)
    y = _rms(_ffn_residual(x, g_ref, wi_ref, wo_ref, between=[after_chunk(c) for c in range(n_pairs)]), gf_ref[...])
    _store_by_step(is_first, yp_ref, ys_ref, y)


def _tail(x2, ox_p, qx_s, mk, mv, wxo, g, wi, wo, gf, *, tm, ts):
    t_p, t_s, nseq = ox_p.shape[0], qx_s.shape[0], mk.shape[0]
    n_first, n_second = t_p // tm, t_s // tm
    nb = nseq // n_first
    assert nb * n_first == nseq and nb % 2 == 0 and nseq * ts == t_s
    mem = pl.BlockSpec((nb, N_MEM, N_X_HEADS, X_HEAD_DIM), lambda i: (jnp.minimum(i, n_first - 1), 0, 0, 0))
    return pl.pallas_call(
        functools.partial(_tail_kernel, n_first=n_first, nb=nb, ts=ts, tm=tm),
        grid=(n_first + n_second,),
        in_specs=[_row_spec(tm, D_MODEL), _two_array_specs(n_first, tm, D_MODEL)[0], _const_spec(qx_s.shape), mem, mem]
                 + [_const_spec(c.shape) for c in (wxo, g, wi, wo, gf)],
        out_specs=_two_array_specs(n_first, tm, D_MODEL),
        out_shape=[jax.ShapeDtypeStruct((t_p, D_MODEL), F32), jax.ShapeDtypeStruct((t_s, D_MODEL), F32)],
        scratch_shapes=[pltpu.VMEM((t_s + 2 * ts, D_MODEL), BF16)],
        compiler_params=_cparams(("arbitrary",)),
        name="tail",
    )(x2, ox_p, qx_s, mk, mv, wxo, g, wi, wo, gf)


def _inproj_kernel(x_ref, g_ref, w_ref, ca_ref, sa_ref, cb_ref, sb_ref, q_ref, k_ref, v_ref, u_ref, ga_ref, gb_ref):
    h = _rms(x_ref[...], g_ref[...]).astype(BF16)
    ca, sa, cb, sb = ca_ref[0], sa_ref[0], cb_ref[0], sb_ref[0]
    cos = ca * cb - sa * sb
    sin = sa * cb + ca * sb
    lane = lax.broadcasted_iota(jnp.int32, cos.shape, 1)
    first_half = (lane & (HEAD_DIM - 1)) < (HEAD_DIM // 2)

    def rope(xc):
        rot = jnp.where(first_half, pltpu.roll(xc, LANES - HEAD_DIM // 2, 1), pltpu.roll(xc, HEAD_DIM // 2, 1))
        return xc * cos + rot * sin

    ga_ref[...] = jax.nn.sigmoid(_dot(h, w_ref[:, OFF_GA:OFF_GB])).astype(BF16)
    gb_ref[...] = jax.nn.sigmoid(_dot(h, w_ref[:, OFF_GB:])).astype(BF16)
    yq = _dot(h, w_ref[:, OFF_Q:OFF_K])
    for c in range(ATTN_WIDTH // LANES):
        q_ref[:, c * LANES:(c + 1) * LANES] = (rope(yq[:, c * LANES:(c + 1) * LANES]) * (HEAD_DIM ** -0.5)).astype(BF16)
    ykvu = _dot(h, w_ref[:, OFF_K:OFF_GA])
    k_ref[...] = rope(ykvu[:, :KV_WIDTH])
    v_ref[...] = ykvu[:, KV_WIDTH:2 * KV_WIDTH]
    u_ref[...] = ykvu[:, 2 * KV_WIDTH:]


def _inproj(x, g, w, rope_tabs, *, tm, n_first):
    t = x.shape[0]
    row = lambda wd: _row_spec(tm, wd)
    step_tab = pl.BlockSpec((1, 1, LANES), lambda i: (i, 0, 0))
    row_tab = pl.BlockSpec((1, tm, LANES), lambda i: (jnp.minimum(i // n_first, 1), 0, 0))
    widths = (ATTN_WIDTH, KV_WIDTH, KV_WIDTH, SSM_WIDTH, D_MODEL, D_MODEL)
    dtypes = (BF16, F32, F32, F32, BF16, BF16)
    return pl.pallas_call(
        _inproj_kernel,
        grid=(t // tm,),
        in_specs=[row(D_MODEL), _const_spec((1, D_MODEL)), _const_spec((D_MODEL, IN_WIDTH)), step_tab, step_tab,
                  row_tab, row_tab],
        out_specs=[row(wd) for wd in widths],
        out_shape=[jax.ShapeDtypeStruct((t, wd), dt) for wd, dt in zip(widths, dtypes)],
        compiler_params=_cparams(("parallel",)),
        name="inproj",
    )(x, g, w, *rope_tabs)


def _rope_tables(step_pos, row_pos):
    half = HEAD_DIM // 2
    inv = ROPE_THETA ** (-jnp.arange(half, dtype=F32) / half)
    sign = jnp.concatenate([-jnp.ones((half,), F32), jnp.ones((half,), F32)])

    def tabs(pos):
        ang = pos.astype(F32)[..., None] * inv
        cos = jnp.tile(jnp.cos(ang), LANES // half)
        sin = jnp.tile(jnp.tile(jnp.sin(ang), 2) * sign, LANES // HEAD_DIM)
        return cos, sin

    ca, sa = tabs(step_pos[:, None])
    cb, sb = tabs(row_pos)
    return ca, sa, cb, sb


def _attn_prompt_kernel(sinks_ref, q_ref, kp_ref, kc_ref, vp_ref, vc_ref, *rest, n_cast):
    cast_src, o_ref, cast_dst = rest[:n_cast], rest[n_cast], rest[n_cast + 1:]
    _cast_chunks(cast_src, cast_dst)
    i = pl.program_id(0)
    r = lax.broadcasted_iota(jnp.int32, (WINDOW, 2 * WINDOW), 0)
    c = lax.broadcasted_iota(jnp.int32, (WINDOW, 2 * WINDOW), 1)
    band = (c >= r) & (c <= WINDOW + r)
    first_mask = band & ((c >= WINDOW) | (i > 0))
    kcat = jnp.concatenate([kp_ref[...], kc_ref[...]], axis=0)
    vcat = jnp.concatenate([vp_ref[...], vc_ref[...]], axis=0)
    kswap = pltpu.roll(kcat, HEAD_DIM, 1)
    vswap = pltpu.roll(vcat, HEAD_DIM, 1)
    lo = lax.broadcasted_iota(jnp.int32, kcat.shape, 1) < HEAD_DIM
    lo_q = lax.broadcasted_iota(jnp.int32, (WINDOW, LANES), 1) < HEAD_DIM
    ones = jnp.ones(kcat.shape, F32)
    for kv in range(N_KV_HEADS):
        k_own, k_other = (kcat, kswap) if kv == 0 else (kswap, kcat)
        v_own, v_other = (vcat, vswap) if kv == 0 else (vswap, vcat)
        k_half = (jnp.where(lo, k_own, 0.0).astype(BF16), jnp.where(lo, 0.0, k_other).astype(BF16))
        rhs_half = (jnp.concatenate([jnp.where(lo, v_own, 0.0), jnp.where(lo, ones, 0.0)], axis=1).astype(BF16),
                    jnp.concatenate([jnp.where(lo, 0.0, v_other), jnp.where(lo, 0.0, ones)], axis=1).astype(BF16))
        for j in range(ATTN_QBLOCKS):
            qrows = slice(j * WINDOW, (j + 1) * WINDOW)
            krows = slice(j * WINDOW, (j + 2) * WINDOW)
            mask = first_mask if j == 0 else band
            for c in range(Q_PER_KV // 2):
                tile = kv * (Q_PER_KV // 2) + c
                qt = q_ref[qrows, tile * LANES:(tile + 1) * LANES]
                acc = None
                sink_terms = []
                for side in range(2):
                    s = jnp.where(mask, _dot_nt(qt, k_half[side][krows]), NEG_INF)
                    sink = sinks_ref[2 * tile + side]
                    m = jnp.maximum(jnp.max(s, axis=-1, keepdims=True), sink)
                    pv = _dot(jnp.exp((s - m).astype(BF16)), rhs_half[side][krows])
                    acc = pv if acc is None else acc + pv
                    sink_terms.append(jnp.exp(sink - m))
                denom = acc[:, LANES:] + jnp.where(lo_q, sink_terms[0], sink_terms[1])
                o_ref[qrows, tile * LANES:(tile + 1) * LANES] = (acc[:, :LANES] / denom).astype(BF16)


def _attn_prompt(sinks, q, k, v, cast_weights, *, t):
    rows = ATTN_QBLOCKS * WINDOW
    cur = lambda w: pl.BlockSpec((rows, w), lambda i: (i, 0))
    prev = lambda w: pl.BlockSpec((WINDOW, w), lambda i: (jnp.maximum(i * ATTN_QBLOCKS - 1, 0), 0))
    c_in, c_out, c_shapes = _cast_stream_specs(cast_weights, t // rows)
    return pl.pallas_call(
        functools.partial(_attn_prompt_kernel, n_cast=len(cast_weights)),
        grid=(t // rows,),
        in_specs=[pl.BlockSpec(memory_space=pltpu.SMEM), cur(ATTN_WIDTH), prev(KV_WIDTH), cur(KV_WIDTH),
                  prev(KV_WIDTH), cur(KV_WIDTH)] + c_in,
        out_specs=[cur(ATTN_WIDTH)] + c_out,
        out_shape=[jax.ShapeDtypeStruct((t, ATTN_WIDTH), BF16)] + c_shapes,
        compiler_params=_cparams(("parallel",)),
        name="attn_prompt",
    )(sinks, q, k, k, v, v, *cast_weights)


def _attn_sample_kernel(sinks_ref, q_ref, k_ref, v_ref, ckt_ref, cvt_ref, o_ref, nkt_ref, nvt_ref, *, nb, ts):
    rows = nb * ts
    q = q_ref[...]
    pad = jnp.zeros((LANES - rows, KV_WIDTH), F32)
    knt = jnp.concatenate([k_ref[...], pad], axis=0).T
    vnt = jnp.concatenate([v_ref[...], pad], axis=0).T
    lane = lax.broadcasted_iota(jnp.int32, (HEAD_DIM, WINDOW), 1)
    is_new = lane >= WINDOW - ts
    for b in range(nb):
        knt_b = pltpu.roll(knt, (WINDOW - ts - b * ts) % LANES, 1)
        vnt_b = pltpu.roll(vnt, (WINDOW - ts - b * ts) % LANES, 1)
        for kv in range(N_KV_HEADS):
            ksl = slice(kv * HEAD_DIM, (kv + 1) * HEAD_DIM)
            nkt_ref[b, kv] = jnp.where(is_new, knt_b[ksl], pltpu.roll(ckt_ref[b, kv], WINDOW - ts, 1))
            nvt_ref[b, kv] = jnp.where(is_new, vnt_b[ksl], pltpu.roll(cvt_ref[b, kv], WINDOW - ts, 1))

    knt = knt[:, :rows].astype(BF16)
    vnt = vnt[:, :rows].astype(BF16)
    grows = Q_PER_KV * rows
    row1 = lax.broadcasted_iota(jnp.int32, (grows, nb * WINDOW), 0)
    col1 = lax.broadcasted_iota(jnp.int32, (grows, nb * WINDOW), 1)
    rseq1 = (row1 % rows) // ts
    mask_ctx = (rseq1 == col1 // WINDOW) & (col1 % WINDOW >= row1 % ts)
    row2 = lax.broadcasted_iota(jnp.int32, (grows, rows), 0)
    col2 = lax.broadcasted_iota(jnp.int32, (grows, rows), 1)
    mask_new = ((row2 % rows) // ts == col2 // ts) & (col2 % ts <= row2 % ts)
    ghead = lax.broadcasted_iota(jnp.int32, (grows, 1), 0) // rows
    for kv in range(N_KV_HEADS):
        ksl = slice(kv * HEAD_DIM, (kv + 1) * HEAD_DIM)
        qg = jnp.concatenate([q[:, (kv * Q_PER_KV + g) * HEAD_DIM:(kv * Q_PER_KV + g + 1) * HEAD_DIM]
                              for g in range(Q_PER_KV)], axis=0)
        kctx = jnp.concatenate([ckt_ref[b, kv] for b in range(nb)], axis=1).astype(BF16)
        vctx = jnp.concatenate([cvt_ref[b, kv] for b in range(nb)], axis=1).astype(BF16)
        s1 = jnp.where(mask_ctx, _dot(qg, kctx), NEG_INF)
        s2 = jnp.where(mask_new, _dot(qg, knt[ksl]), NEG_INF)
        sink = jnp.zeros((grows, 1), F32)
        for g in range(Q_PER_KV):
            sink = jnp.where(ghead == g, sinks_ref[kv * Q_PER_KV + g], sink)
        m = jnp.maximum(jnp.maximum(jnp.max(s1, axis=-1, keepdims=True), jnp.max(s2, axis=-1, keepdims=True)), sink)
        p1 = jnp.exp(s1 - m)
        p2 = jnp.exp(s2 - m)
        denom = jnp.sum(p1, axis=-1, keepdims=True) + jnp.sum(p2, axis=-1, keepdims=True) + jnp.exp(sink - m)
        o = (_dot_nt(p1.astype(BF16), vctx) + _dot_nt(p2.astype(BF16), vnt[ksl])) / denom
        for g in range(Q_PER_KV):
            h = kv * Q_PER_KV + g
            o_ref[:, h * HEAD_DIM:(h + 1) * HEAD_DIM] = o[g * rows:(g + 1) * rows].astype(BF16)


def _attn_sample(sinks, q, k, v, ckt, cvt, *, nb, ts, row0):
    nseq = ckt.shape[0]
    rows = nb * ts
    assert rows <= LANES and row0 % rows == 0
    row_in = lambda w: pl.BlockSpec((rows, w), lambda i: (i + row0 // rows, 0))
    win = pl.BlockSpec((nb, N_KV_HEADS, HEAD_DIM, WINDOW), lambda i: (i, 0, 0, 0))
    win_shape = jax.ShapeDtypeStruct((nseq, N_KV_HEADS, HEAD_DIM, WINDOW), F32)
    return pl.pallas_call(
        functools.partial(_attn_sample_kernel, nb=nb, ts=ts),
        grid=(nseq // nb,),
        in_specs=[pl.BlockSpec(memory_space=pltpu.SMEM), row_in(ATTN_WIDTH), row_in(KV_WIDTH), row_in(KV_WIDTH),
                  win, win],
        out_specs=[_row_spec(rows, ATTN_WIDTH), win, win],
        out_shape=[jax.ShapeDtypeStruct((nseq * ts, ATTN_WIDTH), BF16), win_shape, win_shape],
        compiler_params=_cparams(("parallel",)),
        name="attn_sample",
    )(sinks, q, k, v, ckt, cvt)


def _ssm_prep_kernel(are_ref, aim_ref, ldt_ref, bre_ref, bim_ref, lre_ref, lim_ref, bbre_ref, bbim_ref):
    a_re = are_ref[...]
    a_im = aim_ref[...]
    dt = jnp.exp(ldt_ref[...])
    mag = jnp.exp(a_re * dt)
    lb_re = mag * jnp.cos(a_im * dt)
    lb_im = mag * jnp.sin(a_im * dt)
    den = a_re * a_re + a_im * a_im
    nr = lb_re - 1.0
    ni = lb_im
    k_re = (nr * a_re + ni * a_im) / den
    k_im = (ni * a_re - nr * a_im) / den
    lre_ref[...] = lb_re
    lim_ref[...] = lb_im
    b_re = bre_ref[...]
    b_im = bim_ref[...]
    bbre_ref[...] = k_re[:, None, :] * b_re - k_im[:, None, :] * b_im
    bbim_ref[...] = k_re[:, None, :] * b_im + k_im[:, None, :] * b_re


def _ssm_prep(a_re, a_im, log_dt, b_re, b_im):
    g, n, gs = b_re.shape
    sds = jax.ShapeDtypeStruct
    return pl.pallas_call(
        _ssm_prep_kernel,
        out_shape=[sds((g, n), F32), sds((g, n), F32), sds((g, gs, n), F32), sds((g, gs, n), F32)],
        name="ssm_prep",
    )(a_re, a_im, log_dt.reshape(g, 1), jnp.swapaxes(b_re, 1, 2), jnp.swapaxes(b_im, 1, 2))


def _block_diag_halves(m):
    g, a, b = m.shape
    gh = g // 2
    tiled = jnp.tile(m.reshape(2, gh * a, b), (1, 1, gh))
    row_group = lax.broadcasted_iota(jnp.int32, tiled.shape, 1) // a
    col_group = lax.broadcasted_iota(jnp.int32, tiled.shape, 2) // b
    return jnp.where(row_group == col_group, tiled, 0.0)


def _ssm_y(u, scan_block, cc_ref, d_ref):
    ys = []
    for half in range(2):
        hc = jnp.concatenate([scan_block(half, jj) for jj in range(BLK_PER_HALF)], axis=1).astype(BF16)
        ys.append(_dot(hc, cc_ref[half]))
    y = jnp.concatenate(ys, axis=1) + d_ref[...] * u
    return jax.nn.gelu(y).astype(BF16)


def _ssm_prompt_kernel(u_ref, bb_ref, cc_ref, lam_ref, d_ref, h0_ref, z_ref, hout_ref,
                       x0_ref, x1_ref, s0_ref, s1_ref, hst_ref):
    tb, pitch = SSM_TB, SSM_PITCH
    x_bufs = (x0_ref, x1_ref)
    s_bufs = (s0_ref, s1_ref)

    @pl.when(pl.program_id(0) == 0)
    def _():
        hst_ref[...] = h0_ref[...]

    def b_proj(k):
        ub = u_ref[k * tb:(k + 1) * tb, :].astype(BF16)
        for half in range(2):
            x = _dot(ub[:, half * HALF_U:(half + 1) * HALF_U], bb_ref[half])
            for jj in range(BLK_PER_HALF):
                x_bufs[k % 2][pl.ds((half * BLK_PER_HALF + jj) * pitch, tb), :] = x[:, jj * LANES:(jj + 1) * LANES]

    lam = [lam_ref[g] for g in range(4)]
    carry = [hst_ref[g] for g in range(4)]

    def scan(k):
        for t in range(tb):
            for half in range(2):
                hr, hi = carry[2 * half], carry[2 * half + 1]
                lr, li = lam[2 * half], lam[2 * half + 1]
                rows_re = pl.ds((half * BLK_PER_HALF) * pitch + t, SUBLANES, stride=pitch)
                rows_im = pl.ds((half * BLK_PER_HALF + SUBLANES) * pitch + t, SUBLANES, stride=pitch)
                nr = lr * hr - li * hi + x_bufs[k % 2][rows_re, :]
                ni = lr * hi + li * hr + x_bufs[k % 2][rows_im, :]
                s_bufs[k % 2][rows_re, :] = nr
                s_bufs[k % 2][rows_im, :] = ni
                carry[2 * half], carry[2 * half + 1] = nr, ni

    def c_proj(k):
        z_ref[k * tb:(k + 1) * tb, :] = _ssm_y(
            u_ref[k * tb:(k + 1) * tb, :],
            lambda half, jj: s_bufs[k % 2][pl.ds((half * BLK_PER_HALF + jj) * pitch, tb), :], cc_ref, d_ref)

    b_proj(0)
    for k in range(SSM_NB):
        if k + 1 < SSM_NB:
            b_proj(k + 1)
        scan(k)
        c_proj(k)
    for g in range(4):
        hst_ref[g] = carry[g]
        hout_ref[g] = carry[g]


def _ssm_prompt(u, bb, cc, lam, d, h0, *, t):
    rows = SSM_NB * SSM_TB
    return pl.pallas_call(
        _ssm_prompt_kernel,
        grid=(t // rows,),
        in_specs=[pl.BlockSpec((rows, SSM_WIDTH), lambda i: (i, 0)), _const_spec(bb.shape), _const_spec(cc.shape),
                  _const_spec(lam.shape), _const_spec(d.shape), _const_spec(h0.shape)],
        out_specs=[pl.BlockSpec((rows, SSM_WIDTH), lambda i: (i, 0)),
                   pl.BlockSpec((4, SUBLANES, LANES), lambda i: (0, 0, 0))],
        out_shape=[jax.ShapeDtypeStruct((t, SSM_WIDTH), BF16), jax.ShapeDtypeStruct((4, SUBLANES, LANES), F32)],
        scratch_shapes=[pltpu.VMEM((2 * BLK_PER_HALF * SSM_PITCH, LANES), F32)] * 4
                       + [pltpu.VMEM((4, SUBLANES, LANES), F32)],
        compiler_params=_cparams(("arbitrary",)),
        name="ssm_prompt",
    )(u, bb, cc, lam, d, h0)


def _ssm_sample_kernel(u_ref, bb_ref, cc_ref, lre_ref, lim_ref, d_ref, h0re_ref, h0im_ref,
                       z_ref, hre_ref, him_ref, xs_ref, *, nb, ts):
    u = u_ref[...]
    ub = u.astype(BF16)
    for half in range(2):
        x = _dot(ub[:, half * HALF_U:(half + 1) * HALF_U], bb_ref[half])
        for jj in range(BLK_PER_HALF):
            xs_ref[half * BLK_PER_HALF + jj] = x[:, jj * LANES:(jj + 1) * LANES]
    for cb in range(SSM_CH // LANES):
        half, jj = divmod(cb, SUBLANES)
        j_re = half * BLK_PER_HALF + jj
        j_im = j_re + SUBLANES
        csl = slice(cb * LANES, (cb + 1) * LANES)
        lr = lre_ref[:, csl]
        li = lim_ref[:, csl]
        hr = h0re_ref[:, csl]
        hi = h0im_ref[:, csl]
        for t in range(ts):
            rows = pl.ds(t, nb, stride=ts)
            nr = lr * hr - li * hi + xs_ref[j_re, rows, :]
            ni = lr * hi + li * hr + xs_ref[j_im, rows, :]
            xs_ref[j_re, rows, :] = nr
            xs_ref[j_im, rows, :] = ni
            hr, hi = nr, ni
        hre_ref[:, csl] = hr
        him_ref[:, csl] = hi
    z_ref[...] = _ssm_y(u, lambda half, jj: xs_ref[half * BLK_PER_HALF + jj], cc_ref, d_ref)


def _ssm_sample(u, bb, cc, lre, lim, d, h0re, h0im, *, nb, ts, row0):
    nseq = h0re.shape[0]
    rows = nb * ts
    assert row0 % rows == 0
    st = pl.BlockSpec((nb, SSM_CH), lambda i: (i, 0))
    return pl.pallas_call(
        functools.partial(_ssm_sample_kernel, nb=nb, ts=ts),
        grid=(nseq // nb,),
        in_specs=[pl.BlockSpec((rows, SSM_WIDTH), lambda i: (i + row0 // rows, 0)), _const_spec(bb.shape),
                  _const_spec(cc.shape),
                  _const_spec(lre.shape), _const_spec(lim.shape), _const_spec(d.shape), st, st],
        out_specs=[pl.BlockSpec((rows, SSM_WIDTH), lambda i: (i, 0)), st, st],
        out_shape=[jax.ShapeDtypeStruct((nseq * ts, SSM_WIDTH), BF16),
                   jax.ShapeDtypeStruct((nseq, SSM_CH), F32), jax.ShapeDtypeStruct((nseq, SSM_CH), F32)],
        scratch_shapes=[pltpu.VMEM((2 * BLK_PER_HALF, rows, LANES), F32)],
        compiler_params=_cparams(("parallel",)),
        name="ssm_sample",
    )(u, bb, cc, lre, lim, d, h0re, h0im)


def _xattn_heads(q, mk, mv):
    outs = []
    for h in range(N_X_HEADS):
        sl = slice(h * X_HEAD_DIM, (h + 1) * X_HEAD_DIM)
        s = _dot_nt(q[:, sl], mk[:, sl])
        p = jnp.exp(s - jnp.max(s, axis=-1, keepdims=True))
        outs.append(_dot(p.astype(BF16), mv[:, sl]) / jnp.sum(p, axis=-1, keepdims=True))
    return jnp.concatenate(outs, axis=1)


def _merge_kernel(x1_ref, op_ref, os_ref, zp_ref, zs_ref, ga_ref, gb_ref, wup_ref, wglu_ref, wout_ref, gx_ref,
                  wxq_ref, mk_ref, mv_ref, x2_ref, oxp_ref, qxs_ref, *, n_first):
    is_first = pl.program_id(0) < n_first
    o = jnp.where(is_first, op_ref[...], os_ref[...])
    z = jnp.where(is_first, zp_ref[...], zs_ref[...])
    ya = _dot(o, wup_ref[...])
    glu = _dot(z, wglu_ref[...])
    yb = glu[:, :D_MODEL] * jax.nn.sigmoid(glu[:, D_MODEL:])
    merged = ga_ref[...].astype(F32) * ya + gb_ref[...].astype(F32) * yb
    x2 = x1_ref[...] + _dot(merged.astype(BF16), wout_ref[...])
    x2_ref[...] = x2
    hx = _rms(x2, gx_ref[...]).astype(BF16)
    qx = (_dot(hx, wxq_ref[...]) * (X_HEAD_DIM ** -0.5)).astype(BF16)
    ox = _xattn_heads(qx, mk_ref[...].astype(BF16), mv_ref[...].astype(BF16)).astype(BF16)

    @pl.when(is_first)
    def _():
        oxp_ref[...] = ox

    @pl.when(jnp.logical_not(is_first))
    def _():
        qxs_ref[...] = qx


def _merge(x1, o_p, o_s, z_p, z_s, ga, gb, wup, wglu, wout, gx, wxq, mk, mv, *, tm):
    n_first, n_second = o_p.shape[0] // tm, o_s.shape[0] // tm
    row = lambda w: _row_spec(tm, w)
    consts = [wup, wglu, wout, gx, wxq, mk, mv]
    return pl.pallas_call(
        functools.partial(_merge_kernel, n_first=n_first),
        grid=(n_first + n_second,),
        in_specs=[row(D_MODEL)] + _two_array_specs(n_first, tm, ATTN_WIDTH) + _two_array_specs(n_first, tm, SSM_WIDTH)
                 + [row(D_MODEL), row(D_MODEL)] + [_const_spec(c.shape) for c in consts],
        out_specs=[row(D_MODEL)] + _two_array_specs(n_first, tm, D_MODEL),
        out_shape=[jax.ShapeDtypeStruct(x1.shape, F32), jax.ShapeDtypeStruct((o_p.shape[0], D_MODEL), BF16),
                   jax.ShapeDtypeStruct((o_s.shape[0], D_MODEL), BF16)],
        compiler_params=_cparams(("arbitrary",)),
        name="merge",
    )(x1, o_p, o_s, z_p, z_s, ga, gb, *consts)


def _memkv_kernel(mem_ref, g_ref, wk_ref, wv_ref, mk_ref, mv_ref):
    mn = _rms(mem_ref[...], g_ref[...]).astype(BF16)
    mk_ref[...] = _dot(mn, wk_ref[...].astype(BF16))
    mv_ref[...] = _dot(mn, wv_ref[...].astype(BF16))


def _memkv(mem, g, wk, wv):
    sds = jax.ShapeDtypeStruct((mem.shape[0], D_MODEL), F32)
    return pl.pallas_call(_memkv_kernel, out_shape=[sds, sds], name="memkv",
                          compiler_params=pltpu.CompilerParams(vmem_limit_bytes=VMEM_LIMIT))(mem, g, wk, wv)


def kernel(x_prompt, x_sample, cache_win_k, cache_win_v, state_ssm_re, state_ssm_im, cache_mem_k, cache_mem_v, mem_prompt, g_ffn1, w_ffn1_in, w_ffn1_out, g_mix, w_in, attn_sinks, ssm_a_re, ssm_a_im, ssm_log_dt, ssm_b_re, ssm_b_im, ssm_c_re, ssm_c_im, ssm_d, w_attn_up, w_ssm_glu, w_out, g_xattn, g_mem, w_xq, w_xk, w_xv, w_xo, g_ffn2, w_ffn2_in, w_ffn2_out, g_final):
    assert x_prompt.shape[0] == 1 and g_ffn1.shape[0] == 1
    seq = x_prompt.shape[1]
    nseq, ts = x_sample.shape[0], x_sample.shape[1]
    past_len = PAST_LEN
    l = 0
    vec = lambda g: g[l].reshape(1, -1)
    w1i, w1o = w_ffn1_in[l].astype(BF16), w_ffn1_out[l].astype(BF16)
    gf = g_final.reshape(1, -1)
    sinks = attn_sinks[l]

    lam_re, lam_im, bbt_re, bbt_im = _ssm_prep(ssm_a_re[l], ssm_a_im[l], ssm_log_dt[l], ssm_b_re[l], ssm_b_im[l])
    bb = jnp.concatenate([_block_diag_halves(bbt_re), _block_diag_halves(bbt_im)], axis=-1).astype(BF16)
    ct_re = jnp.swapaxes(ssm_c_re[l], 1, 2)
    ct_im = jnp.swapaxes(ssm_c_im[l], 1, 2)
    cc = jnp.concatenate([_block_diag_halves(ct_re), -_block_diag_halves(ct_im)], axis=1).astype(BF16)
    d_skip = ssm_d[l].reshape(1, -1)
    lre16 = lam_re.reshape(2, SUBLANES, LANES)
    lim16 = lam_im.reshape(2, SUBLANES, LANES)
    lam_tm = jnp.stack([lre16[0], lim16[0], lre16[1], lim16[1]])
    lre_row = lam_re.reshape(1, SSM_CH)
    lim_row = lam_im.reshape(1, SSM_CH)

    mk_p, mv_p = _memkv(mem_prompt[0], vec(g_mem), w_xk[l], w_xv[l])

    tm = TM_MIX
    t_s = nseq * ts
    assert seq % tm == 0 and t_s % tm == 0 and tm % ts == 0
    n_p, n_s = seq // tm, t_s // tm
    rope_tabs = _rope_tables(jnp.concatenate([jnp.arange(n_p) * tm, jnp.full((n_s,), past_len)]),
                             jnp.stack([jnp.arange(tm), jnp.arange(tm) % ts]))

    x1, win = _ffn_first(x_prompt[0], x_sample.reshape(t_s, D_MODEL), vec(g_ffn1), w1i, w1o, [w_in[l]], tm=TM_FFN)
    q, k, v, u, ga, gb = _inproj(x1, vec(g_mix), win, rope_tabs, tm=tm, n_first=n_p)

    o_p, wup, wglu, wout, wxq, wxo, w2i, w2o = _attn_prompt(
        sinks, q, k, v, [w_attn_up[l], w_ssm_glu[l], w_out[l], w_xq[l], w_xo[l], w_ffn2_in[l], w_ffn2_out[l]], t=seq)
    nk_p, nv_p = k[seq - WINDOW:seq], v[seq - WINDOW:seq]
    to_t = lambda c: jnp.transpose(c[l], (0, 2, 3, 1))
    o_s, nkt, nvt = _attn_sample(sinks, q, k, v, to_t(cache_win_k), to_t(cache_win_v), nb=8, ts=ts, row0=seq)
    nk_s, nv_s = jnp.transpose(nkt, (0, 3, 1, 2)), jnp.transpose(nvt, (0, 3, 1, 2))

    z_p, hout_p = _ssm_prompt(u, bb, cc, lam_tm, d_skip, jnp.zeros((4, SUBLANES, LANES), F32), t=seq)
    hre_p = jnp.concatenate([hout_p[0], hout_p[2]], axis=0)
    him_p = jnp.concatenate([hout_p[1], hout_p[3]], axis=0)
    z_s, hre_s, him_s = _ssm_sample(u, bb, cc, lre_row, lim_row, d_skip, state_ssm_re[l].reshape(nseq, SSM_CH),
                                    state_ssm_im[l].reshape(nseq, SSM_CH), nb=64, ts=ts, row0=seq)

    x2, ox_p, qx_s = _merge(x1, o_p, o_s, z_p, z_s, ga, gb, wup, wglu, wout, vec(g_xattn), wxq, mk_p, mv_p, tm=tm)
    y_p, y_s = _tail(x2, ox_p, qx_s, cache_mem_k[l], cache_mem_v[l], wxo, vec(g_ffn2), w2i, w2o, gf,
                     tm=TM_TAIL, ts=ts)

    kvshape = (1, 1, WINDOW, N_KV_HEADS, HEAD_DIM)
    stshape = (1, 1, N_SSM_GROUPS, SSM_STATE)
    memshape = (1, 1, N_MEM, N_X_HEADS, X_HEAD_DIM)
    return (y_p.reshape(1, seq, D_MODEL), y_s.reshape(nseq, ts, D_MODEL),
            nk_p.reshape(kvshape), nv_p.reshape(kvshape), hre_p.reshape(stshape), him_p.reshape(stshape),
            mk_p.reshape(memshape), mv_p.reshape(memshape),
            nk_s.reshape(1, nseq, WINDOW, N_KV_HEADS, HEAD_DIM), nv_s.reshape(1, nseq, WINDOW, N_KV_HEADS, HEAD_DIM),
            hre_s.reshape(1, nseq, N_SSM_GROUPS, SSM_STATE), him_s.reshape(1, nseq, N_SSM_GROUPS, SSM_STATE))
```

```python
import functools
import math

import jax
import jax.numpy as jnp
from jax import lax
from jax.experimental import pallas as pl
from jax.experimental.pallas import tpu as pltpu

F32 = jnp.float32
BF16 = jnp.bfloat16

D_MODEL = 1024
PAST_LEN = 16384
N_Q_HEADS = 8
N_KV_HEADS = 2
HEAD_DIM = 64
Q_PER_KV = N_Q_HEADS // N_KV_HEADS
ATTN_WIDTH = N_Q_HEADS * HEAD_DIM
KV_WIDTH = N_KV_HEADS * HEAD_DIM
WINDOW = 128
ROPE_THETA = 10000.0
SSM_WIDTH = D_MODEL // 2
SSM_GROUP = 16
N_SSM_GROUPS = SSM_WIDTH // SSM_GROUP
SSM_STATE = 64
N_MEM = 256
N_X_HEADS = 4
X_HEAD_DIM = D_MODEL // N_X_HEADS
D_FF = 2816
RMS_EPS = 1e-6
NEG_INF = -1e30
IN_SPLITS = (ATTN_WIDTH, KV_WIDTH, KV_WIDTH, SSM_WIDTH, D_MODEL, D_MODEL)
IN_WIDTH = sum(IN_SPLITS)
OFF_Q, OFF_K, OFF_V, OFF_U, OFF_GA, OFF_GB = (0, 512, 640, 768, 1280, 2304)

LANES = 128
SUBLANES = 8
BF16_SUBLANES = 16
VMEM_LIMIT = 60 * 1024 * 1024

SSM_CH = N_SSM_GROUPS * SSM_STATE
HALF_CH = SSM_CH // 2
HALF_U = SSM_WIDTH // 2
BLK_PER_HALF = 2 * HALF_CH // LANES
TM_FFN = 1024
TM_TAIL = 512
FFN_CHUNKS = ((0, 768), (768, 1536), (1536, 2304), (2304, D_FF))
TM_MIX = 1024
MEMKV_STEPS = 16
ATTN_QBLOCKS = 16
SSM_TB = 256
SSM_NB = 2
SSM_PITCH = SSM_TB + 4


def _cparams(sem):
    return pltpu.CompilerParams(dimension_semantics=sem, vmem_limit_bytes=VMEM_LIMIT)


def _const_spec(shape):
    nd = len(shape)
    return pl.BlockSpec(shape, lambda *_: (0,) * nd, pipeline_mode=pl.Buffered(1))


def _rms(x, g):
    return x * lax.rsqrt(jnp.mean(x * x, axis=-1, keepdims=True) + RMS_EPS) * g


def _dot(a, b):
    return jnp.dot(a, b, preferred_element_type=F32)


def _dot_nt(a, b):
    return lax.dot_general(a, b, (((1,), (1,)), ((), ())), preferred_element_type=F32)


def _row_spec(tm, width):
    return pl.BlockSpec((tm, width), lambda i: (i, 0))


def _two_array_specs(n_first, tm, width):
    return [pl.BlockSpec((tm, width), lambda i: (jnp.minimum(i, n_first - 1), 0)),
            pl.BlockSpec((tm, width), lambda i: (jnp.maximum(i - n_first, 0), 0))]


def _cast_stream_specs(weights, nsteps):
    in_specs, out_specs, out_shapes = [], [], []
    for w in weights:
        rows, cols = w.shape
        chunk = rows // nsteps
        assert chunk * nsteps == rows and chunk % BF16_SUBLANES == 0
        spec = pl.BlockSpec((chunk, cols), lambda i: (jnp.minimum(i, nsteps - 1), 0))
        in_specs.append(spec)
        out_specs.append(spec)
        out_shapes.append(jax.ShapeDtypeStruct(w.shape, BF16))
    return in_specs, out_specs, out_shapes


def _cast_chunks(src_refs, dst_refs):
    for src, dst in zip(src_refs, dst_refs, strict=True):
        dst[...] = src[...].astype(BF16)


def _store_by_step(is_first, first_ref, second_ref, value):
    @pl.when(is_first)
    def _():
        first_ref[...] = value

    @pl.when(jnp.logical_not(is_first))
    def _():
        second_ref[...] = value


def _ffn_residual(x, g_ref, wi_ref, wo_ref, between=()):
    h = _rms(x, g_ref[...]).astype(BF16)
    acc = None
    for c, (lo, hi) in enumerate(FFN_CHUNKS):
        a = _dot(h, wi_ref[:, lo:hi])
        b = _dot(h, wi_ref[:, D_FF + lo:D_FF + hi])
        part = _dot((a * jax.nn.sigmoid(a) * b).astype(BF16), wo_ref[lo:hi, :])
        acc = part if acc is None else acc + part
        if c < len(between):
            between[c]()
    return x + 0.5 * acc


def _ffn_first_kernel(xp_ref, xs_ref, g_ref, wi_ref, wo_ref, *rest, n_first, n_cast):
    cast_src, o_ref, cast_dst = rest[:n_cast], rest[n_cast], rest[n_cast + 1:]
    _cast_chunks(cast_src, cast_dst)
    x = jnp.where(pl.program_id(0) < n_first, xp_ref[...], xs_ref[...])
    o_ref[...] = _ffn_residual(x, g_ref, wi_ref, wo_ref)


def _ffn_first(xp, xs, g, wi, wo, cast_weights, *, tm):
    n_first, n_second = xp.shape[0] // tm, xs.shape[0] // tm
    c_in, c_out, c_shapes = _cast_stream_specs(cast_weights, n_first)
    return pl.pallas_call(
        functools.partial(_ffn_first_kernel, n_first=n_first, n_cast=len(cast_weights)),
        grid=(n_first + n_second,),
        in_specs=_two_array_specs(n_first, tm, D_MODEL)
                 + [_const_spec(g.shape), _const_spec(wi.shape), _const_spec(wo.shape)] + c_in,
        out_specs=[_row_spec(tm, D_MODEL)] + c_out,
        out_shape=[jax.ShapeDtypeStruct((xp.shape[0] + xs.shape[0], D_MODEL), F32)] + c_shapes,
        compiler_params=_cparams(("arbitrary",)),
        name="ffn",
    )(xp, xs, g, wi, wo, *cast_weights)


def _xattn_seq_probs(qb, mk):
    ts = qb.shape[0]
    krows = N_MEM * N_X_HEADS
    row = lax.broadcasted_iota(jnp.int32, (ts * N_X_HEADS, krows), 0)
    col = lax.broadcasted_iota(jnp.int32, (ts * N_X_HEADS, krows), 1)
    same_head = (row // ts) == (col & (N_X_HEADS - 1))
    qs = jnp.concatenate([qb[:, h * X_HEAD_DIM:(h + 1) * X_HEAD_DIM] for h in range(N_X_HEADS)], axis=0)
    s = jnp.where(same_head, _dot_nt(qs, mk.reshape(krows, X_HEAD_DIM).astype(BF16)), NEG_INF)
    p = jnp.exp(s - jnp.max(s, axis=-1, keepdims=True))
    return p.astype(BF16), jnp.sum(p, axis=-1, keepdims=True)


def _xattn_seq_values(p, denom, mv):
    ts = p.shape[0] // N_X_HEADS
    o = (_dot(p, mv.reshape(N_MEM * N_X_HEADS, X_HEAD_DIM).astype(BF16)) / denom).astype(BF16)
    return jnp.concatenate([o[h * ts:(h + 1) * ts] for h in range(N_X_HEADS)], axis=1)


def _tail_kernel(x_ref, oxp_ref, qxs_ref, mk_ref, mv_ref, wxo_ref, g_ref, wi_ref, wo_ref, gf_ref, yp_ref, ys_ref,
                 oxs_ref, *, n_first, nb, ts, tm):
    i = pl.program_id(0)
    is_first = i < n_first
    t_s = qxs_ref.shape[0]
    pair = 2 * ts

    @pl.when(i == 0)
    def _():
        oxs_ref[...] = jnp.zeros(oxs_ref.shape, oxs_ref.dtype)

    row0 = pl.multiple_of(jnp.maximum(i - n_first, 0) * tm, tm)
    ox = jnp.where(is_first, oxp_ref[...], oxs_ref[pl.ds(row0, tm), :])
    x = x_ref[...] + _dot(ox, wxo_ref[...])

    seq0 = jnp.minimum(i, n_first - 1) * nb
    pending = {}

    def probs_stage(j):
        src = pl.multiple_of((seq0 + 2 * j) * ts, pair)
        q2 = qxs_ref[pl.ds(src, pair), :]
        pending[j] = (src, [_xattn_seq_probs(q2[b * ts:(b + 1) * ts], mk_ref[2 * j + b]) for b in range(2)])

    def values_stage(j):
        src, probs = pending.pop(j)
        o2 = jnp.concatenate([_xattn_seq_values(p, denom, mv_ref[2 * j + b]) for b, (p, denom) in enumerate(probs)],
                             axis=0)
        dst = pl.multiple_of(jnp.where(is_first, src, t_s), pair)
        oxs_ref[pl.ds(dst, pair), :] = o2

    n_pairs = nb // 2
    assert 1 <= n_pairs < len(FFN_CHUNKS)

    def after_chunk(c):
        def run():
            values_stage(c)
            if c + 1 < n_pairs:
                probs_stage(c + 1)
        return run

    probs_stage(0)
    y = _rms(_ffn_residual(x, g_ref, wi_ref, wo_ref, between=[after_chunk(c) for c in range(n_pairs)]), gf_ref[...])
    _store_by_step(is_first, yp_ref, ys_ref, y)


def _tail(x2, ox_p, qx_s, mk, mv, wxo, g, wi, wo, gf, *, tm, ts):
    t_p, t_s, nseq = ox_p.shape[0], qx_s.shape[0], mk.shape[0]
    n_first, n_second = t_p // tm, t_s // tm
    nb = nseq // n_first
    assert nb * n_first == nseq and nb % 2 == 0 and nseq * ts == t_s
    mem = pl.BlockSpec((nb, N_MEM, N_X_HEADS, X_HEAD_DIM), lambda i: (jnp.minimum(i, n_first - 1), 0, 0, 0))
    return pl.pallas_call(
        functools.partial(_tail_kernel, n_first=n_first, nb=nb, ts=ts, tm=tm),
        grid=(n_first + n_second,),
        in_specs=[_row_spec(tm, D_MODEL), _two_array_specs(n_first, tm, D_MODEL)[0], _const_spec(qx_s.shape), mem, mem]
                 + [_const_spec(c.shape) for c in (wxo, g, wi, wo, gf)],
        out_specs=_two_array_specs(n_first, tm, D_MODEL),
        out_shape=[jax.ShapeDtypeStruct((t_p, D_MODEL), F32), jax.ShapeDtypeStruct((t_s, D_MODEL), F32)],
        scratch_shapes=[pltpu.VMEM((t_s + 2 * ts, D_MODEL), BF16)],
        compiler_params=_cparams(("arbitrary",)),
        name="tail",
    )(x2, ox_p, qx_s, mk, mv, wxo, g, wi, wo, gf)


def _inproj_kernel(x_ref, g_ref, w_ref, ca_ref, sa_ref, cb_ref, sb_ref, q_ref, k_ref, v_ref, u_ref, ga_ref, gb_ref):
    h = _rms(x_ref[...], g_ref[...]).astype(BF16)
    ca, sa, cb, sb = ca_ref[0], sa_ref[0], cb_ref[0], sb_ref[0]
    cos = ca * cb - sa * sb
    sin = sa * cb + ca * sb
    lane = lax.broadcasted_iota(jnp.int32, cos.shape, 1)
    first_half = (lane & (HEAD_DIM - 1)) < (HEAD_DIM // 2)

    def rope(xc):
        rot = jnp.where(first_half, pltpu.roll(xc, LANES - HEAD_DIM // 2, 1), pltpu.roll(xc, HEAD_DIM // 2, 1))
        return xc * cos + rot * sin

    ga_ref[...] = jax.nn.sigmoid(_dot(h, w_ref[:, OFF_GA:OFF_GB])).astype(BF16)
    gb_ref[...] = jax.nn.sigmoid(_dot(h, w_ref[:, OFF_GB:])).astype(BF16)
    yq = _dot(h, w_ref[:, OFF_Q:OFF_K])
    for c in range(ATTN_WIDTH // LANES):
        q_ref[:, c * LANES:(c + 1) * LANES] = (rope(yq[:, c * LANES:(c + 1) * LANES]) * (HEAD_DIM ** -0.5)).astype(BF16)
    ykvu = _dot(h, w_ref[:, OFF_K:OFF_GA])
    k_ref[...] = rope(ykvu[:, :KV_WIDTH])
    v_ref[...] = ykvu[:, KV_WIDTH:2 * KV_WIDTH]
    u_ref[...] = ykvu[:, 2 * KV_WIDTH:]


def _inproj(x, g, w, rope_tabs, *, tm, n_first):
    t = x.shape[0]
    row = lambda wd: _row_spec(tm, wd)
    step_tab = pl.BlockSpec((1, 1, LANES), lambda i: (i, 0, 0))
    row_tab = pl.BlockSpec((1, tm, LANES), lambda i: (jnp.minimum(i // n_first, 1), 0, 0))
    widths = (ATTN_WIDTH, KV_WIDTH, KV_WIDTH, SSM_WIDTH, D_MODEL, D_MODEL)
    dtypes = (BF16, F32, F32, F32, BF16, BF16)
    return pl.pallas_call(
        _inproj_kernel,
        grid=(t // tm,),
        in_specs=[row(D_MODEL), _const_spec((1, D_MODEL)), _const_spec((D_MODEL, IN_WIDTH)), step_tab, step_tab,
                  row_tab, row_tab],
        out_specs=[row(wd) for wd in widths],
        out_shape=[jax.ShapeDtypeStruct((t, wd), dt) for wd, dt in zip(widths, dtypes)],
        compiler_params=_cparams(("parallel",)),
        name="inproj",
    )(x, g, w, *rope_tabs)


def _rope_tables(step_pos, row_pos):
    half = HEAD_DIM // 2
    inv = ROPE_THETA ** (-jnp.arange(half, dtype=F32) / half)
    sign = jnp.concatenate([-jnp.ones((half,), F32), jnp.ones((half,), F32)])

    def tabs(pos):
        ang = pos.astype(F32)[..., None] * inv
        cos = jnp.tile(jnp.cos(ang), LANES // half)
        sin = jnp.tile(jnp.tile(jnp.sin(ang), 2) * sign, LANES // HEAD_DIM)
        return cos, sin

    ca, sa = tabs(step_pos[:, None])
    cb, sb = tabs(row_pos)
    return ca, sa, cb, sb


def _attn_prompt_kernel(sinks_ref, q_ref, kp_ref, kc_ref, vp_ref, vc_ref, *rest, n_cast):
    cast_src, o_ref, cast_dst = rest[:n_cast], rest[n_cast], rest[n_cast + 1:]
    _cast_chunks(cast_src, cast_dst)
    i = pl.program_id(0)
    r = lax.broadcasted_iota(jnp.int32, (WINDOW, 2 * WINDOW), 0)
    c = lax.broadcasted_iota(jnp.int32, (WINDOW, 2 * WINDOW), 1)
    band = (c >= r) & (c <= WINDOW + r)
    first_mask = band & ((c >= WINDOW) | (i > 0))
    kcat = jnp.concatenate([kp_ref[...], kc_ref[...]], axis=0)
    vcat = jnp.concatenate([vp_ref[...], vc_ref[...]], axis=0)
    kswap = pltpu.roll(kcat, HEAD_DIM, 1)
    vswap = pltpu.roll(vcat, HEAD_DIM, 1)
    lo = lax.broadcasted_iota(jnp.int32, kcat.shape, 1) < HEAD_DIM
    lo_q = lax.broadcasted_iota(jnp.int32, (WINDOW, LANES), 1) < HEAD_DIM
    ones = jnp.ones(kcat.shape, F32)
    for kv in range(N_KV_HEADS):
        k_own, k_other = (kcat, kswap) if kv == 0 else (kswap, kcat)
        v_own, v_other = (vcat, vswap) if kv == 0 else (vswap, vcat)
        k_half = (jnp.where(lo, k_own, 0.0).astype(BF16), jnp.where(lo, 0.0, k_other).astype(BF16))
        rhs_half = (jnp.concatenate([jnp.where(lo, v_own, 0.0), jnp.where(lo, ones, 0.0)], axis=1).astype(BF16),
                    jnp.concatenate([jnp.where(lo, 0.0, v_other), jnp.where(lo, 0.0, ones)], axis=1).astype(BF16))
        for j in range(ATTN_QBLOCKS):
            qrows = slice(j * WINDOW, (j + 1) * WINDOW)
            krows = slice(j * WINDOW, (j + 2) * WINDOW)
            mask = first_mask if j == 0 else band
            for c in range(Q_PER_KV // 2):
                tile = kv * (Q_PER_KV // 2) + c
                qt = q_ref[qrows, tile * LANES:(tile + 1) * LANES]
                acc = None
                sink_terms = []
                for side in range(2):
                    s = jnp.where(mask, _dot_nt(qt, k_half[side][krows]), NEG_INF)
                    sink = sinks_ref[2 * tile + side]
                    m = jnp.maximum(jnp.max(s, axis=-1, keepdims=True), sink)
                    pv = _dot(jnp.exp((s - m).astype(BF16)), rhs_half[side][krows])
                    acc = pv if acc is None else acc + pv
                    sink_terms.append(jnp.exp(sink - m))
                denom = acc[:, LANES:] + jnp.where(lo_q, sink_terms[0], sink_terms[1])
                o_ref[qrows, tile * LANES:(tile + 1) * LANES] = (acc[:, :LANES] / denom).astype(BF16)


def _attn_prompt(sinks, q, k, v, cast_weights, *, t):
    rows = ATTN_QBLOCKS * WINDOW
    cur = lambda w: pl.BlockSpec((rows, w), lambda i: (i, 0))
    prev = lambda w: pl.BlockSpec((WINDOW, w), lambda i: (jnp.maximum(i * ATTN_QBLOCKS - 1, 0), 0))
    c_in, c_out, c_shapes = _cast_stream_specs(cast_weights, t // rows)
    return pl.pallas_call(
        functools.partial(_attn_prompt_kernel, n_cast=len(cast_weights)),
        grid=(t // rows,),
        in_specs=[pl.BlockSpec(memory_space=pltpu.SMEM), cur(ATTN_WIDTH), prev(KV_WIDTH), cur(KV_WIDTH),
                  prev(KV_WIDTH), cur(KV_WIDTH)] + c_in,
        out_specs=[cur(ATTN_WIDTH)] + c_out,
        out_shape=[jax.ShapeDtypeStruct((t, ATTN_WIDTH), BF16)] + c_shapes,
        compiler_params=_cparams(("parallel",)),
        name="attn_prompt",
    )(sinks, q, k, k, v, v, *cast_weights)


def _attn_sample_kernel(sinks_ref, q_ref, k_ref, v_ref, ckt_ref, cvt_ref, o_ref, nkt_ref, nvt_ref, *, nb, ts):
    rows = nb * ts
    q = q_ref[...]
    pad = jnp.zeros((LANES - rows, KV_WIDTH), F32)
    knt = jnp.concatenate([k_ref[...], pad], axis=0).T
    vnt = jnp.concatenate([v_ref[...], pad], axis=0).T
    lane = lax.broadcasted_iota(jnp.int32, (HEAD_DIM, WINDOW), 1)
    is_new = lane >= WINDOW - ts
    for b in range(nb):
        knt_b = pltpu.roll(knt, (WINDOW - ts - b * ts) % LANES, 1)
        vnt_b = pltpu.roll(vnt, (WINDOW - ts - b * ts) % LANES, 1)
        for kv in range(N_KV_HEADS):
            ksl = slice(kv * HEAD_DIM, (kv + 1) * HEAD_DIM)
            nkt_ref[b, kv] = jnp.where(is_new, knt_b[ksl], pltpu.roll(ckt_ref[b, kv], WINDOW - ts, 1))
            nvt_ref[b, kv] = jnp.where(is_new, vnt_b[ksl], pltpu.roll(cvt_ref[b, kv], WINDOW - ts, 1))

    knt = knt[:, :rows].astype(BF16)
    vnt = vnt[:, :rows].astype(BF16)
    grows = Q_PER_KV * rows
    row1 = lax.broadcasted_iota(jnp.int32, (grows, nb * WINDOW), 0)
    col1 = lax.broadcasted_iota(jnp.int32, (grows, nb * WINDOW), 1)
    rseq1 = (row1 % rows) // ts
    mask_ctx = (rseq1 == col1 // WINDOW) & (col1 % WINDOW >= row1 % ts)
    row2 = lax.broadcasted_iota(jnp.int32, (grows, rows), 0)
    col2 = lax.broadcasted_iota(jnp.int32, (grows, rows), 1)
    mask_new = ((row2 % rows) // ts == col2 // ts) & (col2 % ts <= row2 % ts)
    ghead = lax.broadcasted_iota(jnp.int32, (grows, 1), 0) // rows
    for kv in range(N_KV_HEADS):
        ksl = slice(kv * HEAD_DIM, (kv + 1) * HEAD_DIM)
        qg = jnp.concatenate([q[:, (kv * Q_PER_KV + g) * HEAD_DIM:(kv * Q_PER_KV + g + 1) * HEAD_DIM]
                              for g in range(Q_PER_KV)], axis=0)
        kctx = jnp.concatenate([ckt_ref[b, kv] for b in range(nb)], axis=1).astype(BF16)
        vctx = jnp.concatenate([cvt_ref[b, kv] for b in range(nb)], axis=1).astype(BF16)
        s1 = jnp.where(mask_ctx, _dot(qg, kctx), NEG_INF)
        s2 = jnp.where(mask_new, _dot(qg, knt[ksl]), NEG_INF)
        sink = jnp.zeros((grows, 1), F32)
        for g in range(Q_PER_KV):
            sink = jnp.where(ghead == g, sinks_ref[kv * Q_PER_KV + g], sink)
        m = jnp.maximum(jnp.maximum(jnp.max(s1, axis=-1, keepdims=True), jnp.max(s2, axis=-1, keepdims=True)), sink)
        p1 = jnp.exp(s1 - m)
        p2 = jnp.exp(s2 - m)
        denom = jnp.sum(p1, axis=-1, keepdims=True) + jnp.sum(p2, axis=-1, keepdims=True) + jnp.exp(sink - m)
        o = (_dot_nt(p1.astype(BF16), vctx) + _dot_nt(p2.astype(BF16), vnt[ksl])) / denom
        for g in range(Q_PER_KV):
            h = kv * Q_PER_KV + g
            o_ref[:, h * HEAD_DIM:(h + 1) * HEAD_DIM] = o[g * rows:(g + 1) * rows].astype(BF16)


def _attn_sample(sinks, q, k, v, ckt, cvt, *, nb, ts, row0):
    nseq = ckt.shape[0]
    rows = nb * ts
    assert rows <= LANES and row0 % rows == 0
    row_in = lambda w: pl.BlockSpec((rows, w), lambda i: (i + row0 // rows, 0))
    win = pl.BlockSpec((nb, N_KV_HEADS, HEAD_DIM, WINDOW), lambda i: (i, 0, 0, 0))
    win_shape = jax.ShapeDtypeStruct((nseq, N_KV_HEADS, HEAD_DIM, WINDOW), F32)
    return pl.pallas_call(
        functools.partial(_attn_sample_kernel, nb=nb, ts=ts),
        grid=(nseq // nb,),
        in_specs=[pl.BlockSpec(memory_space=pltpu.SMEM), row_in(ATTN_WIDTH), row_in(KV_WIDTH), row_in(KV_WIDTH),
                  win, win],
        out_specs=[_row_spec(rows, ATTN_WIDTH), win, win],
        out_shape=[jax.ShapeDtypeStruct((nseq * ts, ATTN_WIDTH), BF16), win_shape, win_shape],
        compiler_params=_cparams(("parallel",)),
        name="attn_sample",
    )(sinks, q, k, v, ckt, cvt)


def _ssm_prep_kernel(are_ref, aim_ref, ldt_ref, bre_ref, bim_ref, lre_ref, lim_ref, bbre_ref, bbim_ref):
    a_re = are_ref[...]
    a_im = aim_ref[...]
    dt = jnp.exp(ldt_ref[...])
    mag = jnp.exp(a_re * dt)
    lb_re = mag * jnp.cos(a_im * dt)
    lb_im = mag * jnp.sin(a_im * dt)
    den = a_re * a_re + a_im * a_im
    nr = lb_re - 1.0
    ni = lb_im
    k_re = (nr * a_re + ni * a_im) / den
    k_im = (ni * a_re - nr * a_im) / den
    lre_ref[...] = lb_re
    lim_ref[...] = lb_im
    b_re = bre_ref[...]
    b_im = bim_ref[...]
    bbre_ref[...] = k_re[:, None, :] * b_re - k_im[:, None, :] * b_im
    bbim_ref[...] = k_re[:, None, :] * b_im + k_im[:, None, :] * b_re


def _ssm_prep(a_re, a_im, log_dt, b_re, b_im):
    g, n, gs = b_re.shape
    sds = jax.ShapeDtypeStruct
    return pl.pallas_call(
        _ssm_prep_kernel,
        out_shape=[sds((g, n), F32), sds((g, n), F32), sds((g, gs, n), F32), sds((g, gs, n), F32)],
        name="ssm_prep",
    )(a_re, a_im, log_dt.reshape(g, 1), jnp.swapaxes(b_re, 1, 2), jnp.swapaxes(b_im, 1, 2))


def _block_diag_halves(m):
    g, a, b = m.shape
    gh = g // 2
    tiled = jnp.tile(m.reshape(2, gh * a, b), (1, 1, gh))
    row_group = lax.broadcasted_iota(jnp.int32, tiled.shape, 1) // a
    col_group = lax.broadcasted_iota(jnp.int32, tiled.shape, 2) // b
    return jnp.where(row_group == col_group, tiled, 0.0)


def _ssm_y(u, scan_block, cc_ref, d_ref):
    ys = []
    for half in range(2):
        hc = jnp.concatenate([scan_block(half, jj) for jj in range(BLK_PER_HALF)], axis=1).astype(BF16)
        ys.append(_dot(hc, cc_ref[half]))
    y = jnp.concatenate(ys, axis=1) + d_ref[...] * u
    return jax.nn.gelu(y).astype(BF16)


def _ssm_prompt_kernel(u_ref, bb_ref, cc_ref, lam_ref, d_ref, h0_ref, z_ref, hout_ref,
                       x0_ref, x1_ref, s0_ref, s1_ref, hst_ref):
    tb, pitch = SSM_TB, SSM_PITCH
    x_bufs = (x0_ref, x1_ref)
    s_bufs = (s0_ref, s1_ref)

    @pl.when(pl.program_id(0) == 0)
    def _():
        hst_ref[...] = h0_ref[...]

    def b_proj(k):
        ub = u_ref[k * tb:(k + 1) * tb, :].astype(BF16)
        for half in range(2):
            x = _dot(ub[:, half * HALF_U:(half + 1) * HALF_U], bb_ref[half])
            for jj in range(BLK_PER_HALF):
                x_bufs[k % 2][pl.ds((half * BLK_PER_HALF + jj) * pitch, tb), :] = x[:, jj * LANES:(jj + 1) * LANES]

    lam = [lam_ref[g] for g in range(4)]
    carry = [hst_ref[g] for g in range(4)]

    def scan(k):
        for t in range(tb):
            for half in range(2):
                hr, hi = carry[2 * half], carry[2 * half + 1]
                lr, li = lam[2 * half], lam[2 * half + 1]
                rows_re = pl.ds((half * BLK_PER_HALF) * pitch + t, SUBLANES, stride=pitch)
                rows_im = pl.ds((half * BLK_PER_HALF + SUBLANES) * pitch + t, SUBLANES, stride=pitch)
                nr = lr * hr - li * hi + x_bufs[k % 2][rows_re, :]
                ni = lr * hi + li * hr + x_bufs[k % 2][rows_im, :]
                s_bufs[k % 2][rows_re, :] = nr
                s_bufs[k % 2][rows_im, :] = ni
                carry[2 * half], carry[2 * half + 1] = nr, ni

    def c_proj(k):
        z_ref[k * tb:(k + 1) * tb, :] = _ssm_y(
            u_ref[k * tb:(k + 1) * tb, :],
            lambda half, jj: s_bufs[k % 2][pl.ds((half * BLK_PER_HALF + jj) * pitch, tb), :], cc_ref, d_ref)

    b_proj(0)
    for k in range(SSM_NB):
        if k + 1 < SSM_NB:
            b_proj(k + 1)
        scan(k)
        c_proj(k)
    for g in range(4):
        hst_ref[g] = carry[g]
        hout_ref[g] = carry[g]


def _ssm_prompt(u, bb, cc, lam, d, h0, *, t):
    rows = SSM_NB * SSM_TB
    return pl.pallas_call(
        _ssm_prompt_kernel,
        grid=(t // rows,),
        in_specs=[pl.BlockSpec((rows, SSM_WIDTH), lambda i: (i, 0)), _const_spec(bb.shape), _const_spec(cc.shape),
                  _const_spec(lam.shape), _const_spec(d.shape), _const_spec(h0.shape)],
        out_specs=[pl.BlockSpec((rows, SSM_WIDTH), lambda i: (i, 0)),
                   pl.BlockSpec((4, SUBLANES, LANES), lambda i: (0, 0, 0))],
        out_shape=[jax.ShapeDtypeStruct((t, SSM_WIDTH), BF16), jax.ShapeDtypeStruct((4, SUBLANES, LANES), F32)],
        scratch_shapes=[pltpu.VMEM((2 * BLK_PER_HALF * SSM_PITCH, LANES), F32)] * 4
                       + [pltpu.VMEM((4, SUBLANES, LANES), F32)],
        compiler_params=_cparams(("arbitrary",)),
        name="ssm_prompt",
    )(u, bb, cc, lam, d, h0)


def _ssm_sample_kernel(u_ref, bb_ref, cc_ref, lre_ref, lim_ref, d_ref, h0re_ref, h0im_ref,
                       z_ref, hre_ref, him_ref, xs_ref, *, nb, ts):
    u = u_ref[...]
    ub = u.astype(BF16)
    for half in range(2):
        x = _dot(ub[:, half * HALF_U:(half + 1) * HALF_U], bb_ref[half])
        for jj in range(BLK_PER_HALF):
            xs_ref[half * BLK_PER_HALF + jj] = x[:, jj * LANES:(jj + 1) * LANES]
    for cb in range(SSM_CH // LANES):
        half, jj = divmod(cb, SUBLANES)
        j_re = half * BLK_PER_HALF + jj
        j_im = j_re + SUBLANES
        csl = slice(cb * LANES, (cb + 1) * LANES)
        lr = lre_ref[:, csl]
        li = lim_ref[:, csl]
        hr = h0re_ref[:, csl]
        hi = h0im_ref[:, csl]
        for t in range(ts):
            rows = pl.ds(t, nb, stride=ts)
            nr = lr * hr - li * hi + xs_ref[j_re, rows, :]
            ni = lr * hi + li * hr + xs_ref[j_im, rows, :]
            xs_ref[j_re, rows, :] = nr
            xs_ref[j_im, rows, :] = ni
            hr, hi = nr, ni
        hre_ref[:, csl] = hr
        him_ref[:, csl] = hi
    z_ref[...] = _ssm_y(u, lambda half, jj: xs_ref[half * BLK_PER_HALF + jj], cc_ref, d_ref)


def _ssm_sample(u, bb, cc, lre, lim, d, h0re, h0im, *, nb, ts, row0):
    nseq = h0re.shape[0]
    rows = nb * ts
    assert row0 % rows == 0
    st = pl.BlockSpec((nb, SSM_CH), lambda i: (i, 0))
    return pl.pallas_call(
        functools.partial(_ssm_sample_kernel, nb=nb, ts=ts),
        grid=(nseq // nb,),
        in_specs=[pl.BlockSpec((rows, SSM_WIDTH), lambda i: (i + row0 // rows, 0)), _const_spec(bb.shape),
                  _const_spec(cc.shape),
                  _const_spec(lre.shape), _const_spec(lim.shape), _const_spec(d.shape), st, st],
        out_specs=[pl.BlockSpec((rows, SSM_WIDTH), lambda i: (i, 0)), st, st],
        out_shape=[jax.ShapeDtypeStruct((nseq * ts, SSM_WIDTH), BF16),
                   jax.ShapeDtypeStruct((nseq, SSM_CH), F32), jax.ShapeDtypeStruct((nseq, SSM_CH), F32)],
        scratch_shapes=[pltpu.VMEM((2 * BLK_PER_HALF, rows, LANES), F32)],
        compiler_params=_cparams(("parallel",)),
        name="ssm_sample",
    )(u, bb, cc, lre, lim, d, h0re, h0im)


def _xattn_heads(q, mk, mv):
    outs = []
    for h in range(N_X_HEADS):
        sl = slice(h * X_HEAD_DIM, (h + 1) * X_HEAD_DIM)
        s = _dot_nt(q[:, sl], mk[:, sl])
        p = jnp.exp(s - jnp.max(s, axis=-1, keepdims=True))
        outs.append(_dot(p.astype(BF16), mv[:, sl]) / jnp.sum(p, axis=-1, keepdims=True))
    return jnp.concatenate(outs, axis=1)


def _merge_kernel(x1_ref, op_ref, os_ref, zp_ref, zs_ref, ga_ref, gb_ref, wup_ref, wglu_ref, wout_ref, gx_ref,
                  wxq_ref, mk_ref, mv_ref, x2_ref, oxp_ref, qxs_ref, *, n_first):
    is_first = pl.program_id(0) < n_first
    o = jnp.where(is_first, op_ref[...], os_ref[...])
    z = jnp.where(is_first, zp_ref[...], zs_ref[...])
    ya = _dot(o, wup_ref[...])
    glu = _dot(z, wglu_ref[...])
    yb = glu[:, :D_MODEL] * jax.nn.sigmoid(glu[:, D_MODEL:])
    merged = ga_ref[...].astype(F32) * ya + gb_ref[...].astype(F32) * yb
    x2 = x1_ref[...] + _dot(merged.astype(BF16), wout_ref[...])
    x2_ref[...] = x2
    hx = _rms(x2, gx_ref[...]).astype(BF16)
    qx = (_dot(hx, wxq_ref[...]) * (X_HEAD_DIM ** -0.5)).astype(BF16)
    ox = _xattn_heads(qx, mk_ref[...].astype(BF16), mv_ref[...].astype(BF16)).astype(BF16)

    @pl.when(is_first)
    def _():
        oxp_ref[...] = ox

    @pl.when(jnp.logical_not(is_first))
    def _():
        qxs_ref[...] = qx


def _merge(x1, o_p, o_s, z_p, z_s, ga, gb, wup, wglu, wout, gx, wxq, mk, mv, *, tm):
    n_first, n_second = o_p.shape[0] // tm, o_s.shape[0] // tm
    row = lambda w: _row_spec(tm, w)
    consts = [wup, wglu, wout, gx, wxq, mk, mv]
    return pl.pallas_call(
        functools.partial(_merge_kernel, n_first=n_first),
        grid=(n_first + n_second,),
        in_specs=[row(D_MODEL)] + _two_array_specs(n_first, tm, ATTN_WIDTH) + _two_array_specs(n_first, tm, SSM_WIDTH)
                 + [row(D_MODEL), row(D_MODEL)] + [_const_spec(c.shape) for c in consts],
        out_specs=[row(D_MODEL)] + _two_array_specs(n_first, tm, D_MODEL),
        out_shape=[jax.ShapeDtypeStruct(x1.shape, F32), jax.ShapeDtypeStruct((o_p.shape[0], D_MODEL), BF16),
                   jax.ShapeDtypeStruct((o_s.shape[0], D_MODEL), BF16)],
        compiler_params=_cparams(("arbitrary",)),
        name="merge",
    )(x1, o_p, o_s, z_p, z_s, ga, gb, *consts)


def _memkv_kernel(mem_ref, g_ref, wk_ref, wv_ref, *rest, n_cast):
    cast_src, (mk_ref, mv_ref), cast_dst = rest[:n_cast], rest[n_cast:n_cast + 2], rest[n_cast + 2:]
    _cast_chunks(cast_src, cast_dst)

    @pl.when(pl.program_id(0) == 0)
    def _():
        mn = _rms(mem_ref[...], g_ref[...]).astype(BF16)
        mk_ref[...] = _dot(mn, wk_ref[...].astype(BF16))
        mv_ref[...] = _dot(mn, wv_ref[...].astype(BF16))


def _memkv(mem, g, wk, wv, cast_weights):
    sds = jax.ShapeDtypeStruct((mem.shape[0], D_MODEL), F32)
    c_in, c_out, c_shapes = _cast_stream_specs(cast_weights, MEMKV_STEPS)
    consts = (mem, g, wk, wv)
    return pl.pallas_call(
        functools.partial(_memkv_kernel, n_cast=len(cast_weights)),
        grid=(MEMKV_STEPS,),
        in_specs=[_const_spec(c.shape) for c in consts] + c_in,
        out_specs=[pl.BlockSpec(sds.shape, lambda i: (0, 0))] * 2 + c_out,
        out_shape=[sds, sds] + c_shapes,
        compiler_params=_cparams(("arbitrary",)),
        name="memkv",
    )(*consts, *cast_weights)


def kernel(x_prompt, x_sample, cache_win_k, cache_win_v, state_ssm_re, state_ssm_im, cache_mem_k, cache_mem_v, mem_prompt, g_ffn1, w_ffn1_in, w_ffn1_out, g_mix, w_in, attn_sinks, ssm_a_re, ssm_a_im, ssm_log_dt, ssm_b_re, ssm_b_im, ssm_c_re, ssm_c_im, ssm_d, w_attn_up, w_ssm_glu, w_out, g_xattn, g_mem, w_xq, w_xk, w_xv, w_xo, g_ffn2, w_ffn2_in, w_ffn2_out, g_final):
    assert x_prompt.shape[0] == 1 and g_ffn1.shape[0] == 1
    seq = x_prompt.shape[1]
    nseq, ts = x_sample.shape[0], x_sample.shape[1]
    past_len = PAST_LEN
    l = 0
    vec = lambda g: g[l].reshape(1, -1)
    gf = g_final.reshape(1, -1)
    sinks = attn_sinks[l]

    lam_re, lam_im, bbt_re, bbt_im = _ssm_prep(ssm_a_re[l], ssm_a_im[l], ssm_log_dt[l], ssm_b_re[l], ssm_b_im[l])
    bb = jnp.concatenate([_block_diag_halves(bbt_re), _block_diag_halves(bbt_im)], axis=-1).astype(BF16)
    ct_re = jnp.swapaxes(ssm_c_re[l], 1, 2)
    ct_im = jnp.swapaxes(ssm_c_im[l], 1, 2)
    cc = jnp.concatenate([_block_diag_halves(ct_re), -_block_diag_halves(ct_im)], axis=1).astype(BF16)
    d_skip = ssm_d[l].reshape(1, -1)
    lre16 = lam_re.reshape(2, SUBLANES, LANES)
    lim16 = lam_im.reshape(2, SUBLANES, LANES)
    lam_tm = jnp.stack([lre16[0], lim16[0], lre16[1], lim16[1]])
    lre_row = lam_re.reshape(1, SSM_CH)
    lim_row = lam_im.reshape(1, SSM_CH)

    mk_p, mv_p, w1i, w1o = _memkv(mem_prompt[0], vec(g_mem), w_xk[l], w_xv[l], [w_ffn1_in[l], w_ffn1_out[l]])

    tm = TM_MIX
    t_s = nseq * ts
    assert seq % tm == 0 and t_s % tm == 0 and tm % ts == 0
    n_p, n_s = seq // tm, t_s // tm
    rope_tabs = _rope_tables(jnp.concatenate([jnp.arange(n_p) * tm, jnp.full((n_s,), past_len)]),
                             jnp.stack([jnp.arange(tm), jnp.arange(tm) % ts]))

    x1, win = _ffn_first(x_prompt[0], x_sample.reshape(t_s, D_MODEL), vec(g_ffn1), w1i, w1o, [w_in[l]], tm=TM_FFN)
    q, k, v, u, ga, gb = _inproj(x1, vec(g_mix), win, rope_tabs, tm=tm, n_first=n_p)

    o_p, wup, wglu, wout, wxq, wxo, w2i, w2o = _attn_prompt(
        sinks, q, k, v, [w_attn_up[l], w_ssm_glu[l], w_out[l], w_xq[l], w_xo[l], w_ffn2_in[l], w_ffn2_out[l]], t=seq)
    nk_p, nv_p = k[seq - WINDOW:seq], v[seq - WINDOW:seq]
    to_t = lambda c: jnp.transpose(c[l], (0, 2, 3, 1))
    o_s, nkt, nvt = _attn_sample(sinks, q, k, v, to_t(cache_win_k), to_t(cache_win_v), nb=8, ts=ts, row0=seq)
    nk_s, nv_s = jnp.transpose(nkt, (0, 3, 1, 2)), jnp.transpose(nvt, (0, 3, 1, 2))

    z_p, hout_p = _ssm_prompt(u, bb, cc, lam_tm, d_skip, jnp.zeros((4, SUBLANES, LANES), F32), t=seq)
    hre_p = jnp.concatenate([hout_p[0], hout_p[2]], axis=0)
    him_p = jnp.concatenate([hout_p[1], hout_p[3]], axis=0)
    z_s, hre_s, him_s = _ssm_sample(u, bb, cc, lre_row, lim_row, d_skip, state_ssm_re[l].reshape(nseq, SSM_CH),
                                    state_ssm_im[l].reshape(nseq, SSM_CH), nb=64, ts=ts, row0=seq)

    x2, ox_p, qx_s = _merge(x1, o_p, o_s, z_p, z_s, ga, gb, wup, wglu, wout, vec(g_xattn), wxq, mk_p, mv_p, tm=tm)
    y_p, y_s = _tail(x2, ox_p, qx_s, cache_mem_k[l], cache_mem_v[l], wxo, vec(g_ffn2), w2i, w2o, gf,
                     tm=TM_TAIL, ts=ts)

    kvshape = (1, 1, WINDOW, N_KV_HEADS, HEAD_DIM)
    stshape = (1, 1, N_SSM_GROUPS, SSM_STATE)
    memshape = (1, 1, N_MEM, N_X_HEADS, X_HEAD_DIM)
    return (y_p.reshape(1, seq, D_MODEL), y_s.reshape(nseq, ts, D_MODEL),
            nk_p.reshape(kvshape), nv_p.reshape(kvshape), hre_p.reshape(stshape), him_p.reshape(stshape),
            mk_p.reshape(memshape), mv_p.reshape(memshape),
            nk_s.reshape(1, nseq, WINDOW, N_KV_HEADS, HEAD_DIM), nv_s.reshape(1, nseq, WINDOW, N_KV_HEADS, HEAD_DIM),
            hre_s.reshape(1, nseq, N_SSM_GROUPS, SSM_STATE), him_s.reshape(1, nseq, N_SSM_GROUPS, SSM_STATE))
```

```python
import functools

import jax
import jax.numpy as jnp
from jax import lax
from jax.experimental import pallas as pl
from jax.experimental.pallas import tpu as pltpu

F32 = jnp.float32
BF16 = jnp.bfloat16

D_MODEL = 1024
PAST_LEN = 16384
N_Q_HEADS = 8
N_KV_HEADS = 2
HEAD_DIM = 64
Q_PER_KV = N_Q_HEADS // N_KV_HEADS
ATTN_WIDTH = N_Q_HEADS * HEAD_DIM
KV_WIDTH = N_KV_HEADS * HEAD_DIM
WINDOW = 128
ROPE_THETA = 10000.0
SSM_WIDTH = D_MODEL // 2
SSM_GROUP = 16
N_SSM_GROUPS = SSM_WIDTH // SSM_GROUP
SSM_STATE = 64
N_MEM = 256
N_X_HEADS = 4
X_HEAD_DIM = D_MODEL // N_X_HEADS
D_FF = 2816
RMS_EPS = 1e-6
NEG_INF = -1e30
IN_SPLITS = (ATTN_WIDTH, KV_WIDTH, KV_WIDTH, SSM_WIDTH, D_MODEL, D_MODEL)
IN_WIDTH = sum(IN_SPLITS)
OFF_Q, OFF_K, OFF_GA, OFF_GB = (0, ATTN_WIDTH, ATTN_WIDTH + 2 * KV_WIDTH + SSM_WIDTH,
                                ATTN_WIDTH + 2 * KV_WIDTH + SSM_WIDTH + D_MODEL)

LANES = 128
SUBLANES = 8
BF16_SUBLANES = 16
VMEM_LIMIT = 60 * 1024 * 1024

SSM_CH = N_SSM_GROUPS * SSM_STATE
HALF_CH = SSM_CH // 2
HALF_U = SSM_WIDTH // 2
BLK_PER_HALF = 2 * HALF_CH // LANES
TM_FFN = 1024
TM_TAIL = 512
FFN_CHUNKS = ((0, 768), (768, 1536), (1536, 2304), (2304, D_FF))
TM_MIX = 1024
ATTN_QBLOCKS = 16
ATTN_SAMPLE_SEQS = 8
SSM_SAMPLE_SEQS = 64
SSM_TB = 256
SSM_NB = 2
SSM_PITCH = SSM_TB + 4


def _cparams(sem):
    return pltpu.CompilerParams(dimension_semantics=sem, vmem_limit_bytes=VMEM_LIMIT)


def _const_spec(shape):
    nd = len(shape)
    return pl.BlockSpec(shape, lambda *_: (0,) * nd, pipeline_mode=pl.Buffered(1))


def _rms(x, g):
    return x * lax.rsqrt(jnp.mean(x * x, axis=-1, keepdims=True) + RMS_EPS) * g


def _dot(a, b):
    return jnp.dot(a, b, preferred_element_type=F32)


def _dot_nt(a, b):
    return lax.dot_general(a, b, (((1,), (1,)), ((), ())), preferred_element_type=F32)


def _row_spec(tm, width):
    return pl.BlockSpec((tm, width), lambda i: (i, 0))


def _two_array_specs(n_first, tm, width):
    return [pl.BlockSpec((tm, width), lambda i: (jnp.minimum(i, n_first - 1), 0)),
            pl.BlockSpec((tm, width), lambda i: (jnp.maximum(i - n_first, 0), 0))]


def _cast_stream_specs(weights, nsteps):
    in_specs, out_specs, out_shapes = [], [], []
    for w in weights:
        rows, cols = w.shape
        chunk = rows // nsteps
        assert chunk * nsteps == rows and chunk % BF16_SUBLANES == 0
        spec = pl.BlockSpec((chunk, cols), lambda i: (jnp.minimum(i, nsteps - 1), 0))
        in_specs.append(spec)
        out_specs.append(spec)
        out_shapes.append(jax.ShapeDtypeStruct(w.shape, BF16))
    return in_specs, out_specs, out_shapes


def _cast_chunks(src_refs, dst_refs):
    for src, dst in zip(src_refs, dst_refs, strict=True):
        dst[...] = src[...].astype(BF16)


def _store_by_step(is_first, first_ref, second_ref, value):
    @pl.when(is_first)
    def _():
        first_ref[...] = value

    @pl.when(jnp.logical_not(is_first))
    def _():
        second_ref[...] = value


def _ffn_residual(x, g_ref, wi_ref, wo_ref, between=()):
    h = _rms(x, g_ref[...]).astype(BF16)
    acc = None
    for c, (lo, hi) in enumerate(FFN_CHUNKS):
        a = _dot(h, wi_ref[:, lo:hi])
        b = _dot(h, wi_ref[:, D_FF + lo:D_FF + hi])
        part = _dot((a * jax.nn.sigmoid(a) * b).astype(BF16), wo_ref[lo:hi, :])
        acc = part if acc is None else acc + part
        if c < len(between):
            between[c]()
    return x + 0.5 * acc


def _ffn_first_kernel(xp_ref, xs_ref, g_ref, wi_ref, wo_ref, *rest, n_first, n_cast):
    cast_src, o_ref, cast_dst = rest[:n_cast], rest[n_cast], rest[n_cast + 1:]
    _cast_chunks(cast_src, cast_dst)
    x = jnp.where(pl.program_id(0) < n_first, xp_ref[...], xs_ref[...])
    o_ref[...] = _ffn_residual(x, g_ref, wi_ref, wo_ref)


def _ffn_first(xp, xs, g, wi, wo, cast_weights, *, tm):
    n_first, n_second = xp.shape[0] // tm, xs.shape[0] // tm
    c_in, c_out, c_shapes = _cast_stream_specs(cast_weights, n_first)
    return pl.pallas_call(
        functools.partial(_ffn_first_kernel, n_first=n_first, n_cast=len(cast_weights)),
        grid=(n_first + n_second,),
        in_specs=_two_array_specs(n_first, tm, D_MODEL)
                 + [_const_spec(g.shape), _const_spec(wi.shape), _const_spec(wo.shape)] + c_in,
        out_specs=[_row_spec(tm, D_MODEL)] + c_out,
        out_shape=[jax.ShapeDtypeStruct((xp.shape[0] + xs.shape[0], D_MODEL), F32)] + c_shapes,
        compiler_params=_cparams(("arbitrary",)),
        name="ffn",
    )(xp, xs, g, wi, wo, *cast_weights)


def _xattn_seq_probs(qb, mk):
    ts = qb.shape[0]
    krows = N_MEM * N_X_HEADS
    row = lax.broadcasted_iota(jnp.int32, (ts * N_X_HEADS, krows), 0)
    col = lax.broadcasted_iota(jnp.int32, (ts * N_X_HEADS, krows), 1)
    same_head = (row // ts) == (col & (N_X_HEADS - 1))
    qs = jnp.concatenate([qb[:, h * X_HEAD_DIM:(h + 1) * X_HEAD_DIM] for h in range(N_X_HEADS)], axis=0)
    s = jnp.where(same_head, _dot_nt(qs, mk.reshape(krows, X_HEAD_DIM).astype(BF16)), NEG_INF)
    p = jnp.exp(s - jnp.max(s, axis=-1, keepdims=True))
    return p.astype(BF16), jnp.sum(p, axis=-1, keepdims=True)


def _xattn_seq_values(p, denom, mv):
    ts = p.shape[0] // N_X_HEADS
    o = (_dot(p, mv.reshape(N_MEM * N_X_HEADS, X_HEAD_DIM).astype(BF16)) / denom).astype(BF16)
    return jnp.concatenate([o[h * ts:(h + 1) * ts] for h in range(N_X_HEADS)], axis=1)


def _tail_kernel(x_ref, oxp_ref, qxs_ref, mk_ref, mv_ref, wxo_ref, g_ref, wi_ref, wo_ref, gf_ref, yp_ref, ys_ref,
                 oxs_ref, *, n_first, nb, ts, tm):
    i = pl.program_id(0)
    is_first = i < n_first
    t_s = qxs_ref.shape[0]
    pair = 2 * ts

    @pl.when(i == 0)
    def _():
        oxs_ref[...] = jnp.zeros(oxs_ref.shape, oxs_ref.dtype)

    row0 = pl.multiple_of(jnp.maximum(i - n_first, 0) * tm, tm)
    ox = jnp.where(is_first, oxp_ref[...], oxs_ref[pl.ds(row0, tm), :])
    x = x_ref[...] + _dot(ox, wxo_ref[...])

    seq0 = jnp.minimum(i, n_first - 1) * nb
    pending = {}

    def probs_stage(j):
        src = pl.multiple_of((seq0 + 2 * j) * ts, pair)
        q2 = qxs_ref[pl.ds(src, pair), :]
        pending[j] = (src, [_xattn_seq_probs(q2[b * ts:(b + 1) * ts], mk_ref[2 * j + b]) for b in range(2)])

    def values_stage(j):
        src, probs = pending.pop(j)
        o2 = jnp.concatenate([_xattn_seq_values(p, denom, mv_ref[2 * j + b]) for b, (p, denom) in enumerate(probs)],
                             axis=0)
        dst = pl.multiple_of(jnp.where(is_first, src, t_s), pair)
        oxs_ref[pl.ds(dst, pair), :] = o2

    n_pairs = nb // 2
    assert 1 <= n_pairs < len(FFN_CHUNKS)

    def after_chunk(c):
        def run():
            values_stage(c)
            if c + 1 < n_pairs:
                probs_stage(c + 1)
        return run

    probs_stage(0)
    y = _rms(_ffn_residual(x, g_ref, wi_ref, wo_ref, between=[after_chunk(c) for c in range(n_pairs)]), gf_ref[...])
    _store_by_step(is_first, yp_ref, ys_ref, y)


def _tail(x2, ox_p, qx_s, mk, mv, wxo, g, wi, wo, gf, *, tm, ts):
    t_p, t_s, nseq = ox_p.shape[0], qx_s.shape[0], mk.shape[0]
    n_first, n_second = t_p // tm, t_s // tm
    nb = nseq // n_first
    assert nb * n_first == nseq and nb % 2 == 0 and nseq * ts == t_s
    mem = pl.BlockSpec((nb, N_MEM, N_X_HEADS, X_HEAD_DIM), lambda i: (jnp.minimum(i, n_first - 1), 0, 0, 0))
    return pl.pallas_call(
        functools.partial(_tail_kernel, n_first=n_first, nb=nb, ts=ts, tm=tm),
        grid=(n_first + n_second,),
        in_specs=[_row_spec(tm, D_MODEL), _two_array_specs(n_first, tm, D_MODEL)[0], _const_spec(qx_s.shape), mem, mem]
                 + [_const_spec(c.shape) for c in (wxo, g, wi, wo, gf)],
        out_specs=_two_array_specs(n_first, tm, D_MODEL),
        out_shape=[jax.ShapeDtypeStruct((t_p, D_MODEL), F32), jax.ShapeDtypeStruct((t_s, D_MODEL), F32)],
        scratch_shapes=[pltpu.VMEM((t_s + 2 * ts, D_MODEL), BF16)],
        compiler_params=_cparams(("arbitrary",)),
        name="tail",
    )(x2, ox_p, qx_s, mk, mv, wxo, g, wi, wo, gf)


def _inproj_kernel(x_ref, g_ref, w_ref, ca_ref, sa_ref, cb_ref, sb_ref, q_ref, k_ref, v_ref, u_ref, ga_ref, gb_ref):
    h = _rms(x_ref[...], g_ref[...]).astype(BF16)
    ca, sa, cb, sb = ca_ref[0], sa_ref[0], cb_ref[0], sb_ref[0]
    cos = ca * cb - sa * sb
    sin = sa * cb + ca * sb
    lane = lax.broadcasted_iota(jnp.int32, cos.shape, 1)
    first_half = (lane & (HEAD_DIM - 1)) < (HEAD_DIM // 2)

    def rope(xc):
        rot = jnp.where(first_half, pltpu.roll(xc, LANES - HEAD_DIM // 2, 1), pltpu.roll(xc, HEAD_DIM // 2, 1))
        return xc * cos + rot * sin

    ga_ref[...] = jax.nn.sigmoid(_dot(h, w_ref[:, OFF_GA:OFF_GB])).astype(BF16)
    gb_ref[...] = jax.nn.sigmoid(_dot(h, w_ref[:, OFF_GB:])).astype(BF16)
    yq = _dot(h, w_ref[:, OFF_Q:OFF_K])
    for c in range(ATTN_WIDTH // LANES):
        q_ref[:, c * LANES:(c + 1) * LANES] = (rope(yq[:, c * LANES:(c + 1) * LANES]) * (HEAD_DIM ** -0.5)).astype(BF16)
    ykvu = _dot(h, w_ref[:, OFF_K:OFF_GA])
    k_ref[...] = rope(ykvu[:, :KV_WIDTH])
    v_ref[...] = ykvu[:, KV_WIDTH:2 * KV_WIDTH]
    u_ref[...] = ykvu[:, 2 * KV_WIDTH:]


def _inproj(x, g, w, rope_tabs, *, tm, n_first):
    t = x.shape[0]
    row = lambda wd: _row_spec(tm, wd)
    step_tab = pl.BlockSpec((1, 1, LANES), lambda i: (i, 0, 0))
    row_tab = pl.BlockSpec((1, tm, LANES), lambda i: (jnp.minimum(i // n_first, 1), 0, 0))
    widths = (ATTN_WIDTH, KV_WIDTH, KV_WIDTH, SSM_WIDTH, D_MODEL, D_MODEL)
    dtypes = (BF16, F32, F32, F32, BF16, BF16)
    return pl.pallas_call(
        _inproj_kernel,
        grid=(t // tm,),
        in_specs=[row(D_MODEL), _const_spec((1, D_MODEL)), _const_spec((D_MODEL, IN_WIDTH)), step_tab, step_tab,
                  row_tab, row_tab],
        out_specs=[row(wd) for wd in widths],
        out_shape=[jax.ShapeDtypeStruct((t, wd), dt) for wd, dt in zip(widths, dtypes)],
        compiler_params=_cparams(("parallel",)),
        name="inproj",
    )(x, g, w, *rope_tabs)


def _rope_tables(step_pos, row_pos):
    half = HEAD_DIM // 2
    inv = ROPE_THETA ** (-jnp.arange(half, dtype=F32) / half)
    sign = jnp.concatenate([-jnp.ones((half,), F32), jnp.ones((half,), F32)])

    def tabs(pos):
        ang = pos.astype(F32)[..., None] * inv
        cos = jnp.tile(jnp.cos(ang), LANES // half)
        sin = jnp.tile(jnp.tile(jnp.sin(ang), 2) * sign, LANES // HEAD_DIM)
        return cos, sin

    ca, sa = tabs(step_pos[:, None])
    cb, sb = tabs(row_pos)
    return ca, sa, cb, sb


def _attn_prompt_kernel(sinks_ref, q_ref, kp_ref, kc_ref, vp_ref, vc_ref, *rest, n_cast):
    cast_src, o_ref, cast_dst = rest[:n_cast], rest[n_cast], rest[n_cast + 1:]
    _cast_chunks(cast_src, cast_dst)
    i = pl.program_id(0)
    r = lax.broadcasted_iota(jnp.int32, (WINDOW, 2 * WINDOW), 0)
    c = lax.broadcasted_iota(jnp.int32, (WINDOW, 2 * WINDOW), 1)
    band = (c >= r) & (c <= WINDOW + r)
    first_mask = band & ((c >= WINDOW) | (i > 0))
    kcat = jnp.concatenate([kp_ref[...], kc_ref[...]], axis=0)
    vcat = jnp.concatenate([vp_ref[...], vc_ref[...]], axis=0)
    kswap = pltpu.roll(kcat, HEAD_DIM, 1)
    vswap = pltpu.roll(vcat, HEAD_DIM, 1)
    lo = lax.broadcasted_iota(jnp.int32, kcat.shape, 1) < HEAD_DIM
    lo_q = lax.broadcasted_iota(jnp.int32, (WINDOW, LANES), 1) < HEAD_DIM
    ones = jnp.ones(kcat.shape, F32)
    for kv in range(N_KV_HEADS):
        k_own, k_other = (kcat, kswap) if kv == 0 else (kswap, kcat)
        v_own, v_other = (vcat, vswap) if kv == 0 else (vswap, vcat)
        k_half = (jnp.where(lo, k_own, 0.0).astype(BF16), jnp.where(lo, 0.0, k_other).astype(BF16))
        rhs_half = (jnp.concatenate([jnp.where(lo, v_own, 0.0), jnp.where(lo, ones, 0.0)], axis=1).astype(BF16),
                    jnp.concatenate([jnp.where(lo, 0.0, v_other), jnp.where(lo, 0.0, ones)], axis=1).astype(BF16))
        for j in range(ATTN_QBLOCKS):
            qrows = slice(j * WINDOW, (j + 1) * WINDOW)
            krows = slice(j * WINDOW, (j + 2) * WINDOW)
            mask = first_mask if j == 0 else band
            for c in range(Q_PER_KV // 2):
                tile = kv * (Q_PER_KV // 2) + c
                qt = q_ref[qrows, tile * LANES:(tile + 1) * LANES]
                acc = None
                sink_terms = []
                for side in range(2):
                    s = jnp.where(mask, _dot_nt(qt, k_half[side][krows]), NEG_INF)
                    sink = sinks_ref[2 * tile + side]
                    m = jnp.maximum(jnp.max(s, axis=-1, keepdims=True), sink)
                    pv = _dot(jnp.exp((s - m).astype(BF16)), rhs_half[side][krows])
                    acc = pv if acc is None else acc + pv
                    sink_terms.append(jnp.exp(sink - m))
                denom = acc[:, LANES:] + jnp.where(lo_q, sink_terms[0], sink_terms[1])
                o_ref[qrows, tile * LANES:(tile + 1) * LANES] = (acc[:, :LANES] / denom).astype(BF16)


def _attn_prompt(sinks, q, k, v, cast_weights, *, t):
    rows = ATTN_QBLOCKS * WINDOW
    cur = lambda w: pl.BlockSpec((rows, w), lambda i: (i, 0))
    prev = lambda w: pl.BlockSpec((WINDOW, w), lambda i: (jnp.maximum(i * ATTN_QBLOCKS - 1, 0), 0))
    c_in, c_out, c_shapes = _cast_stream_specs(cast_weights, t // rows)
    return pl.pallas_call(
        functools.partial(_attn_prompt_kernel, n_cast=len(cast_weights)),
        grid=(t // rows,),
        in_specs=[pl.BlockSpec(memory_space=pltpu.SMEM), cur(ATTN_WIDTH), prev(KV_WIDTH), cur(KV_WIDTH),
                  prev(KV_WIDTH), cur(KV_WIDTH)] + c_in,
        out_specs=[cur(ATTN_WIDTH)] + c_out,
        out_shape=[jax.ShapeDtypeStruct((t, ATTN_WIDTH), BF16)] + c_shapes,
        compiler_params=_cparams(("parallel",)),
        name="attn_prompt",
    )(sinks, q, k, k, v, v, *cast_weights)


def _attn_sample_kernel(sinks_ref, q_ref, k_ref, v_ref, ckt_ref, cvt_ref, o_ref, nkt_ref, nvt_ref, *, nb, ts):
    rows = nb * ts
    q = q_ref[...]
    pad = jnp.zeros((LANES - rows, KV_WIDTH), F32)
    knt = jnp.concatenate([k_ref[...], pad], axis=0).T
    vnt = jnp.concatenate([v_ref[...], pad], axis=0).T
    lane = lax.broadcasted_iota(jnp.int32, (HEAD_DIM, WINDOW), 1)
    is_new = lane >= WINDOW - ts
    for b in range(nb):
        knt_b = pltpu.roll(knt, (WINDOW - ts - b * ts) % LANES, 1)
        vnt_b = pltpu.roll(vnt, (WINDOW - ts - b * ts) % LANES, 1)
        for kv in range(N_KV_HEADS):
            ksl = slice(kv * HEAD_DIM, (kv + 1) * HEAD_DIM)
            nkt_ref[b, kv] = jnp.where(is_new, knt_b[ksl], pltpu.roll(ckt_ref[b, kv], WINDOW - ts, 1))
            nvt_ref[b, kv] = jnp.where(is_new, vnt_b[ksl], pltpu.roll(cvt_ref[b, kv], WINDOW - ts, 1))

    knt = knt[:, :rows].astype(BF16)
    vnt = vnt[:, :rows].astype(BF16)
    grows = Q_PER_KV * rows
    row1 = lax.broadcasted_iota(jnp.int32, (grows, nb * WINDOW), 0)
    col1 = lax.broadcasted_iota(jnp.int32, (grows, nb * WINDOW), 1)
    rseq1 = (row1 % rows) // ts
    mask_ctx = (rseq1 == col1 // WINDOW) & (col1 % WINDOW >= row1 % ts)
    row2 = lax.broadcasted_iota(jnp.int32, (grows, rows), 0)
    col2 = lax.broadcasted_iota(jnp.int32, (grows, rows), 1)
    mask_new = ((row2 % rows) // ts == col2 // ts) & (col2 % ts <= row2 % ts)
    ghead = lax.broadcasted_iota(jnp.int32, (grows, 1), 0) // rows
    for kv in range(N_KV_HEADS):
        ksl = slice(kv * HEAD_DIM, (kv + 1) * HEAD_DIM)
        qg = jnp.concatenate([q[:, (kv * Q_PER_KV + g) * HEAD_DIM:(kv * Q_PER_KV + g + 1) * HEAD_DIM]
                              for g in range(Q_PER_KV)], axis=0)
        kctx = jnp.concatenate([ckt_ref[b, kv] for b in range(nb)], axis=1).astype(BF16)
        vctx = jnp.concatenate([cvt_ref[b, kv] for b in range(nb)], axis=1).astype(BF16)
        s1 = jnp.where(mask_ctx, _dot(qg, kctx), NEG_INF)
        s2 = jnp.where(mask_new, _dot(qg, knt[ksl]), NEG_INF)
        sink = jnp.zeros((grows, 1), F32)
        for g in range(Q_PER_KV):
            sink = jnp.where(ghead == g, sinks_ref[kv * Q_PER_KV + g], sink)
        m = jnp.maximum(jnp.maximum(jnp.max(s1, axis=-1, keepdims=True), jnp.max(s2, axis=-1, keepdims=True)), sink)
        p1 = jnp.exp(s1 - m)
        p2 = jnp.exp(s2 - m)
        denom = jnp.sum(p1, axis=-1, keepdims=True) + jnp.sum(p2, axis=-1, keepdims=True) + jnp.exp(sink - m)
        o = (_dot_nt(p1.astype(BF16), vctx) + _dot_nt(p2.astype(BF16), vnt[ksl])) / denom
        for g in range(Q_PER_KV):
            h = kv * Q_PER_KV + g
            o_ref[:, h * HEAD_DIM:(h + 1) * HEAD_DIM] = o[g * rows:(g + 1) * rows].astype(BF16)


def _attn_sample(sinks, q, k, v, ckt, cvt, *, nb, ts, row0):
    nseq = ckt.shape[0]
    rows = nb * ts
    assert rows < LANES and row0 % rows == 0
    row_in = lambda w: pl.BlockSpec((rows, w), lambda i: (i + row0 // rows, 0))
    win = pl.BlockSpec((nb, N_KV_HEADS, HEAD_DIM, WINDOW), lambda i: (i, 0, 0, 0))
    win_shape = jax.ShapeDtypeStruct((nseq, N_KV_HEADS, HEAD_DIM, WINDOW), F32)
    return pl.pallas_call(
        functools.partial(_attn_sample_kernel, nb=nb, ts=ts),
        grid=(nseq // nb,),
        in_specs=[pl.BlockSpec(memory_space=pltpu.SMEM), row_in(ATTN_WIDTH), row_in(KV_WIDTH), row_in(KV_WIDTH),
                  win, win],
        out_specs=[_row_spec(rows, ATTN_WIDTH), win, win],
        out_shape=[jax.ShapeDtypeStruct((nseq * ts, ATTN_WIDTH), BF16), win_shape, win_shape],
        compiler_params=_cparams(("parallel",)),
        name="attn_sample",
    )(sinks, q, k, v, ckt, cvt)


def _ssm_prep_kernel(are_ref, aim_ref, ldt_ref, bre_ref, bim_ref, lre_ref, lim_ref, bbre_ref, bbim_ref):
    a_re = are_ref[...]
    a_im = aim_ref[...]
    dt = jnp.exp(ldt_ref[...])
    mag = jnp.exp(a_re * dt)
    lb_re = mag * jnp.cos(a_im * dt)
    lb_im = mag * jnp.sin(a_im * dt)
    den = a_re * a_re + a_im * a_im
    nr = lb_re - 1.0
    ni = lb_im
    k_re = (nr * a_re + ni * a_im) / den
    k_im = (ni * a_re - nr * a_im) / den
    lre_ref[...] = lb_re
    lim_ref[...] = lb_im
    b_re = bre_ref[...]
    b_im = bim_ref[...]
    bbre_ref[...] = k_re[:, None, :] * b_re - k_im[:, None, :] * b_im
    bbim_ref[...] = k_re[:, None, :] * b_im + k_im[:, None, :] * b_re


def _ssm_prep(a_re, a_im, log_dt, b_re, b_im):
    g, n, gs = b_re.shape
    sds = jax.ShapeDtypeStruct
    return pl.pallas_call(
        _ssm_prep_kernel,
        out_shape=[sds((g, n), F32), sds((g, n), F32), sds((g, gs, n), F32), sds((g, gs, n), F32)],
        name="ssm_prep",
    )(a_re, a_im, log_dt.reshape(g, 1), jnp.swapaxes(b_re, 1, 2), jnp.swapaxes(b_im, 1, 2))


def _block_diag_halves(m):
    g, a, b = m.shape
    gh = g // 2
    tiled = jnp.tile(m.reshape(2, gh * a, b), (1, 1, gh))
    row_group = lax.broadcasted_iota(jnp.int32, tiled.shape, 1) // a
    col_group = lax.broadcasted_iota(jnp.int32, tiled.shape, 2) // b
    return jnp.where(row_group == col_group, tiled, 0.0)


def _ssm_y(u, scan_block, cc_ref, d_ref):
    ys = []
    for half in range(2):
        hc = jnp.concatenate([scan_block(half, jj) for jj in range(BLK_PER_HALF)], axis=1).astype(BF16)
        ys.append(_dot(hc, cc_ref[half]))
    y = jnp.concatenate(ys, axis=1) + d_ref[...] * u
    return jax.nn.gelu(y).astype(BF16)


def _ssm_prompt_kernel(u_ref, bb_ref, cc_ref, lam_ref, d_ref, h0_ref, z_ref, hout_ref,
                       x0_ref, x1_ref, s0_ref, s1_ref, hst_ref):
    tb, pitch = SSM_TB, SSM_PITCH
    x_bufs = (x0_ref, x1_ref)
    s_bufs = (s0_ref, s1_ref)

    @pl.when(pl.program_id(0) == 0)
    def _():
        hst_ref[...] = h0_ref[...]

    def b_proj(k):
        ub = u_ref[k * tb:(k + 1) * tb, :].astype(BF16)
        for half in range(2):
            x = _dot(ub[:, half * HALF_U:(half + 1) * HALF_U], bb_ref[half])
            for jj in range(BLK_PER_HALF):
                x_bufs[k % 2][pl.ds((half * BLK_PER_HALF + jj) * pitch, tb), :] = x[:, jj * LANES:(jj + 1) * LANES]

    lam = [lam_ref[g] for g in range(4)]
    carry = [hst_ref[g] for g in range(4)]

    def scan(k):
        for t in range(tb):
            for half in range(2):
                hr, hi = carry[2 * half], carry[2 * half + 1]
                lr, li = lam[2 * half], lam[2 * half + 1]
                rows_re = pl.ds((half * BLK_PER_HALF) * pitch + t, SUBLANES, stride=pitch)
                rows_im = pl.ds((half * BLK_PER_HALF + SUBLANES) * pitch + t, SUBLANES, stride=pitch)
                nr = lr * hr - li * hi + x_bufs[k % 2][rows_re, :]
                ni = lr * hi + li * hr + x_bufs[k % 2][rows_im, :]
                s_bufs[k % 2][rows_re, :] = nr
                s_bufs[k % 2][rows_im, :] = ni
                carry[2 * half], carry[2 * half + 1] = nr, ni

    def c_proj(k):
        z_ref[k * tb:(k + 1) * tb, :] = _ssm_y(
            u_ref[k * tb:(k + 1) * tb, :],
            lambda half, jj: s_bufs[k % 2][pl.ds((half * BLK_PER_HALF + jj) * pitch, tb), :], cc_ref, d_ref)

    b_proj(0)
    for k in range(SSM_NB):
        if k + 1 < SSM_NB:
            b_proj(k + 1)
        scan(k)
        c_proj(k)
    for g in range(4):
        hst_ref[g] = carry[g]
        hout_ref[g] = carry[g]


def _ssm_prompt(u, bb, cc, lam, d, h0, *, t):
    rows = SSM_NB * SSM_TB
    return pl.pallas_call(
        _ssm_prompt_kernel,
        grid=(t // rows,),
        in_specs=[pl.BlockSpec((rows, SSM_WIDTH), lambda i: (i, 0)), _const_spec(bb.shape), _const_spec(cc.shape),
                  _const_spec(lam.shape), _const_spec(d.shape), _const_spec(h0.shape)],
        out_specs=[pl.BlockSpec((rows, SSM_WIDTH), lambda i: (i, 0)),
                   pl.BlockSpec((4, SUBLANES, LANES), lambda i: (0, 0, 0))],
        out_shape=[jax.ShapeDtypeStruct((t, SSM_WIDTH), BF16), jax.ShapeDtypeStruct((4, SUBLANES, LANES), F32)],
        scratch_shapes=[pltpu.VMEM((2 * BLK_PER_HALF * SSM_PITCH, LANES), F32)] * 4
                       + [pltpu.VMEM((4, SUBLANES, LANES), F32)],
        compiler_params=_cparams(("arbitrary",)),
        name="ssm_prompt",
    )(u, bb, cc, lam, d, h0)


def _ssm_sample_kernel(u_ref, bb_ref, cc_ref, lre_ref, lim_ref, d_ref, h0re_ref, h0im_ref,
                       z_ref, hre_ref, him_ref, xs_ref, *, nb, ts):
    u = u_ref[...]
    ub = u.astype(BF16)
    for half in range(2):
        x = _dot(ub[:, half * HALF_U:(half + 1) * HALF_U], bb_ref[half])
        for jj in range(BLK_PER_HALF):
            xs_ref[half * BLK_PER_HALF + jj] = x[:, jj * LANES:(jj + 1) * LANES]
    for cb in range(SSM_CH // LANES):
        half, jj = divmod(cb, SUBLANES)
        j_re = half * BLK_PER_HALF + jj
        j_im = j_re + SUBLANES
        csl = slice(cb * LANES, (cb + 1) * LANES)
        lr = lre_ref[:, csl]
        li = lim_ref[:, csl]
        hr = h0re_ref[:, csl]
        hi = h0im_ref[:, csl]
        for t in range(ts):
            rows = pl.ds(t, nb, stride=ts)
            nr = lr * hr - li * hi + xs_ref[j_re, rows, :]
            ni = lr * hi + li * hr + xs_ref[j_im, rows, :]
            xs_ref[j_re, rows, :] = nr
            xs_ref[j_im, rows, :] = ni
            hr, hi = nr, ni
        hre_ref[:, csl] = hr
        him_ref[:, csl] = hi
    z_ref[...] = _ssm_y(u, lambda half, jj: xs_ref[half * BLK_PER_HALF + jj], cc_ref, d_ref)


def _ssm_sample(u, bb, cc, lre, lim, d, h0re, h0im, *, nb, ts, row0):
    nseq = h0re.shape[0]
    rows = nb * ts
    assert row0 % rows == 0
    st = pl.BlockSpec((nb, SSM_CH), lambda i: (i, 0))
    return pl.pallas_call(
        functools.partial(_ssm_sample_kernel, nb=nb, ts=ts),
        grid=(nseq // nb,),
        in_specs=[pl.BlockSpec((rows, SSM_WIDTH), lambda i: (i + row0 // rows, 0)), _const_spec(bb.shape),
                  _const_spec(cc.shape),
                  _const_spec(lre.shape), _const_spec(lim.shape), _const_spec(d.shape), st, st],
        out_specs=[pl.BlockSpec((rows, SSM_WIDTH), lambda i: (i, 0)), st, st],
        out_shape=[jax.ShapeDtypeStruct((nseq * ts, SSM_WIDTH), BF16),
                   jax.ShapeDtypeStruct((nseq, SSM_CH), F32), jax.ShapeDtypeStruct((nseq, SSM_CH), F32)],
        scratch_shapes=[pltpu.VMEM((2 * BLK_PER_HALF, rows, LANES), F32)],
        compiler_params=_cparams(("parallel",)),
        name="ssm_sample",
    )(u, bb, cc, lre, lim, d, h0re, h0im)


def _xattn_heads(q, mk, mv):
    outs = []
    for h in range(N_X_HEADS):
        sl = slice(h * X_HEAD_DIM, (h + 1) * X_HEAD_DIM)
        s = _dot_nt(q[:, sl], mk[:, sl])
        p = jnp.exp(s - jnp.max(s, axis=-1, keepdims=True))
        outs.append(_dot(p.astype(BF16), mv[:, sl]) / jnp.sum(p, axis=-1, keepdims=True))
    return jnp.concatenate(outs, axis=1)


def _merge_kernel(x1_ref, op_ref, os_ref, zp_ref, zs_ref, ga_ref, gb_ref, wup_ref, wglu_ref, wout_ref, gx_ref,
                  wxq_ref, mk_ref, mv_ref, x2_ref, oxp_ref, qxs_ref, *, n_first):
    is_first = pl.program_id(0) < n_first
    o = jnp.where(is_first, op_ref[...], os_ref[...])
    z = jnp.where(is_first, zp_ref[...], zs_ref[...])
    glu_gate = jax.nn.sigmoid(_dot(z, wglu_ref[:, D_MODEL:]))
    yb = _dot(z, wglu_ref[:, :D_MODEL]) * glu_gate
    ya = _dot(o, wup_ref[...])
    merged = ga_ref[...].astype(F32) * ya + gb_ref[...].astype(F32) * yb
    x2 = x1_ref[...] + _dot(merged.astype(BF16), wout_ref[...])
    x2_ref[...] = x2
    hx = _rms(x2, gx_ref[...]).astype(BF16)
    qx = (_dot(hx, wxq_ref[...]) * (X_HEAD_DIM ** -0.5)).astype(BF16)
    ox = _xattn_heads(qx, mk_ref[...].astype(BF16), mv_ref[...].astype(BF16)).astype(BF16)

    @pl.when(is_first)
    def _():
        oxp_ref[...] = ox

    @pl.when(jnp.logical_not(is_first))
    def _():
        qxs_ref[...] = qx


def _merge(x1, o_p, o_s, z_p, z_s, ga, gb, wup, wglu, wout, gx, wxq, mk, mv, *, tm):
    n_first, n_second = o_p.shape[0] // tm, o_s.shape[0] // tm
    row = lambda w: _row_spec(tm, w)
    consts = [wup, wglu, wout, gx, wxq, mk, mv]
    return pl.pallas_call(
        functools.partial(_merge_kernel, n_first=n_first),
        grid=(n_first + n_second,),
        in_specs=[row(D_MODEL)] + _two_array_specs(n_first, tm, ATTN_WIDTH) + _two_array_specs(n_first, tm, SSM_WIDTH)
                 + [row(D_MODEL), row(D_MODEL)] + [_const_spec(c.shape) for c in consts],
        out_specs=[row(D_MODEL)] + _two_array_specs(n_first, tm, D_MODEL),
        out_shape=[jax.ShapeDtypeStruct(x1.shape, F32), jax.ShapeDtypeStruct((o_p.shape[0], D_MODEL), BF16),
                   jax.ShapeDtypeStruct((o_s.shape[0], D_MODEL), BF16)],
        compiler_params=_cparams(("arbitrary",)),
        name="merge",
    )(x1, o_p, o_s, z_p, z_s, ga, gb, *consts)


def _memkv_kernel(mem_ref, g_ref, wk_ref, wv_ref, mk_ref, mv_ref):
    mn = _rms(mem_ref[...], g_ref[...]).astype(BF16)
    mk_ref[...] = _dot(mn, wk_ref[...].astype(BF16))
    mv_ref[...] = _dot(mn, wv_ref[...].astype(BF16))


def _memkv(mem, g, wk, wv):
    sds = jax.ShapeDtypeStruct((mem.shape[0], D_MODEL), F32)
    return pl.pallas_call(_memkv_kernel, out_shape=[sds, sds], name="memkv",
                          compiler_params=pltpu.CompilerParams(vmem_limit_bytes=VMEM_LIMIT))(mem, g, wk, wv)


def kernel(x_prompt, x_sample, cache_win_k, cache_win_v, state_ssm_re, state_ssm_im, cache_mem_k, cache_mem_v, mem_prompt, g_ffn1, w_ffn1_in, w_ffn1_out, g_mix, w_in, attn_sinks, ssm_a_re, ssm_a_im, ssm_log_dt, ssm_b_re, ssm_b_im, ssm_c_re, ssm_c_im, ssm_d, w_attn_up, w_ssm_glu, w_out, g_xattn, g_mem, w_xq, w_xk, w_xv, w_xo, g_ffn2, w_ffn2_in, w_ffn2_out, g_final):
    assert x_prompt.shape[0] == 1 and g_ffn1.shape[0] == 1
    seq = x_prompt.shape[1]
    nseq, ts = x_sample.shape[0], x_sample.shape[1]
    past_len = PAST_LEN
    l = 0
    vec = lambda g: g[l].reshape(1, -1)
    w1i, w1o = w_ffn1_in[l].astype(BF16), w_ffn1_out[l].astype(BF16)
    gf = g_final.reshape(1, -1)
    sinks = attn_sinks[l]

    lam_re, lam_im, bbt_re, bbt_im = _ssm_prep(ssm_a_re[l], ssm_a_im[l], ssm_log_dt[l], ssm_b_re[l], ssm_b_im[l])
    bb = jnp.concatenate([_block_diag_halves(bbt_re), _block_diag_halves(bbt_im)], axis=-1).astype(BF16)
    ct_re = jnp.swapaxes(ssm_c_re[l], 1, 2)
    ct_im = jnp.swapaxes(ssm_c_im[l], 1, 2)
    cc = jnp.concatenate([_block_diag_halves(ct_re), -_block_diag_halves(ct_im)], axis=1).astype(BF16)
    d_skip = ssm_d[l].reshape(1, -1)
    lre16 = lam_re.reshape(2, SUBLANES, LANES)
    lim16 = lam_im.reshape(2, SUBLANES, LANES)
    lam_tm = jnp.stack([lre16[0], lim16[0], lre16[1], lim16[1]])
    lre_row = lam_re.reshape(1, SSM_CH)
    lim_row = lam_im.reshape(1, SSM_CH)

    mk_p, mv_p = _memkv(mem_prompt[0], vec(g_mem), w_xk[l], w_xv[l])

    tm = TM_MIX
    t_s = nseq * ts
    assert seq % tm == 0 and t_s % tm == 0 and tm % ts == 0
    n_p, n_s = seq // tm, t_s // tm
    rope_tabs = _rope_tables(jnp.concatenate([jnp.arange(n_p) * tm, jnp.full((n_s,), past_len)]),
                             jnp.stack([jnp.arange(tm), jnp.arange(tm) % ts]))

    x1, win = _ffn_first(x_prompt[0], x_sample.reshape(t_s, D_MODEL), vec(g_ffn1), w1i, w1o, [w_in[l]], tm=TM_FFN)
    q, k, v, u, ga, gb = _inproj(x1, vec(g_mix), win, rope_tabs, tm=tm, n_first=n_p)

    o_p, wup, wglu, wout, wxq, wxo, w2i, w2o = _attn_prompt(
        sinks, q, k, v, [w_attn_up[l], w_ssm_glu[l], w_out[l], w_xq[l], w_xo[l], w_ffn2_in[l], w_ffn2_out[l]], t=seq)
    nk_p, nv_p = k[seq - WINDOW:seq], v[seq - WINDOW:seq]
    to_t = lambda c: jnp.transpose(c[l], (0, 2, 3, 1))
    o_s, nkt, nvt = _attn_sample(sinks, q, k, v, to_t(cache_win_k), to_t(cache_win_v),
                                 nb=ATTN_SAMPLE_SEQS, ts=ts, row0=seq)
    nk_s, nv_s = jnp.transpose(nkt, (0, 3, 1, 2)), jnp.transpose(nvt, (0, 3, 1, 2))

    z_p, hout_p = _ssm_prompt(u, bb, cc, lam_tm, d_skip, jnp.zeros((4, SUBLANES, LANES), F32), t=seq)
    hre_p = jnp.concatenate([hout_p[0], hout_p[2]], axis=0)
    him_p = jnp.concatenate([hout_p[1], hout_p[3]], axis=0)
    z_s, hre_s, him_s = _ssm_sample(u, bb, cc, lre_row, lim_row, d_skip, state_ssm_re[l].reshape(nseq, SSM_CH),
                                    state_ssm_im[l].reshape(nseq, SSM_CH), nb=SSM_SAMPLE_SEQS, ts=ts, row0=seq)

    x2, ox_p, qx_s = _merge(x1, o_p, o_s, z_p, z_s, ga, gb, wup, wglu, wout, vec(g_xattn), wxq, mk_p, mv_p, tm=tm)
    y_p, y_s = _tail(x2, ox_p, qx_s, cache_mem_k[l], cache_mem_v[l], wxo, vec(g_ffn2), w2i, w2o, gf,
                     tm=TM_TAIL, ts=ts)

    kvshape = (1, 1, WINDOW, N_KV_HEADS, HEAD_DIM)
    stshape = (1, 1, N_SSM_GROUPS, SSM_STATE)
    memshape = (1, 1, N_MEM, N_X_HEADS, X_HEAD_DIM)
    return (y_p.reshape(1, seq, D_MODEL), y_s.reshape(nseq, ts, D_MODEL),
            nk_p.reshape(kvshape), nv_p.reshape(kvshape), hre_p.reshape(stshape), him_p.reshape(stshape),
            mk_p.reshape(memshape), mv_p.reshape(memshape),
            nk_s.reshape(1, nseq, WINDOW, N_KV_HEADS, HEAD_DIM), nv_s.reshape(1, nseq, WINDOW, N_KV_HEADS, HEAD_DIM),
            hre_s.reshape(1, nseq, N_SSM_GROUPS, SSM_STATE), him_s.reshape(1, nseq, N_SSM_GROUPS, SSM_STATE))
```
